```python
import jax, jax.numpy as jnp
from jax import lax
import numpy as np

D_MODEL = 1024
BATCH = 2
SEQ = 16384
DEPTH = 2
DEC_BATCH = 16
DEC_SEQ = 32
PAST_LEN = 4096

CHUNK = 64
Q_BLOCK = 128
KEY_BLOCK = 64
H_SB = 4
HD_SB = 128
H_RET = 4
DK_RET = 128
DV_RET = 256
W_SB = H_SB * HD_SB
WK_RET = H_RET * DK_RET
WV_RET = H_RET * DV_RET
D_FF = 2816
N_EXPERTS = 8
TOP_K = 2
D_FF_EXPERT = 704
N_DENSE = (DEPTH + 1) // 2
N_MOE = DEPTH // 2
ROPE_BASE = 10000.0
EPS = 1e-6
GN_EPS = 1e-5
SPLIT_POINTS = (W_SB, 2 * W_SB, 3 * W_SB,
                3 * W_SB + WK_RET, 3 * W_SB + 2 * WK_RET,
                3 * W_SB + 2 * WK_RET + WV_RET, 3 * W_SB + 2 * WK_RET + 2 * WV_RET,
                3 * W_SB + 2 * WK_RET + 2 * WV_RET + D_MODEL)
D_IN = 3 * W_SB + 2 * WK_RET + 2 * WV_RET + 2 * D_MODEL

kernel_name = "stickbreak_retention_hybrid_step"


def rms_norm(x, w):
    x32 = x.astype(jnp.float32)
    y = x32 * lax.rsqrt(jnp.mean(x32 * x32, axis=-1, keepdims=True) + EPS)
    return (y * w.astype(jnp.float32)).astype(x.dtype)


def rope(x, pos):
    half = x.shape[-1] // 2
    inv = ROPE_BASE ** (-jnp.arange(half, dtype=jnp.float32) / half)
    ang = pos[:, None] * inv[None, :]
    cos = jnp.cos(ang)[None, :, None, :]
    sin = jnp.sin(ang)[None, :, None, :]
    x32 = x.astype(jnp.float32)
    x1, x2 = x32[..., :half], x32[..., half:]
    return jnp.concatenate([x1 * cos - x2 * sin, x1 * sin + x2 * cos], axis=-1)


def sb_attend(q, k, v, q_pos, k_pos):
    B, Tq, H, _ = q.shape
    nk = k.shape[1] // KEY_BLOCK
    z = jnp.einsum('bqhd,bkhd->bhqk', q, k, preferred_element_type=jnp.float32)
    z = z.reshape(B, H, Tq, nk, KEY_BLOCK)
    mask = (k_pos[None, :] < q_pos[:, None]).reshape(Tq, nk, KEY_BLOCK)
    log_keep = jnp.where(mask, -jax.nn.softplus(z), 0.0)
    idx = jnp.arange(KEY_BLOCK)
    later_in_block = (idx[:, None] > idx[None, :]).astype(jnp.float32)
    within = jnp.einsum('bhqnj,js->bhqns', log_keep, later_in_block)
    blk = jnp.sum(log_keep, axis=-1)
    later_blocks = lax.cumsum(blk, axis=3, reverse=True) - blk
    expo = jnp.where(mask, z + log_keep + within + later_blocks[..., None], -jnp.inf)
    a = jnp.exp(expo).reshape(B, H, Tq, nk * KEY_BLOCK)
    return jnp.einsum('bhqk,bkhd->bqhd', a.astype(v.dtype), v)


def sb_prompt(q, k, v):
    T = q.shape[1]
    pos = jnp.arange(T)
    outs = []
    for i in range(T // Q_BLOCK):
        s, e = i * Q_BLOCK, (i + 1) * Q_BLOCK
        outs.append(sb_attend(q[:, s:e], k[:, :e], v[:, :e], pos[s:e], pos[:e]))
    return jnp.concatenate(outs, axis=1)


def sb_sample(q, k_cache, v_cache, k_new, v_new):
    T = q.shape[1]
    k_all = jnp.concatenate([k_cache.astype(k_new.dtype), k_new], axis=1)
    v_all = jnp.concatenate([v_cache.astype(v_new.dtype), v_new], axis=1)
    n_real = PAST_LEN + T
    pad = (-n_real) % KEY_BLOCK
    k_all = jnp.pad(k_all, ((0, 0), (0, pad), (0, 0), (0, 0)))
    v_all = jnp.pad(v_all, ((0, 0), (0, pad), (0, 0), (0, 0)))
    k_pos = jnp.concatenate([jnp.arange(n_real), jnp.full((pad,), n_real + 1, jnp.int32)])
    q_pos = PAST_LEN + jnp.arange(T)
    return sb_attend(q, k_all, v_all, q_pos, k_pos)


def retention_chunk(q, k, v, state, log_gamma):
    L = q.shape[1]
    n = jnp.arange(L, dtype=jnp.float32)
    diff = n[:, None] - n[None, :]
    causal = diff >= 0
    decay = jnp.where(causal[None], jnp.exp(jnp.where(causal, diff, 0.0)[None] * log_gamma[:, None, None]), 0.0)
    scores = jnp.einsum('bqhd,bkhd->bhqk', q, k) * decay[None]
    inner = jnp.einsum('bhqk,bkhe->bqhe', scores, v)
    q_decay = jnp.exp((n[:, None] + 1.0) * log_gamma[None, :])
    cross = jnp.einsum('bqhd,bhde->bqhe', q, state) * q_decay[None, :, :, None]
    k_decay = jnp.exp((L - 1.0 - n)[:, None] * log_gamma[None, :])
    new_state = (jnp.exp(L * log_gamma)[None, :, None, None] * state
                 + jnp.einsum('bkhd,bkhe->bhde', k * k_decay[None, :, :, None], v))
    return inner + cross, new_state


def retention_prompt(q, k, v, log_gamma):
    B, T, H, _ = q.shape
    nc = T // CHUNK

    def to_chunks(x):
        return jnp.moveaxis(x.reshape(B, nc, CHUNK, H, x.shape[-1]), 1, 0)

    def step(state, inp):
        qc, kc, vc = inp
        out, new = retention_chunk(qc, kc, vc, state, log_gamma)
        return new, out

    state0 = jnp.zeros((B, H, DK_RET, DV_RET), jnp.float32)
    final, outs = lax.scan(step, state0, (to_chunks(q), to_chunks(k), to_chunks(v)))
    return jnp.moveaxis(outs, 0, 1).reshape(B, T, H, DV_RET), final


def group_norm_heads(o, w):
    mu = jnp.mean(o, axis=-1, keepdims=True)
    var = jnp.mean(jnp.square(o - mu), axis=-1, keepdims=True)
    y = (o - mu) * lax.rsqrt(var + GN_EPS)
    return y.reshape(o.shape[0], o.shape[1], -1) * w.astype(jnp.float32)


def swiglu(x, w_gate, w_up, w_down):
    return (jax.nn.silu(x @ w_gate) * (x @ w_up)) @ w_down


def moe(x, router_w, w_gate, w_up, w_down):
    logits = (x @ router_w).astype(jnp.float32)
    top_val, top_idx = lax.top_k(logits, TOP_K)
    top_w = jax.nn.softmax(top_val, axis=-1)
    combine = jnp.sum(jax.nn.one_hot(top_idx, N_EXPERTS, dtype=jnp.float32) * top_w[..., None], axis=-2)
    y = jnp.zeros_like(x)
    for e in range(N_EXPERTS):
        y = y + combine[..., e:e + 1].astype(x.dtype) * swiglu(x, w_gate[e], w_up[e], w_down[e])
    return y


def setup_inputs(seed: int = 0) -> dict:
    key = jax.random.key(seed)
    ks = jax.random.split(key, 20)
    f32 = jnp.float32

    def nrm(k, shape, scale):
        return jax.random.normal(k, shape, f32) * scale

    return {
        "x_prompt": nrm(ks[0], (BATCH, SEQ, D_MODEL), 1.0),
        "x_sample": nrm(ks[1], (DEC_BATCH, DEC_SEQ, D_MODEL), 1.0),
        "cache_sb_k": nrm(ks[2], (DEPTH, DEC_BATCH, PAST_LEN, H_SB, HD_SB), 1.0),
        "cache_sb_v": nrm(ks[3], (DEPTH, DEC_BATCH, PAST_LEN, H_SB, HD_SB), 1.0),
        "state_ret": nrm(ks[4], (DEPTH, DEC_BATCH, H_RET, DK_RET, DV_RET), 0.5),
        "norm_mix_w": 1.0 + nrm(ks[5], (DEPTH, D_MODEL), 0.01),
        "w_in": nrm(ks[6], (DEPTH, D_MODEL, D_IN), D_MODEL ** -0.5),
        "w_sb_o": nrm(ks[7], (DEPTH, W_SB, D_MODEL), W_SB ** -0.5),
        "w_ret_o": nrm(ks[8], (DEPTH, WV_RET, D_MODEL), WV_RET ** -0.5),
        "gn_ret_w": 1.0 + nrm(ks[9], (DEPTH, WV_RET), 0.01),
        "w_out": nrm(ks[10], (DEPTH, D_MODEL, D_MODEL), D_MODEL ** -0.5),
        "norm_ffn_w": 1.0 + nrm(ks[11], (DEPTH, D_MODEL), 0.01),
        "ffn_w_gate": nrm(ks[12], (N_DENSE, D_MODEL, D_FF), D_MODEL ** -0.5),
        "ffn_w_up": nrm(ks[13], (N_DENSE, D_MODEL, D_FF), D_MODEL ** -0.5),
        "ffn_w_down": nrm(ks[14], (N_DENSE, D_FF, D_MODEL), D_FF ** -0.5),
        "router_w": nrm(ks[15], (N_MOE, D_MODEL, N_EXPERTS), D_MODEL ** -0.5),
        "moe_w_gate": nrm(ks[16], (N_MOE, N_EXPERTS, D_MODEL, D_FF_EXPERT), D_MODEL ** -0.5),
        "moe_w_up": nrm(ks[17], (N_MOE, N_EXPERTS, D_MODEL, D_FF_EXPERT), D_MODEL ** -0.5),
        "moe_w_down": nrm(ks[18], (N_MOE, N_EXPERTS, D_FF_EXPERT, D_MODEL), D_FF_EXPERT ** -0.5),
        "norm_final_w": 1.0 + nrm(ks[19], (D_MODEL,), 0.01),
    }


def reference(x_prompt, x_sample, cache_sb_k, cache_sb_v, state_ret,
              norm_mix_w, w_in, w_sb_o, w_ret_o, gn_ret_w, w_out, norm_ffn_w,
              ffn_w_gate, ffn_w_up, ffn_w_down, router_w, moe_w_gate, moe_w_up,
              moe_w_down, norm_final_w):
    log_gamma = jnp.log1p(-jnp.exp2(-5.0 - jnp.arange(H_RET, dtype=jnp.float32)))

    def run(x, pos0, is_prompt):
        B, T, _ = x.shape
        pos_f = pos0 + jnp.arange(T, dtype=jnp.float32)
        h = x
        new_k, new_v, new_r = [], [], []
        for l in range(DEPTH):
            hn = rms_norm(h, norm_mix_w[l])
            proj = hn @ w_in[l]
            q_sb, k_sb, v_sb, q_r, k_r, v_r, g_r, a_sb, a_r = jnp.split(proj, SPLIT_POINTS, axis=-1)
            q_sb = q_sb.reshape(B, T, H_SB, HD_SB) * (HD_SB ** -0.5)
            k_sb = k_sb.reshape(B, T, H_SB, HD_SB)
            v_sb = v_sb.reshape(B, T, H_SB, HD_SB)
            if is_prompt:
                o_sb = sb_prompt(q_sb, k_sb, v_sb)
            else:
                o_sb = sb_sample(q_sb, cache_sb_k[l], cache_sb_v[l], k_sb, v_sb)
            qr = rope(q_r.reshape(B, T, H_RET, DK_RET), pos_f)
            kr = rope(k_r.reshape(B, T, H_RET, DK_RET), pos_f) * (DK_RET ** -0.5)
            vr = v_r.reshape(B, T, H_RET, DV_RET).astype(jnp.float32)
            if is_prompt:
                o_r, r_fin = retention_prompt(qr, kr, vr, log_gamma)
            else:
                o_r, r_fin = retention_chunk(qr, kr, vr, state_ret[l].astype(jnp.float32), log_gamma)
            o_r = (jax.nn.silu(g_r.astype(jnp.float32)) * group_norm_heads(o_r, gn_ret_w[l])).astype(h.dtype)
            b_sb = o_sb.reshape(B, T, W_SB) @ w_sb_o[l]
            b_r = o_r @ w_ret_o[l]
            merged = jax.nn.sigmoid(a_sb) * b_sb + jax.nn.sigmoid(a_r) * b_r
            h = h + merged @ w_out[l]
            hn = rms_norm(h, norm_ffn_w[l])
            if l % 2 == 0:
                h = h + swiglu(hn, ffn_w_gate[l // 2], ffn_w_up[l // 2], ffn_w_down[l // 2])
            else:
                h = h + moe(hn, router_w[l // 2], moe_w_gate[l // 2], moe_w_up[l // 2], moe_w_down[l // 2])
            new_k.append(k_sb)
            new_v.append(v_sb)
            new_r.append(r_fin.astype(x.dtype))
        return rms_norm(h, norm_final_w), jnp.stack(new_k), jnp.stack(new_v), jnp.stack(new_r)

    y_prompt, k_p, v_p, r_p = run(x_prompt, 0.0, True)
    y_sample, k_s, v_s, r_s = run(x_sample, float(PAST_LEN), False)
    return (y_prompt, y_sample, k_p, v_p, r_p, k_s, v_s, r_s)
```

```python
import functools
import math

import jax
import jax.numpy as jnp
from jax import lax
from jax.experimental import pallas as pl
from jax.experimental.pallas import tpu as pltpu

F32 = jnp.float32
BF16 = jnp.bfloat16

H_SB = 4
H_RET = 4
TOP_K = 2
ROPE_BASE = 10000.0
EPS = 1e-6
GN_EPS = 1e-5
RET_CHUNK_MAX = 256

LANES = 128
V7X_VMEM_BYTES = 64 * 1024 * 1024
VMEM_CAP_BYTES = V7X_VMEM_BYTES - 8 * 1024 * 1024


def _cparams(sem, est_bytes):
    limit = int(min(VMEM_CAP_BYTES, max(32 * 1024 * 1024, 2 * est_bytes)))
    return pltpu.CompilerParams(dimension_semantics=sem, vmem_limit_bytes=limit)


def _row_tile(n, want):
    t = min(n, want)
    assert n % t == 0, (n, t)
    return t


def _rms_scale(x, w):
    ms = jnp.mean(x * x, axis=-1, keepdims=True)
    return x * lax.rsqrt(ms + EPS) * w


def _inproj_kernel(x_ref, nw_ref, w_ref, *rest, q_scale, n_alias):
    proj_ref, k_ref, v_ref, hn_ref = rest[n_alias:]
    j = pl.program_id(1)

    @pl.when(j == 0)
    def _():
        hn_ref[...] = _rms_scale(x_ref[...], nw_ref[...]).astype(BF16)

    acc = jnp.dot(hn_ref[...], w_ref[...], preferred_element_type=F32)
    scale = jnp.where(j == 0, q_scale, 1.0).astype(F32)
    proj_ref[...] = (acc * scale).astype(BF16)

    @pl.when(j == 1)
    def _():
        k_ref[...] = acc

    @pl.when(j == 2)
    def _():
        v_ref[...] = acc


def _in_projection(x, norm_w, w_bf, layer, depth, kv_prev, *, w_sb):
    n, d = x.shape
    d_in = w_bf.shape[1]
    tn = w_sb
    assert d_in % tn == 0
    tm = _row_tile(n, 1024)
    hd = w_sb // H_SB
    grid = (n // tm, d_in // tn)
    kv_shape = jax.ShapeDtypeStruct((depth, n, w_sb), F32)
    kv_spec = pl.BlockSpec((None, tm, w_sb), lambda i, j: (layer, i, 0))
    in_specs = [
        pl.BlockSpec((tm, d), lambda i, j: (i, 0)),
        pl.BlockSpec((1, d), lambda i, j: (0, 0)),
        pl.BlockSpec((d, tn), lambda i, j: (0, j)),
    ]
    args = [x, norm_w.reshape(1, d), w_bf]
    aliases = {}
    n_alias = 0
    if kv_prev is not None:
        in_specs += [pl.BlockSpec(memory_space=pl.ANY), pl.BlockSpec(memory_space=pl.ANY)]
        args += [kv_prev[0], kv_prev[1]]
        aliases = {3: 1, 4: 2}
        n_alias = 2
    est = 2 * tm * d * 4 + tm * d * 2 + 2 * d * tn * 2 + 2 * tm * tn * 2 + 4 * tm * w_sb * 4 + tm * tn * 4
    proj, k_all, v_all = pl.pallas_call(
        functools.partial(_inproj_kernel, q_scale=hd ** -0.5, n_alias=n_alias),
        grid=grid,
        in_specs=in_specs,
        out_specs=[pl.BlockSpec((tm, tn), lambda i, j: (i, j)), kv_spec, kv_spec],
        out_shape=[jax.ShapeDtypeStruct((n, d_in), BF16), kv_shape, kv_shape],
        scratch_shapes=[pltpu.VMEM((tm, d), BF16)],
        input_output_aliases=aliases,
        compiler_params=_cparams(("parallel", "arbitrary"), est),
        name=f"in_projection_l{layer}",
    )(*args)
    return proj, k_all, v_all


def _cumsum_matrix():
    j = jnp.arange(LANES)[:, None]
    s = jnp.arange(LANES)[None, :]
    tri = (j >= s).astype(BF16)
    return jnp.concatenate([tri, jnp.ones((LANES, LANES), BF16)], axis=1)


def _sb_group(q, k, v, r, cum, mask):
    z = lax.dot_general(q, k, (((1,), (1,)), ((), ())), preferred_element_type=F32)
    sp = jnp.maximum(z, 0.0) + jnp.log(1.0 + jnp.exp(-jnp.abs(z)))
    if mask is not None:
        sp = jnp.where(mask, sp, 0.0)
    hi = sp.astype(BF16)
    lo = (sp - hi.astype(F32)).astype(BF16)
    ct = jnp.dot(hi, cum, preferred_element_type=F32) + jnp.dot(lo, cum, preferred_element_type=F32)
    a = jnp.exp(z - ct[:, :LANES] - r)
    if mask is not None:
        a = jnp.where(mask, a, 0.0)
    pv = jnp.dot(a.astype(BF16), v, preferred_element_type=F32)
    return pv, r + ct[:, LANES:]


def _sb_prompt_kernel(q_ref, k_ref, v_ref, cum_ref, o_ref, r_ref, acc_ref, *, qb):
    i = pl.program_id(2)
    q = q_ref[...]
    cum = cum_ref[...]
    groups = qb // LANES
    r_ref[...] = jnp.zeros_like(r_ref)
    acc_ref[...] = jnp.zeros_like(acc_ref)
    row = lax.broadcasted_iota(jnp.int32, (qb, LANES), 0)
    col = lax.broadcasted_iota(jnp.int32, (qb, LANES), 1)

    def step(start, mask):
        k = k_ref[pl.ds(start, LANES), :]
        v = v_ref[pl.ds(start, LANES), :]
        pv, r_new = _sb_group(q, k, v, r_ref[...], cum, mask)
        acc_ref[...] += pv
        r_ref[...] = r_new

    for g in reversed(range(groups)):
        step(pl.multiple_of(i * qb + g * LANES, LANES), (col + g * LANES) < row)

    n_prev = i * groups

    def body(t, carry):
        step(pl.multiple_of((n_prev - 1 - t) * LANES, LANES), None)
        return carry

    lax.fori_loop(0, n_prev, body, 0)
    o_ref[...] = acc_ref[...].astype(o_ref.dtype)


def _sb_attention_prompt(proj, batch, seq, *, w_sb):
    n = proj.shape[0]
    hd = w_sb // H_SB
    assert hd == LANES
    qb = 256 if seq % 256 == 0 else LANES
    assert seq % qb == 0
    nq = seq // qb
    est = 2 * qb * hd * 2 + 4 * seq * hd * 2 + 2 * qb * hd * 2 + 2 * qb * LANES * 4 + 16 * qb * LANES * 4
    return pl.pallas_call(
        functools.partial(_sb_prompt_kernel, qb=qb),
        grid=(batch, H_SB, nq),
        in_specs=[
            pl.BlockSpec((qb, hd), lambda b, h, i: (b * nq + i, h)),
            pl.BlockSpec((seq, hd), lambda b, h, i: (b, H_SB + h)),
            pl.BlockSpec((seq, hd), lambda b, h, i: (b, 2 * H_SB + h)),
            pl.BlockSpec((LANES, 2 * LANES), lambda b, h, i: (0, 0)),
        ],
        out_specs=pl.BlockSpec((qb, hd), lambda b, h, i: (b * nq + i, h)),
        out_shape=jax.ShapeDtypeStruct((n, w_sb), BF16),
        scratch_shapes=[pltpu.VMEM((qb, LANES), F32), pltpu.VMEM((qb, LANES), F32)],
        compiler_params=_cparams(("parallel", "parallel", "arbitrary"), est),
        name="sb_attention_prompt",
    )(proj, proj, proj, _cumsum_matrix())


def _sb_sample_kernel(q_ref, kn_ref, vn_ref, kc_ref, vc_ref, cum_ref, o_ref, r_ref, acc_ref, *, tq, past):
    q = q_ref[...]
    cum = cum_ref[...]
    r_ref[...] = jnp.zeros_like(r_ref)
    acc_ref[...] = jnp.zeros_like(acc_ref)
    row = lax.broadcasted_iota(jnp.int32, (tq, LANES), 0)
    col = lax.broadcasted_iota(jnp.int32, (tq, LANES), 1)

    pv, r_new = _sb_group(q, kn_ref[...], vn_ref[...], r_ref[...], cum, col < row)
    acc_ref[...] += pv
    r_ref[...] = r_new

    n_prev = past // LANES

    def body(t, carry):
        start = pl.multiple_of((n_prev - 1 - t) * LANES, LANES)
        k = kc_ref[pl.ds(start, LANES), :].astype(BF16)
        v = vc_ref[pl.ds(start, LANES), :].astype(BF16)
        pv, r_new = _sb_group(q, k, v, r_ref[...], cum, None)
        acc_ref[...] += pv
        r_ref[...] = r_new
        return carry

    lax.fori_loop(0, n_prev, body, 0)
    o_ref[...] = acc_ref[...].astype(o_ref.dtype)


def _sb_attention_sample(proj, k_cache, v_cache, batch, tq, *, w_sb):
    n = proj.shape[0]
    hd = w_sb // H_SB
    past = k_cache.shape[1]
    assert hd == LANES and past % LANES == 0 and tq <= LANES
    kv_new = proj[:, w_sb:3 * w_sb].reshape(batch, tq, 2 * w_sb)
    kv_new = jnp.pad(kv_new, ((0, 0), (0, LANES - tq), (0, 0)))
    est = 4 * past * hd * 4 + 8 * LANES * hd * 2 + 24 * tq * LANES * 4
    return pl.pallas_call(
        functools.partial(_sb_sample_kernel, tq=tq, past=past),
        grid=(batch, H_SB),
        in_specs=[
            pl.BlockSpec((tq, hd), lambda b, h: (b, h)),
            pl.BlockSpec((None, LANES, hd), lambda b, h: (b, 0, h)),
            pl.BlockSpec((None, LANES, hd), lambda b, h: (b, 0, H_SB + h)),
            pl.BlockSpec((None, past, hd), lambda b, h: (b, 0, h)),
            pl.BlockSpec((None, past, hd), lambda b, h: (b, 0, h)),
            pl.BlockSpec((LANES, 2 * LANES), lambda b, h: (0, 0)),
        ],
        out_specs=pl.BlockSpec((tq, hd), lambda b, h: (b, h)),
        out_shape=jax.ShapeDtypeStruct((n, w_sb), BF16),
        scratch_shapes=[pltpu.VMEM((tq, LANES), F32), pltpu.VMEM((tq, LANES), F32)],
        compiler_params=_cparams(("parallel", "parallel"), est),
        name="sb_attention_sample",
    )(proj, kv_new, kv_new, k_cache, v_cache, _cumsum_matrix())


def _retention_kernel(dec_ref, q_ref, k_ref, v_ref, g_ref, cos_ref, sin_ref, gnw_ref, s0_ref,
                      o_ref, sout_ref, s_ref, d_ref, qd_ref, kd_ref, *, blk, k_scale):
    h = pl.program_id(1)
    c = pl.program_id(2)
    lg = dec_ref[0, h]
    g_blk = dec_ref[1, h]

    @pl.when(c == 0)
    def _():
        s_ref[...] = s0_ref[...]
        n = lax.broadcasted_iota(jnp.int32, (blk, blk), 0)
        m = lax.broadcasted_iota(jnp.int32, (blk, blk), 1)
        causal = n >= m
        diff = jnp.where(causal, n - m, 0).astype(F32)
        d_ref[...] = jnp.where(causal, jnp.exp(diff * lg), 0.0)
        rows = lax.broadcasted_iota(jnp.int32, (blk, LANES), 0).astype(F32)
        qd_ref[...] = jnp.exp((rows + 1.0) * lg)
        kd_ref[...] = jnp.exp((blk - 1.0 - rows) * lg)

    cos = cos_ref[...]
    sin = sin_ref[...]
    half = cos.shape[1] // 2
    qf = q_ref[...].astype(F32)
    kf = k_ref[...].astype(F32)
    qr = qf * cos + pltpu.roll(qf, half, 1) * sin
    kr = (kf * cos + pltpu.roll(kf, half, 1) * sin) * k_scale
    v = v_ref[...]

    scores = lax.dot_general(qr.astype(BF16), kr.astype(BF16), (((1,), (1,)), ((), ())),
                             preferred_element_type=F32) * d_ref[...]
    inner = jnp.dot(scores.astype(BF16), v, preferred_element_type=F32)
    s = s_ref[...]
    cross = jnp.dot((qr * qd_ref[...]).astype(BF16), s.astype(BF16), preferred_element_type=F32)
    o = inner + cross
    kv = lax.dot_general((kr * kd_ref[...]).astype(BF16), v, (((0,), (0,)), ((), ())),
                         preferred_element_type=F32)
    s_new = g_blk * s + kv
    s_ref[...] = s_new
    sout_ref[...] = s_new

    mu = jnp.mean(o, axis=-1, keepdims=True)
    dev = o - mu
    var = jnp.mean(dev * dev, axis=-1, keepdims=True)
    y = dev * lax.rsqrt(var + GN_EPS) * gnw_ref[...]
    gf = g_ref[...].astype(F32)
    o_ref[...] = (gf * jax.nn.sigmoid(gf) * y).astype(o_ref.dtype)


def _retention(proj, gn_w, state0, batch, seq, pos0, *, w_sb, wk_ret, wv_ret):
    n = proj.shape[0]
    dk = wk_ret // H_RET
    dv = wv_ret // H_RET
    assert dk == LANES and dv % LANES == 0
    blk = min(seq, RET_CHUNK_MAX)
    assert seq % blk == 0
    nc = seq // blk
    q0 = 3 * w_sb // dk
    k0 = (3 * w_sb + wk_ret) // dk
    v0 = (3 * w_sb + 2 * wk_ret) // dv
    g0 = (3 * w_sb + 2 * wk_ret + wv_ret) // dv
    assert (3 * w_sb + 2 * wk_ret) % dv == 0

    half = dk // 2
    inv = ROPE_BASE ** (-jnp.arange(half, dtype=F32) / half)
    ang = (pos0 + jnp.arange(seq, dtype=F32))[:, None] * inv[None, :]
    cos = jnp.concatenate([jnp.cos(ang), jnp.cos(ang)], axis=1)
    sin = jnp.concatenate([-jnp.sin(ang), jnp.sin(ang)], axis=1)
    log_gamma = jnp.log1p(-jnp.exp2(-5.0 - jnp.arange(H_RET, dtype=F32)))
    dec = jnp.stack([log_gamma, jnp.exp(blk * log_gamma)])

    est = 2 * blk * (2 * dk + 2 * dv) * 2 + 4 * blk * dk * 4 + 4 * dk * dv * 4 + 2 * blk * dv * 2 \
        + dk * dv * 4 + blk * blk * 4 + 2 * blk * LANES * 4 + 12 * blk * dv * 4
    row = lambda b, h, c: b * nc + c
    return pl.pallas_call(
        functools.partial(_retention_kernel, blk=blk, k_scale=dk ** -0.5),
        grid=(batch, H_RET, nc),
        in_specs=[
            pl.BlockSpec(memory_space=pltpu.SMEM),
            pl.BlockSpec((blk, dk), lambda b, h, c: (row(b, h, c), q0 + h)),
            pl.BlockSpec((blk, dk), lambda b, h, c: (row(b, h, c), k0 + h)),
            pl.BlockSpec((blk, dv), lambda b, h, c: (row(b, h, c), v0 + h)),
            pl.BlockSpec((blk, dv), lambda b, h, c: (row(b, h, c), g0 + h)),
            pl.BlockSpec((blk, dk), lambda b, h, c: (c, 0)),
            pl.BlockSpec((blk, dk), lambda b, h, c: (c, 0)),
            pl.BlockSpec((1, dv), lambda b, h, c: (0, h)),
            pl.BlockSpec((None, None, dk, dv), lambda b, h, c: (b, h, 0, 0)),
        ],
        out_specs=[
            pl.BlockSpec((blk, dv), lambda b, h, c: (row(b, h, c), h)),
            pl.BlockSpec((None, None, dk, dv), lambda b, h, c: (b, h, 0, 0)),
        ],
        out_shape=[jax.ShapeDtypeStruct((n, wv_ret), BF16),
                   jax.ShapeDtypeStruct((batch, H_RET, dk, dv), F32)],
        scratch_shapes=[pltpu.VMEM((dk, dv), F32), pltpu.VMEM((blk, blk), F32),
                        pltpu.VMEM((blk, LANES), F32), pltpu.VMEM((blk, LANES), F32)],
        compiler_params=_cparams(("parallel", "parallel", "arbitrary"), est),
        name="retention",
    )(dec, proj, proj, proj, proj, cos, sin, gn_w.reshape(1, wv_ret), state0)


def _merge_kernel(osb_ref, or_ref, asb0_ref, asb1_ref, ar0_ref, ar1_ref, h_ref,
                  wsb_ref, wr_ref, wo_ref, out_ref):
    b_sb = jnp.dot(osb_ref[...], wsb_ref[...], preferred_element_type=F32)
    b_r = jnp.dot(or_ref[...], wr_ref[...], preferred_element_type=F32)
    a_sb = jnp.concatenate([asb0_ref[...], asb1_ref[...]], axis=1).astype(F32)
    a_r = jnp.concatenate([ar0_ref[...], ar1_ref[...]], axis=1).astype(F32)
    merged = jax.nn.sigmoid(a_sb) * b_sb + jax.nn.sigmoid(a_r) * b_r
    out_ref[...] = h_ref[...] + jnp.dot(merged.astype(BF16), wo_ref[...], preferred_element_type=F32)


def _merge(o_sb, o_r, proj, h, w_sb_o, w_ret_o, w_out, *, w_sb, wk_ret, wv_ret):
    n, d = h.shape
    tm = _row_tile(n, 512)
    half = d // 2
    a0 = (3 * w_sb + 2 * wk_ret + 2 * wv_ret) // half
    assert (3 * w_sb + 2 * wk_ret + 2 * wv_ret) % half == 0
    gate = lambda t: pl.BlockSpec((tm, half), lambda i: (i, a0 + t))
    full = lambda a: pl.BlockSpec(a.shape, lambda i: (0, 0))
    est = 2 * tm * (w_sb + wv_ret + 2 * d) * 2 + 4 * tm * d * 4 + 2 * (w_sb + wv_ret + d) * d * 2 + 6 * tm * d * 4
    return pl.pallas_call(
        _merge_kernel,
        grid=(n // tm,),
        in_specs=[
            pl.BlockSpec((tm, w_sb), lambda i: (i, 0)),
            pl.BlockSpec((tm, wv_ret), lambda i: (i, 0)),
            gate(0), gate(1), gate(2), gate(3),
            pl.BlockSpec((tm, d), lambda i: (i, 0)),
            full(w_sb_o), full(w_ret_o), full(w_out),
        ],
        out_specs=pl.BlockSpec((tm, d), lambda i: (i, 0)),
        out_shape=jax.ShapeDtypeStruct((n, d), F32),
        compiler_params=_cparams(("parallel",), est),
        name="merge",
    )(o_sb, o_r, proj, proj, proj, proj, h, w_sb_o, w_ret_o, w_out)


def _finish(h, acc, fw_ref, out_ref):
    y = h + acc
    if fw_ref is not None:
        y = _rms_scale(y, fw_ref[...])
    out_ref[...] = y


def _ffn_kernel(h_ref, nw_ref, wg_ref, wu_ref, wd_ref, *rest, final):
    fw_ref = rest[0] if final else None
    out_ref, hn_ref, acc_ref = rest[1:] if final else rest
    f = pl.program_id(1)

    @pl.when(f == 0)
    def _():
        hn_ref[...] = _rms_scale(h_ref[...], nw_ref[...]).astype(BF16)
        acc_ref[...] = jnp.zeros_like(acc_ref)

    hn = hn_ref[...]
    g = jnp.dot(hn, wg_ref[...], preferred_element_type=F32)
    u = jnp.dot(hn, wu_ref[...], preferred_element_type=F32)
    act = (g * jax.nn.sigmoid(g) * u).astype(BF16)
    acc_ref[...] += jnp.dot(act, wd_ref[...], preferred_element_type=F32)

    @pl.when(f == pl.num_programs(1) - 1)
    def _():
        _finish(h_ref[...], acc_ref[...], fw_ref, out_ref)


def _ffn(h, norm_w, wg, wu, wd, final_w):
    n, d = h.shape
    ff = wg.shape[1]
    tm = _row_tile(n, 512)
    tf = ff // 2 if (ff // 2) % LANES == 0 else ff
    final = final_w is not None
    in_specs = [
        pl.BlockSpec((tm, d), lambda i, f: (i, 0)),
        pl.BlockSpec((1, d), lambda i, f: (0, 0)),
        pl.BlockSpec((d, tf), lambda i, f: (0, f)),
        pl.BlockSpec((d, tf), lambda i, f: (0, f)),
        pl.BlockSpec((tf, d), lambda i, f: (f, 0)),
    ]
    args = [h, norm_w.reshape(1, d), wg, wu, wd]
    if final:
        in_specs.append(pl.BlockSpec((1, d), lambda i, f: (0, 0)))
        args.append(final_w.reshape(1, d))
    est = 4 * tm * d * 4 + tm * d * 6 + 6 * d * tf * 2 + 4 * tm * tf * 4
    return pl.pallas_call(
        functools.partial(_ffn_kernel, final=final),
        grid=(n // tm, ff // tf),
        in_specs=in_specs,
        out_specs=pl.BlockSpec((tm, d), lambda i, f: (i, 0)),
        out_shape=jax.ShapeDtypeStruct((n, d), F32),
        scratch_shapes=[pltpu.VMEM((tm, d), BF16), pltpu.VMEM((tm, d), F32)],
        compiler_params=_cparams(("parallel", "arbitrary"), est),
        name="ffn_dense",
    )(*args)


def _moe_kernel(h_ref, nw_ref, rw_ref, wg_ref, wu_ref, wd_ref, *rest, final, n_exp):
    fw_ref = rest[0] if final else None
    out_ref, hn_ref, acc_ref, comb_ref = rest[1:] if final else rest
    e = pl.program_id(1)

    @pl.when(e == 0)
    def _():
        hn = _rms_scale(h_ref[...], nw_ref[...])
        hn_ref[...] = hn.astype(BF16)
        acc_ref[...] = jnp.zeros_like(acc_ref)
        logits = jnp.dot(hn, rw_ref[...], preferred_element_type=F32, precision=lax.Precision.HIGHEST)
        col = lax.broadcasted_iota(jnp.int32, logits.shape, 1).astype(F32)
        logits = jnp.where(col < n_exp, logits, -jnp.inf)
        m1 = jnp.max(logits, axis=-1, keepdims=True)
        i1 = jnp.min(jnp.where(logits == m1, col, float(LANES)), axis=-1, keepdims=True)
        rest_l = jnp.where(col == i1, -jnp.inf, logits)
        m2 = jnp.max(rest_l, axis=-1, keepdims=True)
        i2 = jnp.min(jnp.where(rest_l == m2, col, float(LANES)), axis=-1, keepdims=True)
        ex = jnp.exp(m2 - m1)
        w1 = 1.0 / (1.0 + ex)
        w2 = ex / (1.0 + ex)
        for x in range(n_exp):
            wx = jnp.where(i1 == x, w1, 0.0) + jnp.where(i2 == x, w2, 0.0)
            comb_ref[x] = jnp.broadcast_to(wx, comb_ref.shape[1:])

    hn = hn_ref[...]
    g = jnp.dot(hn, wg_ref[...], preferred_element_type=F32)
    u = jnp.dot(hn, wu_ref[...], preferred_element_type=F32)
    c = comb_ref[e]
    cw = jnp.concatenate([c] * (g.shape[1] // LANES), axis=1)
    act = (g * jax.nn.sigmoid(g) * u * cw).astype(BF16)
    acc_ref[...] += jnp.dot(act, wd_ref[...], preferred_element_type=F32)

    @pl.when(e == n_exp - 1)
    def _():
        _finish(h_ref[...], acc_ref[...], fw_ref, out_ref)


def _moe(h, norm_w, router_w, wg, wu, wd, final_w):
    n, d = h.shape
    n_exp, _, fp = wg.shape
    tm = _row_tile(n, 512)
    final = final_w is not None
    rw = jnp.pad(router_w.astype(F32), ((0, 0), (0, LANES - n_exp)))
    in_specs = [
        pl.BlockSpec((tm, d), lambda i, e: (i, 0)),
        pl.BlockSpec((1, d), lambda i, e: (0, 0)),
        pl.BlockSpec((d, LANES), lambda i, e: (0, 0)),
        pl.BlockSpec((None, d, fp), lambda i, e: (e, 0, 0)),
        pl.BlockSpec((None, d, fp), lambda i, e: (e, 0, 0)),
        pl.BlockSpec((None, fp, d), lambda i, e: (e, 0, 0)),
    ]
    args = [h, norm_w.reshape(1, d), rw, wg, wu, wd]
    if final:
        in_specs.append(pl.BlockSpec((1, d), lambda i, e: (0, 0)))
        args.append(final_w.reshape(1, d))
    est = 4 * tm * d * 4 + tm * d * 6 + 6 * d * fp * 2 + 4 * tm * fp * 4 + n_exp * tm * LANES * 4 + 2 * d * LANES * 4
    return pl.pallas_call(
        functools.partial(_moe_kernel, final=final, n_exp=n_exp),
        grid=(n // tm, n_exp),
        in_specs=in_specs,
        out_specs=pl.BlockSpec((tm, d), lambda i, e: (i, 0)),
        out_shape=jax.ShapeDtypeStruct((n, d), F32),
        scratch_shapes=[pltpu.VMEM((tm, d), BF16), pltpu.VMEM((tm, d), F32),
                        pltpu.VMEM((n_exp, tm, LANES), F32)],
        compiler_params=_cparams(("parallel", "arbitrary"), est),
        name="moe",
    )(*args)


def _pad_axis(a, axis, to):
    pad = [(0, 0)] * a.ndim
    pad[axis] = (0, to - a.shape[axis])
    return jnp.pad(a, pad)


def _run_group(x, pos0, caches, states, weights):
    (norm_mix_w, w_in, w_sb_o, w_ret_o, gn_ret_w, w_out, norm_ffn_w, ffn_w_gate, ffn_w_up,
     ffn_w_down, router_w, moe_w_gate, moe_w_up, moe_w_down, norm_final_w) = weights
    batch, seq, d = x.shape
    depth = w_in.shape[0]
    w_sb = w_sb_o.shape[1]
    wv_ret = w_ret_o.shape[1]
    d_in = w_in.shape[2]
    wk_ret = (d_in - 3 * w_sb - 2 * wv_ret - 2 * d) // 2
    dims = dict(w_sb=w_sb, wk_ret=wk_ret, wv_ret=wv_ret)
    dk, dv = wk_ret // H_RET, wv_ret // H_RET

    h = x.reshape(batch * seq, d)
    kv = None
    finals = []
    for l in range(depth):
        proj, k_all, v_all = _in_projection(h, norm_mix_w[l], w_in[l], l, depth, kv, w_sb=w_sb)
        kv = (k_all, v_all)
        if caches is None:
            o_sb = _sb_attention_prompt(proj, batch, seq, w_sb=w_sb)
            s0 = jnp.zeros((batch, H_RET, dk, dv), F32)
        else:
            kc = caches[0][l].reshape(batch, -1, w_sb)
            vc = caches[1][l].reshape(batch, -1, w_sb)
            o_sb = _sb_attention_sample(proj, kc, vc, batch, seq, w_sb=w_sb)
            s0 = states[l].astype(F32)
        o_r, s_fin = _retention(proj, gn_ret_w[l], s0, batch, seq, pos0, **dims)
        finals.append(s_fin)
        h = _merge(o_sb, o_r, proj, h, w_sb_o[l], w_ret_o[l], w_out[l], **dims)
        fw = norm_final_w if l == depth - 1 else None
        if l % 2 == 0:
            h = _ffn(h, norm_ffn_w[l], ffn_w_gate[l // 2], ffn_w_up[l // 2], ffn_w_down[l // 2], fw)
        else:
            h = _moe(h, norm_ffn_w[l], router_w[l // 2], moe_w_gate[l // 2], moe_w_up[l // 2],
                     moe_w_down[l // 2], fw)
    hd = w_sb // H_SB
    y = h.reshape(batch, seq, d)
    new_k = kv[0].reshape(depth, batch, seq, H_SB, hd)
    new_v = kv[1].reshape(depth, batch, seq, H_SB, hd)
    return y, new_k, new_v, jnp.stack(finals)


def kernel(x_prompt, x_sample, cache_sb_k, cache_sb_v, state_ret, norm_mix_w, w_in, w_sb_o, w_ret_o,
           gn_ret_w, w_out, norm_ffn_w, ffn_w_gate, ffn_w_up, ffn_w_down, router_w, moe_w_gate,
           moe_w_up, moe_w_down, norm_final_w):
    fe = moe_w_gate.shape[-1]
    fp = -(-fe // LANES) * LANES
    weights = (
        norm_mix_w, w_in.astype(BF16), w_sb_o.astype(BF16), w_ret_o.astype(BF16), gn_ret_w,
        w_out.astype(BF16), norm_ffn_w, ffn_w_gate.astype(BF16), ffn_w_up.astype(BF16),
        ffn_w_down.astype(BF16), router_w,
        _pad_axis(moe_w_gate.astype(BF16), 3, fp), _pad_axis(moe_w_up.astype(BF16), 3, fp),
        _pad_axis(moe_w_down.astype(BF16), 2, fp), norm_final_w,
    )
    past = cache_sb_k.shape[2]
    y_p, k_p, v_p, r_p = _run_group(x_prompt, 0.0, None, None, weights)
    y_s, k_s, v_s, r_s = _run_group(x_sample, float(past), (cache_sb_k, cache_sb_v), state_ret, weights)
    return (y_p, y_s, k_p, v_p, r_p, k_s, v_s, r_s)
```

```python
import functools

import jax
import jax.numpy as jnp
from jax import lax
from jax.experimental import pallas as pl
from jax.experimental.pallas import tpu as pltpu

F32 = jnp.float32
BF16 = jnp.bfloat16

H_SB = 4
H_RET = 4
TOP_K = 2
ROPE_BASE = 10000.0
EPS = 1e-6
GN_EPS = 1e-5
RET_CHUNK_MAX = 256

LANES = 128
V7X_VMEM_BYTES = 64 * 1024 * 1024
VMEM_CAP_BYTES = V7X_VMEM_BYTES - 8 * 1024 * 1024

SB_EXIT_MASS = 104.0
SB_BLOCK = 256


def _cparams(sem, est_bytes):
    limit = int(min(VMEM_CAP_BYTES, max(32 * 1024 * 1024, 2 * est_bytes)))
    return pltpu.CompilerParams(dimension_semantics=sem, vmem_limit_bytes=limit)


def _row_tile(n, want):
    t = min(n, want)
    assert n % t == 0, (n, t)
    return t


def _rms_scale(x, w):
    ms = jnp.mean(x * x, axis=-1, keepdims=True)
    return x * lax.rsqrt(ms + EPS) * w


def _inproj_kernel(x_ref, nw_ref, w_ref, *rest, q_scale, n_alias, tm):
    proj_ref, k_ref, v_ref, hn_ref = rest[n_alias:]
    j = pl.program_id(1)

    @pl.when(j == 0)
    def _():
        hn_ref[...] = _rms_scale(x_ref[...], nw_ref[...]).astype(BF16)

    acc = jnp.dot(hn_ref[...], w_ref[...], preferred_element_type=F32)
    scale = jnp.where(j == 0, q_scale, 1.0).astype(F32)
    proj_ref[...] = (acc * scale).astype(BF16)

    def store_heads(dst_ref):
        for hh in range(H_SB):
            dst_ref[pl.ds(hh, tm, stride=H_SB), :] = acc[:, hh * LANES:(hh + 1) * LANES]

    @pl.when(j == 1)
    def _():
        store_heads(k_ref)

    @pl.when(j == 2)
    def _():
        store_heads(v_ref)


def _in_projection(x, norm_w, w_bf, layer, depth, kv_prev, *, w_sb):
    n, d = x.shape
    d_in = w_bf.shape[1]
    tn = w_sb
    assert d_in % tn == 0
    tm = _row_tile(n, 1024)
    hd = w_sb // H_SB
    assert hd == LANES
    nm = n // tm
    grid = (nm, d_in // tn)
    kv_shape = jax.ShapeDtypeStruct((depth * n * H_SB, hd), F32)
    kv_spec = pl.BlockSpec((tm * H_SB, hd), lambda i, j: (layer * nm + i, 0))
    in_specs = [
        pl.BlockSpec((tm, d), lambda i, j: (i, 0)),
        pl.BlockSpec((1, d), lambda i, j: (0, 0)),
        pl.BlockSpec((d, tn), lambda i, j: (0, j)),
    ]
    args = [x, norm_w.reshape(1, d), w_bf]
    aliases = {}
    n_alias = 0
    if kv_prev is not None:
        in_specs += [pl.BlockSpec(memory_space=pl.ANY), pl.BlockSpec(memory_space=pl.ANY)]
        args += [kv_prev[0], kv_prev[1]]
        aliases = {3: 1, 4: 2}
        n_alias = 2
    est = 2 * tm * d * 4 + tm * d * 2 + 2 * d * tn * 2 + 2 * tm * tn * 2 + 4 * tm * w_sb * 4 + tm * tn * 4
    proj, k_all, v_all = pl.pallas_call(
        functools.partial(_inproj_kernel, q_scale=hd ** -0.5, n_alias=n_alias, tm=tm),
        grid=grid,
        in_specs=in_specs,
        out_specs=[pl.BlockSpec((tm, tn), lambda i, j: (i, j)), kv_spec, kv_spec],
        out_shape=[jax.ShapeDtypeStruct((n, d_in), BF16), kv_shape, kv_shape],
        scratch_shapes=[pltpu.VMEM((tm, d), BF16)],
        input_output_aliases=aliases,
        compiler_params=_cparams(("parallel", "arbitrary"), est),
        name=f"in_projection_l{layer}",
    )(*args)
    return proj, k_all, v_all


def _cumsum_matrix():
    j = jnp.arange(LANES)[:, None]
    s = jnp.arange(LANES)[None, :]
    tri = (j >= s).astype(BF16)
    m = jnp.concatenate([tri, jnp.ones((LANES, LANES), BF16)], axis=1)
    return jnp.concatenate([m, m], axis=0)


def _sb_step(q, chunks, r, cum):
    groups = []
    for k, _, masks, live in chunks:
        z = lax.dot_general(q, k, (((1,), (1,)), ((), ())), preferred_element_type=F32)
        for g in range(k.shape[0] // LANES):
            zg = z[:, g * LANES:(g + 1) * LANES]
            sp = jnp.maximum(zg, 0.0) + jnp.log(1.0 + jnp.exp(-jnp.abs(zg)))
            m = None if masks is None else masks[g]
            if m is not None:
                sp = jnp.where(m, sp, 0.0)
            if live is not None:
                sp = jnp.where(live, sp, 0.0)
            hi = sp.astype(BF16)
            lo = (sp - hi.astype(F32)).astype(BF16)
            ct = jnp.dot(jnp.concatenate([hi, lo], axis=1), cum, preferred_element_type=F32)
            groups.append((zg, ct, m, live))
    weights = [None] * len(groups)
    for idx in reversed(range(len(groups))):
        zg, ct, m, live = groups[idx]
        a = jnp.exp(zg - ct[:, :LANES] - r)
        if m is not None:
            a = jnp.where(m, a, 0.0)
        if live is not None:
            a = jnp.where(live, a, 0.0)
        weights[idx] = a.astype(BF16)
        r = r + ct[:, LANES:]
    pv = None
    idx = 0
    for _, v, _, _ in chunks:
        n_g = v.shape[0] // LANES
        a = weights[idx] if n_g == 1 else jnp.concatenate(weights[idx:idx + n_g], axis=1)
        t = jnp.dot(a, v, preferred_element_type=F32)
        pv = t if pv is None else pv + t
        idx += n_g
    return pv, r


def _sb_prompt_kernel(q_ref, k_ref, v_ref, cum_ref, o_ref, r_ref, acc_ref, *, qb, heads):
    i = pl.program_id(2)
    cum = cum_ref[...]
    row = lax.broadcasted_iota(jnp.int32, (qb, LANES), 0)
    col = lax.broadcasted_iota(jnp.int32, (qb, LANES), 1)
    causal = [(col + g * LANES) < row for g in range(qb // LANES)]
    has_prev = i > 0
    prev0 = pl.multiple_of(jnp.maximum(i - 1, 0) * qb, qb)
    diag0 = pl.multiple_of(i * qb, qb)
    zero = jnp.zeros((qb, LANES), F32)

    for hh in range(heads):
        cs = slice(hh * LANES, (hh + 1) * LANES)
        chunks = [
            (k_ref[pl.ds(prev0, qb), cs], v_ref[pl.ds(prev0, qb), cs], None, has_prev),
            (k_ref[pl.ds(diag0, qb), cs], v_ref[pl.ds(diag0, qb), cs], causal, None),
        ]
        pv, r = _sb_step(q_ref[:, cs], chunks, zero, cum)
        acc_ref[hh] = pv
        r_ref[hh] = r

    for hh in range(heads):
        cs = slice(hh * LANES, (hh + 1) * LANES)

        def cond(j, hh=hh):
            return jnp.logical_and(j >= 0, jnp.min(r_ref[hh]) < SB_EXIT_MASS)

        def body(j, hh=hh, cs=cs):
            start = pl.multiple_of(j * qb, qb)
            chunks = [(k_ref[pl.ds(start, qb), cs], v_ref[pl.ds(start, qb), cs], None, None)]
            pv, r = _sb_step(q_ref[:, cs], chunks, r_ref[hh], cum)
            acc_ref[hh] += pv
            r_ref[hh] = r
            return j - 1

        lax.while_loop(cond, body, i - 2)

    for hh in range(heads):
        o_ref[:, hh * LANES:(hh + 1) * LANES] = acc_ref[hh].astype(o_ref.dtype)


def _sb_attention_prompt(proj, batch, seq, *, w_sb):
    n = proj.shape[0]
    hd = w_sb // H_SB
    assert hd == LANES
    qb = SB_BLOCK
    assert seq % qb == 0
    nq = seq // qb
    heads = 2
    assert H_SB % heads == 0
    wb = heads * hd
    kcol = w_sb // wb
    vcol = 2 * w_sb // wb
    est = 4 * qb * wb * 2 + 4 * seq * wb * 2 + 4 * heads * qb * LANES * 4 + 40 * qb * LANES * 4 * heads
    return pl.pallas_call(
        functools.partial(_sb_prompt_kernel, qb=qb, heads=heads),
        grid=(batch, H_SB // heads, nq),
        in_specs=[
            pl.BlockSpec((qb, wb), lambda b, h, i: (b * nq + i, h)),
            pl.BlockSpec((seq, wb), lambda b, h, i: (b, kcol + h)),
            pl.BlockSpec((seq, wb), lambda b, h, i: (b, vcol + h)),
            pl.BlockSpec((2 * LANES, 2 * LANES), lambda b, h, i: (0, 0)),
        ],
        out_specs=pl.BlockSpec((qb, wb), lambda b, h, i: (b * nq + i, h)),
        out_shape=jax.ShapeDtypeStruct((n, w_sb), BF16),
        scratch_shapes=[pltpu.VMEM((heads, qb, LANES), F32), pltpu.VMEM((heads, qb, LANES), F32)],
        compiler_params=_cparams(("parallel", "parallel", "arbitrary"), est),
        name="sb_attention_prompt",
    )(proj, proj, proj, _cumsum_matrix())


def _sb_sample_kernel(q_ref, kn_ref, vn_ref, kc_ref, vc_ref, cum_ref, o_ref, r_ref, acc_ref, *, tq, past, cb):
    cum = cum_ref[...]
    row = lax.broadcasted_iota(jnp.int32, (tq, LANES), 0)
    col = lax.broadcasted_iota(jnp.int32, (tq, LANES), 1)
    zero = jnp.zeros((tq, LANES), F32)

    def cached(ref, start, hh):
        return ref[pl.ds(start * H_SB + hh, cb, stride=H_SB), :].astype(BF16)

    for hh in range(H_SB):
        cs = slice(hh * LANES, (hh + 1) * LANES)
        chunks = [
            (cached(kc_ref, past - cb, hh), cached(vc_ref, past - cb, hh), None, None),
            (kn_ref[:, cs], vn_ref[:, cs], [col < row], None),
        ]
        pv, r = _sb_step(q_ref[:, cs], chunks, zero, cum)
        acc_ref[hh] = pv
        r_ref[hh] = r

    for hh in range(H_SB):
        cs = slice(hh * LANES, (hh + 1) * LANES)

        def cond(j, hh=hh):
            return jnp.logical_and(j >= 0, jnp.min(r_ref[hh]) < SB_EXIT_MASS)

        def body(j, hh=hh, cs=cs):
            start = j * cb
            chunks = [(cached(kc_ref, start, hh), cached(vc_ref, start, hh), None, None)]
            pv, r = _sb_step(q_ref[:, cs], chunks, r_ref[hh], cum)
            acc_ref[hh] += pv
            r_ref[hh] = r
            return j - 1

        lax.while_loop(cond, body, past // cb - 2)

    for hh in range(H_SB):
        o_ref[:, hh * LANES:(hh + 1) * LANES] = acc_ref[hh].astype(o_ref.dtype)


def _sb_attention_sample(proj, k_cache, v_cache, layer, batch, tq, *, w_sb):
    n = proj.shape[0]
    hd = w_sb // H_SB
    past = k_cache.shape[1] // H_SB
    cb = SB_BLOCK if past % SB_BLOCK == 0 else LANES
    assert hd == LANES and past % cb == 0 and tq <= LANES
    kv_new = proj[:, w_sb:3 * w_sb].reshape(batch, tq, 2 * w_sb)
    kv_new = jnp.pad(kv_new, ((0, 0), (0, LANES - tq), (0, 0)))
    est = 4 * past * w_sb * 4 + 8 * LANES * w_sb * 2 + 40 * H_SB * max(tq, 8) * LANES * 4
    cache_spec = pl.BlockSpec((None, past * H_SB, hd), lambda b: (layer * batch + b, 0, 0))
    return pl.pallas_call(
        functools.partial(_sb_sample_kernel, tq=tq, past=past, cb=cb),
        grid=(batch,),
        in_specs=[
            pl.BlockSpec((tq, w_sb), lambda b: (b, 0)),
            pl.BlockSpec((None, LANES, w_sb), lambda b: (b, 0, 0)),
            pl.BlockSpec((None, LANES, w_sb), lambda b: (b, 0, 1)),
            cache_spec, cache_spec,
            pl.BlockSpec((2 * LANES, 2 * LANES), lambda b: (0, 0)),
        ],
        out_specs=pl.BlockSpec((tq, w_sb), lambda b: (b, 0)),
        out_shape=jax.ShapeDtypeStruct((n, w_sb), BF16),
        scratch_shapes=[pltpu.VMEM((H_SB, tq, LANES), F32), pltpu.VMEM((H_SB, tq, LANES), F32)],
        compiler_params=_cparams(("parallel",), est),
        name="sb_attention_sample",
    )(proj, kv_new, kv_new, k_cache, v_cache, _cumsum_matrix())


def _retention_kernel(dec_ref, q_ref, k_ref, v_ref, g_ref, cos_ref, sin_ref, gnw_ref, s0_ref,
                      o_ref, sout_ref, s_ref, d_ref, qd_ref, kd_ref, *, blk, k_scale):
    h = pl.program_id(1)
    c = pl.program_id(2)
    lg = dec_ref[0, h]
    g_blk = dec_ref[1, h]

    @pl.when(c == 0)
    def _():
        s_ref[...] = s0_ref[...]
        n = lax.broadcasted_iota(jnp.int32, (blk, blk), 0)
        m = lax.broadcasted_iota(jnp.int32, (blk, blk), 1)
        causal = n >= m
        diff = jnp.where(causal, n - m, 0).astype(F32)
        d_ref[...] = jnp.where(causal, jnp.exp(diff * lg), 0.0)
        rows = lax.broadcasted_iota(jnp.int32, (blk, LANES), 0).astype(F32)
        qd_ref[...] = jnp.exp((rows + 1.0) * lg)
        kd_ref[...] = jnp.exp((blk - 1.0 - rows) * lg)

    cos = cos_ref[...]
    sin = sin_ref[...]
    half = cos.shape[1] // 2
    qf = q_ref[...].astype(F32)
    kf = k_ref[...].astype(F32)
    qr = qf * cos + pltpu.roll(qf, half, 1) * sin
    kr = (kf * cos + pltpu.roll(kf, half, 1) * sin) * k_scale
    v = v_ref[...]

    scores = lax.dot_general(qr.astype(BF16), kr.astype(BF16), (((1,), (1,)), ((), ())),
                             preferred_element_type=F32) * d_ref[...]
    inner = jnp.dot(scores.astype(BF16), v, preferred_element_type=F32)
    s = s_ref[...]
    cross = jnp.dot((qr * qd_ref[...]).astype(BF16), s.astype(BF16), preferred_element_type=F32)
    o = inner + cross
    kv = lax.dot_general((kr * kd_ref[...]).astype(BF16), v, (((0,), (0,)), ((), ())),
                         preferred_element_type=F32)
    s_new = g_blk * s + kv
    s_ref[...] = s_new
    sout_ref[...] = s_new

    mu = jnp.mean(o, axis=-1, keepdims=True)
    dev = o - mu
    var = jnp.mean(dev * dev, axis=-1, keepdims=True)
    y = dev * lax.rsqrt(var + GN_EPS) * gnw_ref[...]
    gf = g_ref[...].astype(F32)
    o_ref[...] = (gf * jax.nn.sigmoid(gf) * y).astype(o_ref.dtype)


def _retention(proj, gn_w, state0, batch, seq, pos0, *, w_sb, wk_ret, wv_ret):
    n = proj.shape[0]
    dk = wk_ret // H_RET
    dv = wv_ret // H_RET
    assert dk == LANES and dv % LANES == 0
    blk = min(seq, RET_CHUNK_MAX)
    assert seq % blk == 0
    nc = seq // blk
    q0 = 3 * w_sb // dk
    k0 = (3 * w_sb + wk_ret) // dk
    v0 = (3 * w_sb + 2 * wk_ret) // dv
    g0 = (3 * w_sb + 2 * wk_ret + wv_ret) // dv
    assert (3 * w_sb + 2 * wk_ret) % dv == 0

    half = dk // 2
    inv = ROPE_BASE ** (-jnp.arange(half, dtype=F32) / half)
    ang = (pos0 + jnp.arange(seq, dtype=F32))[:, None] * inv[None, :]
    cos = jnp.concatenate([jnp.cos(ang), jnp.cos(ang)], axis=1)
    sin = jnp.concatenate([-jnp.sin(ang), jnp.sin(ang)], axis=1)
    log_gamma = jnp.log1p(-jnp.exp2(-5.0 - jnp.arange(H_RET, dtype=F32)))
    dec = jnp.stack([log_gamma, jnp.exp(blk * log_gamma)])

    est = 2 * blk * (2 * dk + 2 * dv) * 2 + 4 * blk * dk * 4 + 4 * dk * dv * 4 + 2 * blk * dv * 2 \
        + dk * dv * 4 + blk * blk * 4 + 2 * blk * LANES * 4 + 12 * blk * dv * 4
    row = lambda b, h, c: b * nc + c
    return pl.pallas_call(
        functools.partial(_retention_kernel, blk=blk, k_scale=dk ** -0.5),
        grid=(batch, H_RET, nc),
        in_specs=[
            pl.BlockSpec(memory_space=pltpu.SMEM),
            pl.BlockSpec((blk, dk), lambda b, h, c: (row(b, h, c), q0 + h)),
            pl.BlockSpec((blk, dk), lambda b, h, c: (row(b, h, c), k0 + h)),
            pl.BlockSpec((blk, dv), lambda b, h, c: (row(b, h, c), v0 + h)),
            pl.BlockSpec((blk, dv), lambda b, h, c: (row(b, h, c), g0 + h)),
            pl.BlockSpec((blk, dk), lambda b, h, c: (c, 0)),
            pl.BlockSpec((blk, dk), lambda b, h, c: (c, 0)),
            pl.BlockSpec((1, dv), lambda b, h, c: (0, h)),
            pl.BlockSpec((None, None, dk, dv), lambda b, h, c: (b, h, 0, 0)),
        ],
        out_specs=[
            pl.BlockSpec((blk, dv), lambda b, h, c: (row(b, h, c), h)),
            pl.BlockSpec((None, None, dk, dv), lambda b, h, c: (b, h, 0, 0)),
        ],
        out_shape=[jax.ShapeDtypeStruct((n, wv_ret), BF16),
                   jax.ShapeDtypeStruct((batch, H_RET, dk, dv), F32)],
        scratch_shapes=[pltpu.VMEM((dk, dv), F32), pltpu.VMEM((blk, blk), F32),
                        pltpu.VMEM((blk, LANES), F32), pltpu.VMEM((blk, LANES), F32)],
        compiler_params=_cparams(("parallel", "parallel", "arbitrary"), est),
        name="retention",
    )(dec, proj, proj, proj, proj, cos, sin, gn_w.reshape(1, wv_ret), state0)


def _merge_kernel(osb_ref, or_ref, asb0_ref, asb1_ref, ar0_ref, ar1_ref, h_ref,
                  wsb_ref, wr_ref, wo_ref, out_ref):
    b_sb = jnp.dot(osb_ref[...], wsb_ref[...], preferred_element_type=F32)
    b_r = jnp.dot(or_ref[...], wr_ref[...], preferred_element_type=F32)
    a_sb = jnp.concatenate([asb0_ref[...], asb1_ref[...]], axis=1).astype(F32)
    a_r = jnp.concatenate([ar0_ref[...], ar1_ref[...]], axis=1).astype(F32)
    merged = jax.nn.sigmoid(a_sb) * b_sb + jax.nn.sigmoid(a_r) * b_r
    out_ref[...] = h_ref[...] + jnp.dot(merged.astype(BF16), wo_ref[...], preferred_element_type=F32)


def _merge(o_sb, o_r, proj, h, w_sb_o, w_ret_o, w_out, *, w_sb, wk_ret, wv_ret):
    n, d = h.shape
    tm = _row_tile(n, 512)
    half = d // 2
    a0 = (3 * w_sb + 2 * wk_ret + 2 * wv_ret) // half
    assert (3 * w_sb + 2 * wk_ret + 2 * wv_ret) % half == 0
    gate = lambda t: pl.BlockSpec((tm, half), lambda i: (i, a0 + t))
    full = lambda a: pl.BlockSpec(a.shape, lambda i: (0, 0))
    est = 2 * tm * (w_sb + wv_ret + 2 * d) * 2 + 4 * tm * d * 4 + 2 * (w_sb + wv_ret + d) * d * 2 + 6 * tm * d * 4
    return pl.pallas_call(
        _merge_kernel,
        grid=(n // tm,),
        in_specs=[
            pl.BlockSpec((tm, w_sb), lambda i: (i, 0)),
            pl.BlockSpec((tm, wv_ret), lambda i: (i, 0)),
            gate(0), gate(1), gate(2), gate(3),
            pl.BlockSpec((tm, d), lambda i: (i, 0)),
            full(w_sb_o), full(w_ret_o), full(w_out),
        ],
        out_specs=pl.BlockSpec((tm, d), lambda i: (i, 0)),
        out_shape=jax.ShapeDtypeStruct((n, d), F32),
        compiler_params=_cparams(("parallel",), est),
        name="merge",
    )(o_sb, o_r, proj, proj, proj, proj, h, w_sb_o, w_ret_o, w_out)


def _finish(h, acc, fw_ref, out_ref):
    y = h + acc
    if fw_ref is not None:
        y = _rms_scale(y, fw_ref[...])
    out_ref[...] = y


def _ffn_kernel(h_ref, nw_ref, wg_ref, wu_ref, wd_ref, *rest, final):
    fw_ref = rest[0] if final else None
    out_ref, hn_ref, acc_ref = rest[1:] if final else rest
    f = pl.program_id(1)

    @pl.when(f == 0)
    def _():
        hn_ref[...] = _rms_scale(h_ref[...], nw_ref[...]).astype(BF16)
        acc_ref[...] = jnp.zeros_like(acc_ref)

    hn = hn_ref[...]
    g = jnp.dot(hn, wg_ref[...], preferred_element_type=F32)
    u = jnp.dot(hn, wu_ref[...], preferred_element_type=F32)
    act = (g * jax.nn.sigmoid(g) * u).astype(BF16)
    acc_ref[...] += jnp.dot(act, wd_ref[...], preferred_element_type=F32)

    @pl.when(f == pl.num_programs(1) - 1)
    def _():
        _finish(h_ref[...], acc_ref[...], fw_ref, out_ref)


def _ffn(h, norm_w, wg, wu, wd, final_w):
    n, d = h.shape
    ff = wg.shape[1]
    tm = _row_tile(n, 512)
    tf = ff // 2 if (ff // 2) % LANES == 0 else ff
    final = final_w is not None
    in_specs = [
        pl.BlockSpec((tm, d), lambda i, f: (i, 0)),
        pl.BlockSpec((1, d), lambda i, f: (0, 0)),
        pl.BlockSpec((d, tf), lambda i, f: (0, f)),
        pl.BlockSpec((d, tf), lambda i, f: (0, f)),
        pl.BlockSpec((tf, d), lambda i, f: (f, 0)),
    ]
    args = [h, norm_w.reshape(1, d), wg, wu, wd]
    if final:
        in_specs.append(pl.BlockSpec((1, d), lambda i, f: (0, 0)))
        args.append(final_w.reshape(1, d))
    est = 4 * tm * d * 4 + tm * d * 6 + 6 * d * tf * 2 + 4 * tm * tf * 4
    return pl.pallas_call(
        functools.partial(_ffn_kernel, final=final),
        grid=(n // tm, ff // tf),
        in_specs=in_specs,
        out_specs=pl.BlockSpec((tm, d), lambda i, f: (i, 0)),
        out_shape=jax.ShapeDtypeStruct((n, d), F32),
        scratch_shapes=[pltpu.VMEM((tm, d), BF16), pltpu.VMEM((tm, d), F32)],
        compiler_params=_cparams(("parallel", "arbitrary"), est),
        name="ffn_dense",
    )(*args)


def _moe_kernel(h_ref, nw_ref, rw_ref, wg_ref, wu_ref, wd_ref, *rest, final, n_exp):
    fw_ref = rest[0] if final else None
    out_ref, hn_ref, acc_ref, comb_ref = rest[1:] if final else rest
    e = pl.program_id(1)

    @pl.when(e == 0)
    def _():
        hn = _rms_scale(h_ref[...], nw_ref[...])
        hn_ref[...] = hn.astype(BF16)
        acc_ref[...] = jnp.zeros_like(acc_ref)
        logits = jnp.dot(hn, rw_ref[...], preferred_element_type=F32, precision=lax.Precision.HIGHEST)
        col = lax.broadcasted_iota(jnp.int32, logits.shape, 1).astype(F32)
        logits = jnp.where(col < n_exp, logits, -jnp.inf)
        m1 = jnp.max(logits, axis=-1, keepdims=True)
        i1 = jnp.min(jnp.where(logits == m1, col, float(LANES)), axis=-1, keepdims=True)
        rest_l = jnp.where(col == i1, -jnp.inf, logits)
        m2 = jnp.max(rest_l, axis=-1, keepdims=True)
        i2 = jnp.min(jnp.where(rest_l == m2, col, float(LANES)), axis=-1, keepdims=True)
        ex = jnp.exp(m2 - m1)
        w1 = 1.0 / (1.0 + ex)
        w2 = ex / (1.0 + ex)
        for x in range(n_exp):
            wx = jnp.where(i1 == x, w1, 0.0) + jnp.where(i2 == x, w2, 0.0)
            comb_ref[x] = jnp.broadcast_to(wx, comb_ref.shape[1:])

    hn = hn_ref[...]
    g = jnp.dot(hn, wg_ref[...], preferred_element_type=F32)
    u = jnp.dot(hn, wu_ref[...], preferred_element_type=F32)
    c = comb_ref[e]
    cw = jnp.concatenate([c] * (g.shape[1] // LANES), axis=1)
    act = (g * jax.nn.sigmoid(g) * u * cw).astype(BF16)
    acc_ref[...] += jnp.dot(act, wd_ref[...], preferred_element_type=F32)

    @pl.when(e == n_exp - 1)
    def _():
        _finish(h_ref[...], acc_ref[...], fw_ref, out_ref)


def _moe(h, norm_w, router_w, wg, wu, wd, final_w):
    n, d = h.shape
    n_exp, _, fp = wg.shape
    tm = _row_tile(n, 512)
    final = final_w is not None
    rw = jnp.pad(router_w.astype(F32), ((0, 0), (0, LANES - n_exp)))
    in_specs = [
        pl.BlockSpec((tm, d), lambda i, e: (i, 0)),
        pl.BlockSpec((1, d), lambda i, e: (0, 0)),
        pl.BlockSpec((d, LANES), lambda i, e: (0, 0)),
        pl.BlockSpec((None, d, fp), lambda i, e: (e, 0, 0)),
        pl.BlockSpec((None, d, fp), lambda i, e: (e, 0, 0)),
        pl.BlockSpec((None, fp, d), lambda i, e: (e, 0, 0)),
    ]
    args = [h, norm_w.reshape(1, d), rw, wg, wu, wd]
    if final:
        in_specs.append(pl.BlockSpec((1, d), lambda i, e: (0, 0)))
        args.append(final_w.reshape(1, d))
    est = 4 * tm * d * 4 + tm * d * 6 + 6 * d * fp * 2 + 4 * tm * fp * 4 + n_exp * tm * LANES * 4 + 2 * d * LANES * 4
    return pl.pallas_call(
        functools.partial(_moe_kernel, final=final, n_exp=n_exp),
        grid=(n // tm, n_exp),
        in_specs=in_specs,
        out_specs=pl.BlockSpec((tm, d), lambda i, e: (i, 0)),
        out_shape=jax.ShapeDtypeStruct((n, d), F32),
        scratch_shapes=[pltpu.VMEM((tm, d), BF16), pltpu.VMEM((tm, d), F32),
                        pltpu.VMEM((n_exp, tm, LANES), F32)],
        compiler_params=_cparams(("parallel", "arbitrary"), est),
        name="moe",
    )(*args)


def _pad_axis(a, axis, to):
    pad = [(0, 0)] * a.ndim
    pad[axis] = (0, to - a.shape[axis])
    return jnp.pad(a, pad)


def _run_group(x, pos0, caches, states, weights):
    (norm_mix_w, w_in, w_sb_o, w_ret_o, gn_ret_w, w_out, norm_ffn_w, ffn_w_gate, ffn_w_up,
     ffn_w_down, router_w, moe_w_gate, moe_w_up, moe_w_down, norm_final_w) = weights
    batch, seq, d = x.shape
    depth = w_in.shape[0]
    w_sb = w_sb_o.shape[1]
    wv_ret = w_ret_o.shape[1]
    d_in = w_in.shape[2]
    wk_ret = (d_in - 3 * w_sb - 2 * wv_ret - 2 * d) // 2
    dims = dict(w_sb=w_sb, wk_ret=wk_ret, wv_ret=wv_ret)
    dk, dv = wk_ret // H_RET, wv_ret // H_RET
    hd = w_sb // H_SB

    h = x.reshape(batch * seq, d)
    kv = None
    finals = []
    if caches is not None:
        kc = caches[0].reshape(depth * batch, -1, hd)
        vc = caches[1].reshape(depth * batch, -1, hd)
    for l in range(depth):
        proj, k_all, v_all = _in_projection(h, norm_mix_w[l], w_in[l], l, depth, kv, w_sb=w_sb)
        kv = (k_all, v_all)
        if caches is None:
            o_sb = _sb_attention_prompt(proj, batch, seq, w_sb=w_sb)
            s0 = jnp.zeros((batch, H_RET, dk, dv), F32)
        else:
            o_sb = _sb_attention_sample(proj, kc, vc, l, batch, seq, w_sb=w_sb)
            s0 = states[l].astype(F32)
        o_r, s_fin = _retention(proj, gn_ret_w[l], s0, batch, seq, pos0, **dims)
        finals.append(s_fin)
        h = _merge(o_sb, o_r, proj, h, w_sb_o[l], w_ret_o[l], w_out[l], **dims)
        fw = norm_final_w if l == depth - 1 else None
        if l % 2 == 0:
            h = _ffn(h, norm_ffn_w[l], ffn_w_gate[l // 2], ffn_w_up[l // 2], ffn_w_down[l // 2], fw)
        else:
            h = _moe(h, norm_ffn_w[l], router_w[l // 2], moe_w_gate[l // 2], moe_w_up[l // 2],
                     moe_w_down[l // 2], fw)
    y = h.reshape(batch, seq, d)
    new_k = kv[0].reshape(depth, batch, seq, H_SB, hd)
    new_v = kv[1].reshape(depth, batch, seq, H_SB, hd)
    return y, new_k, new_v, jnp.stack(finals)


def kernel(x_prompt, x_sample, cache_sb_k, cache_sb_v, state_ret, norm_mix_w, w_in, w_sb_o, w_ret_o,
           gn_ret_w, w_out, norm_ffn_w, ffn_w_gate, ffn_w_up, ffn_w_down, router_w, moe_w_gate,
           moe_w_up, moe_w_down, norm_final_w):
    fe = moe_w_gate.shape[-1]
    fp = -(-fe // LANES) * LANES
    weights = (
        norm_mix_w, w_in.astype(BF16), w_sb_o.astype(BF16), w_ret_o.astype(BF16), gn_ret_w,
        w_out.astype(BF16), norm_ffn_w, ffn_w_gate.astype(BF16), ffn_w_up.astype(BF16),
        ffn_w_down.astype(BF16), router_w,
        _pad_axis(moe_w_gate.astype(BF16), 3, fp), _pad_axis(moe_w_up.astype(BF16), 3, fp),
        _pad_axis(moe_w_down.astype(BF16), 2, fp), norm_final_w,
    )
    past = cache_sb_k.shape[2]
    y_p, k_p, v_p, r_p = _run_group(x_prompt, 0.0, None, None, weights)
    y_s, k_s, v_s, r_s = _run_group(x_sample, float(past), (cache_sb_k, cache_sb_v), state_ret, weights)
    return (y_p, y_s, k_p, v_p, r_p, k_s, v_s, r_s)
```

```python
import functools

import jax
import jax.numpy as jnp
from jax import lax
from jax.experimental import pallas as pl
from jax.experimental.pallas import tpu as pltpu

F32 = jnp.float32
BF16 = jnp.bfloat16

H_SB = 4
H_RET = 4
TOP_K = 2
ROPE_BASE = 10000.0
EPS = 1e-6
GN_EPS = 1e-5
RET_CHUNK_MAX = 256

LANES = 128
V7X_VMEM_BYTES = 64 * 1024 * 1024
VMEM_CAP_BYTES = V7X_VMEM_BYTES - 8 * 1024 * 1024

SB_EXIT_MASS = 104.0
SB_BLOCK = 256


def _cparams(sem, est_bytes):
    limit = int(min(VMEM_CAP_BYTES, max(32 * 1024 * 1024, 2 * est_bytes)))
    return pltpu.CompilerParams(dimension_semantics=sem, vmem_limit_bytes=limit)


def _row_tile(n, want):
    t = min(n, want)
    assert n % t == 0, (n, t)
    return t


def _rms_scale(x, w):
    ms = jnp.mean(x * x, axis=-1, keepdims=True)
    return x * lax.rsqrt(ms + EPS) * w


def _inproj_kernel(x_ref, nw_ref, w_ref, *rest, q_scale, n_alias, tm, tn):
    proj_ref, k_ref, v_ref = rest[n_alias:]
    hn = _rms_scale(x_ref[...], nw_ref[...]).astype(BF16)

    def store_heads(dst_ref, acc):
        for hh in range(H_SB):
            dst_ref[pl.ds(hh, tm, stride=H_SB), :] = acc[:, hh * LANES:(hh + 1) * LANES]

    for j in range(w_ref.shape[1] // tn):
        cols = slice(j * tn, (j + 1) * tn)
        acc = jnp.dot(hn, w_ref[:, cols], preferred_element_type=F32)
        if j == 1:
            store_heads(k_ref, acc)
        if j == 2:
            store_heads(v_ref, acc)
        proj_ref[:, cols] = (acc * q_scale if j == 0 else acc).astype(BF16)


def _in_projection(x, norm_w, w_bf, layer, depth, kv_prev, *, w_sb):
    n, d = x.shape
    d_in = w_bf.shape[1]
    tn = w_sb
    assert d_in % tn == 0
    tm = _row_tile(n, 512)
    hd = w_sb // H_SB
    assert hd == LANES
    nm = n // tm
    kv_shape = jax.ShapeDtypeStruct((depth * n * H_SB, hd), F32)
    kv_spec = pl.BlockSpec((tm * H_SB, hd), lambda i: (layer * nm + i, 0))
    in_specs = [
        pl.BlockSpec((tm, d), lambda i: (i, 0)),
        pl.BlockSpec((1, d), lambda i: (0, 0)),
        pl.BlockSpec((d, d_in), lambda i: (0, 0), pipeline_mode=pl.Buffered(1)),
    ]
    args = [x, norm_w.reshape(1, d), w_bf]
    aliases = {}
    n_alias = 0
    if kv_prev is not None:
        in_specs += [pl.BlockSpec(memory_space=pl.ANY), pl.BlockSpec(memory_space=pl.ANY)]
        args += [kv_prev[0], kv_prev[1]]
        aliases = {3: 1, 4: 2}
        n_alias = 2
    est = 2 * tm * d * 4 + tm * d * 2 + d * d_in * 2 + 2 * tm * d_in * 2 + 4 * tm * w_sb * 4 + 4 * tm * tn * 4
    proj, k_all, v_all = pl.pallas_call(
        functools.partial(_inproj_kernel, q_scale=hd ** -0.5, n_alias=n_alias, tm=tm, tn=tn),
        grid=(nm,),
        in_specs=in_specs,
        out_specs=[pl.BlockSpec((tm, d_in), lambda i: (i, 0)), kv_spec, kv_spec],
        out_shape=[jax.ShapeDtypeStruct((n, d_in), BF16), kv_shape, kv_shape],
        input_output_aliases=aliases,
        compiler_params=_cparams(("parallel",), est),
        name=f"in_projection_l{layer}",
    )(*args)
    return proj, k_all, v_all


def _cumsum_matrix():
    j = jnp.arange(LANES)[:, None]
    s = jnp.arange(LANES)[None, :]
    tri = (j >= s).astype(BF16)
    m = jnp.concatenate([tri, jnp.ones((LANES, LANES), BF16)], axis=1)
    return jnp.concatenate([m, m], axis=0)


def _sb_step(q, chunks, r, cum):
    groups = []
    for k, _, masks, live in chunks:
        z = lax.dot_general(q, k, (((1,), (1,)), ((), ())), preferred_element_type=F32)
        for g in range(k.shape[0] // LANES):
            zg = z[:, g * LANES:(g + 1) * LANES]
            sp = jnp.maximum(zg, 0.0) + jnp.log(1.0 + jnp.exp(-jnp.abs(zg)))
            m = None if masks is None else masks[g]
            if m is not None:
                sp = jnp.where(m, sp, 0.0)
            if live is not None:
                sp = jnp.where(live, sp, 0.0)
            hi = sp.astype(BF16)
            lo = (sp - hi.astype(F32)).astype(BF16)
            ct = jnp.dot(jnp.concatenate([hi, lo], axis=1), cum, preferred_element_type=F32)
            groups.append((zg, ct, m, live))
    weights = [None] * len(groups)
    for idx in reversed(range(len(groups))):
        zg, ct, m, live = groups[idx]
        a = jnp.exp(zg - ct[:, :LANES] - r)
        if m is not None:
            a = jnp.where(m, a, 0.0)
        if live is not None:
            a = jnp.where(live, a, 0.0)
        weights[idx] = a.astype(BF16)
        r = r + ct[:, LANES:]
    pv = None
    idx = 0
    for _, v, _, _ in chunks:
        n_g = v.shape[0] // LANES
        a = weights[idx] if n_g == 1 else jnp.concatenate(weights[idx:idx + n_g], axis=1)
        t = jnp.dot(a, v, preferred_element_type=F32)
        pv = t if pv is None else pv + t
        idx += n_g
    return pv, r


def _sb_prompt_kernel(q_ref, k_ref, v_ref, cum_ref, o_ref, r_ref, acc_ref, *, qb, heads):
    i = pl.program_id(2)
    cum = cum_ref[...]
    row = lax.broadcasted_iota(jnp.int32, (qb, LANES), 0)
    col = lax.broadcasted_iota(jnp.int32, (qb, LANES), 1)
    causal = [(col + g * LANES) < row for g in range(qb // LANES)]
    has_prev = i > 0
    prev0 = pl.multiple_of(jnp.maximum(i - 1, 0) * qb, qb)
    diag0 = pl.multiple_of(i * qb, qb)
    zero = jnp.zeros((qb, LANES), F32)

    for hh in range(heads):
        cs = slice(hh * LANES, (hh + 1) * LANES)
        chunks = [
            (k_ref[pl.ds(prev0, qb), cs], v_ref[pl.ds(prev0, qb), cs], None, has_prev),
            (k_ref[pl.ds(diag0, qb), cs], v_ref[pl.ds(diag0, qb), cs], causal, None),
        ]
        pv, r = _sb_step(q_ref[:, cs], chunks, zero, cum)
        acc_ref[hh] = pv
        r_ref[hh] = r

    for hh in range(heads):
        cs = slice(hh * LANES, (hh + 1) * LANES)

        def cond(j, hh=hh):
            return jnp.logical_and(j >= 0, jnp.min(r_ref[hh]) < SB_EXIT_MASS)

        def body(j, hh=hh, cs=cs):
            start = pl.multiple_of(j * qb, qb)
            chunks = [(k_ref[pl.ds(start, qb), cs], v_ref[pl.ds(start, qb), cs], None, None)]
            pv, r = _sb_step(q_ref[:, cs], chunks, r_ref[hh], cum)
            acc_ref[hh] += pv
            r_ref[hh] = r
            return j - 1

        lax.while_loop(cond, body, i - 2)

    for hh in range(heads):
        o_ref[:, hh * LANES:(hh + 1) * LANES] = acc_ref[hh].astype(o_ref.dtype)


def _sb_attention_prompt(proj, batch, seq, *, w_sb):
    n = proj.shape[0]
    hd = w_sb // H_SB
    assert hd == LANES
    qb = SB_BLOCK
    assert seq % qb == 0
    nq = seq // qb
    heads = H_SB
    wb = heads * hd
    kcol = w_sb // wb
    vcol = 2 * w_sb // wb
    est = 4 * qb * wb * 2 + 2 * seq * wb * 2 + 4 * heads * qb * LANES * 4 + 40 * qb * LANES * 4 * heads
    resident = lambda c0: pl.BlockSpec((seq, wb), lambda b, h, i: (b, c0 + h), pipeline_mode=pl.Buffered(1))
    return pl.pallas_call(
        functools.partial(_sb_prompt_kernel, qb=qb, heads=heads),
        grid=(batch, H_SB // heads, nq),
        in_specs=[
            pl.BlockSpec((qb, wb), lambda b, h, i: (b * nq + i, h)),
            resident(kcol),
            resident(vcol),
            pl.BlockSpec((2 * LANES, 2 * LANES), lambda b, h, i: (0, 0)),
        ],
        out_specs=pl.BlockSpec((qb, wb), lambda b, h, i: (b * nq + i, h)),
        out_shape=jax.ShapeDtypeStruct((n, w_sb), BF16),
        scratch_shapes=[pltpu.VMEM((heads, qb, LANES), F32), pltpu.VMEM((heads, qb, LANES), F32)],
        compiler_params=_cparams(("parallel", "parallel", "arbitrary"), est),
        name="sb_attention_prompt",
    )(proj, proj, proj, _cumsum_matrix())


def _sb_sample_kernel(q_ref, kn_ref, vn_ref, kc_ref, vc_ref, cum_ref, o_ref, r_ref, acc_ref, *, tq, past, cb):
    cum = cum_ref[...]
    row = lax.broadcasted_iota(jnp.int32, (tq, LANES), 0)
    col = lax.broadcasted_iota(jnp.int32, (tq, LANES), 1)
    zero = jnp.zeros((tq, LANES), F32)

    def cached(ref, start, hh):
        return ref[pl.ds(start * H_SB + hh, cb, stride=H_SB), :].astype(BF16)

    for hh in range(H_SB):
        cs = slice(hh * LANES, (hh + 1) * LANES)
        chunks = [
            (cached(kc_ref, past - cb, hh), cached(vc_ref, past - cb, hh), None, None),
            (kn_ref[:, cs], vn_ref[:, cs], [col < row], None),
        ]
        pv, r = _sb_step(q_ref[:, cs], chunks, zero, cum)
        acc_ref[hh] = pv
        r_ref[hh] = r

    for hh in range(H_SB):
        cs = slice(hh * LANES, (hh + 1) * LANES)

        def cond(j, hh=hh):
            return jnp.logical_and(j >= 0, jnp.min(r_ref[hh]) < SB_EXIT_MASS)

        def body(j, hh=hh, cs=cs):
            start = j * cb
            chunks = [(cached(kc_ref, start, hh), cached(vc_ref, start, hh), None, None)]
            pv, r = _sb_step(q_ref[:, cs], chunks, r_ref[hh], cum)
            acc_ref[hh] += pv
            r_ref[hh] = r
            return j - 1

        lax.while_loop(cond, body, past // cb - 2)

    for hh in range(H_SB):
        o_ref[:, hh * LANES:(hh + 1) * LANES] = acc_ref[hh].astype(o_ref.dtype)


def _sb_attention_sample(proj, k_cache, v_cache, layer, batch, tq, *, w_sb):
    n = proj.shape[0]
    hd = w_sb // H_SB
    past = k_cache.shape[1] // H_SB
    cb = SB_BLOCK if past % SB_BLOCK == 0 else LANES
    assert hd == LANES and past % cb == 0 and tq <= LANES
    kv_new = proj[:, w_sb:3 * w_sb].reshape(batch, tq, 2 * w_sb)
    kv_new = jnp.pad(kv_new, ((0, 0), (0, LANES - tq), (0, 0)))
    est = 4 * past * w_sb * 4 + 8 * LANES * w_sb * 2 + 40 * H_SB * max(tq, 8) * LANES * 4
    cache_spec = pl.BlockSpec((None, past * H_SB, hd), lambda b: (layer * batch + b, 0, 0))
    return pl.pallas_call(
        functools.partial(_sb_sample_kernel, tq=tq, past=past, cb=cb),
        grid=(batch,),
        in_specs=[
            pl.BlockSpec((tq, w_sb), lambda b: (b, 0)),
            pl.BlockSpec((None, LANES, w_sb), lambda b: (b, 0, 0)),
            pl.BlockSpec((None, LANES, w_sb), lambda b: (b, 0, 1)),
            cache_spec, cache_spec,
            pl.BlockSpec((2 * LANES, 2 * LANES), lambda b: (0, 0)),
        ],
        out_specs=pl.BlockSpec((tq, w_sb), lambda b: (b, 0)),
        out_shape=jax.ShapeDtypeStruct((n, w_sb), BF16),
        scratch_shapes=[pltpu.VMEM((H_SB, tq, LANES), F32), pltpu.VMEM((H_SB, tq, LANES), F32)],
        compiler_params=_cparams(("parallel",), est),
        name="sb_attention_sample",
    )(proj, kv_new, kv_new, k_cache, v_cache, _cumsum_matrix())


def _retention_kernel(dec_ref, q_ref, k_ref, va_ref, vb_ref, ga_ref, gb_ref, cos_ref, sin_ref, gnw_ref,
                      s0_ref, o_ref, sout_ref, s_ref, d_ref, qd_ref, kd_ref, *, blk, k_scale, dk, dv):
    c = pl.program_id(1)
    per_half = H_RET // 2

    @pl.when(c == 0)
    def _():
        s_ref[...] = s0_ref[...]
        n = lax.broadcasted_iota(jnp.int32, (blk, blk), 0)
        m = lax.broadcasted_iota(jnp.int32, (blk, blk), 1)
        causal = n >= m
        diff = jnp.where(causal, n - m, 0).astype(F32)
        rows = lax.broadcasted_iota(jnp.int32, (blk, LANES), 0).astype(F32)
        for hh in range(H_RET):
            lg = dec_ref[0, hh]
            d_ref[hh] = jnp.where(causal, jnp.exp(diff * lg), 0.0)
            qd_ref[hh] = jnp.exp((rows + 1.0) * lg)
            kd_ref[hh] = jnp.exp((blk - 1.0 - rows) * lg)

    cos = cos_ref[...]
    sin = sin_ref[...]
    half = dk // 2
    for hh in range(H_RET):
        ks = slice(hh * dk, (hh + 1) * dk)
        vs = slice(hh * dv, (hh + 1) * dv)
        qf = q_ref[:, ks].astype(F32)
        kf = k_ref[:, ks].astype(F32)
        qr = qf * cos + pltpu.roll(qf, half, 1) * sin
        kr = (kf * cos + pltpu.roll(kf, half, 1) * sin) * k_scale
        hs = slice((hh % per_half) * dv, (hh % per_half + 1) * dv)
        v = (va_ref if hh < per_half else vb_ref)[:, hs]

        scores = lax.dot_general(qr.astype(BF16), kr.astype(BF16), (((1,), (1,)), ((), ())),
                                 preferred_element_type=F32) * d_ref[hh]
        inner = jnp.dot(scores.astype(BF16), v, preferred_element_type=F32)
        s = s_ref[hh]
        cross = jnp.dot((qr * qd_ref[hh]).astype(BF16), s.astype(BF16), preferred_element_type=F32)
        o = inner + cross
        kv = lax.dot_general((kr * kd_ref[hh]).astype(BF16), v, (((0,), (0,)), ((), ())),
                             preferred_element_type=F32)
        s_new = dec_ref[1, hh] * s + kv
        s_ref[hh] = s_new
        sout_ref[hh] = s_new

        mu = jnp.mean(o, axis=-1, keepdims=True)
        dev = o - mu
        var = jnp.mean(dev * dev, axis=-1, keepdims=True)
        y = dev * lax.rsqrt(var + GN_EPS) * gnw_ref[:, vs]
        gf = (ga_ref if hh < per_half else gb_ref)[:, hs].astype(F32)
        o_ref[:, vs] = (gf * jax.nn.sigmoid(gf) * y).astype(o_ref.dtype)


def _retention(proj, gn_w, state0, batch, seq, pos0, *, w_sb, wk_ret, wv_ret):
    n = proj.shape[0]
    dk = wk_ret // H_RET
    dv = wv_ret // H_RET
    assert dk == LANES and dv % LANES == 0
    blk = min(seq, RET_CHUNK_MAX)
    assert seq % blk == 0
    nc = seq // blk
    hw = wv_ret // 2
    assert (3 * w_sb) % wk_ret == 0 and (3 * w_sb + 2 * wk_ret) % hw == 0 and H_RET % 2 == 0
    q0 = 3 * w_sb // wk_ret
    k0 = q0 + 1
    v0 = (3 * w_sb + 2 * wk_ret) // hw
    g0 = v0 + 2

    half = dk // 2
    inv = ROPE_BASE ** (-jnp.arange(half, dtype=F32) / half)
    ang = (pos0 + jnp.arange(seq, dtype=F32))[:, None] * inv[None, :]
    cos = jnp.concatenate([jnp.cos(ang), jnp.cos(ang)], axis=1)
    sin = jnp.concatenate([-jnp.sin(ang), jnp.sin(ang)], axis=1)
    log_gamma = jnp.log1p(-jnp.exp2(-5.0 - jnp.arange(H_RET, dtype=F32)))
    dec = jnp.stack([log_gamma, jnp.exp(blk * log_gamma)])

    est = 2 * blk * (2 * wk_ret + 2 * wv_ret) * 2 + 4 * blk * dk * 4 + 5 * H_RET * dk * dv * 4 \
        + 2 * blk * wv_ret * 2 + H_RET * (blk * blk + 2 * blk * LANES) * 4 + 12 * H_RET * blk * dv * 4
    row = lambda b, c: b * nc + c
    return pl.pallas_call(
        functools.partial(_retention_kernel, blk=blk, k_scale=dk ** -0.5, dk=dk, dv=dv),
        grid=(batch, nc),
        in_specs=[
            pl.BlockSpec(memory_space=pltpu.SMEM),
            pl.BlockSpec((blk, wk_ret), lambda b, c: (row(b, c), q0)),
            pl.BlockSpec((blk, wk_ret), lambda b, c: (row(b, c), k0)),
            pl.BlockSpec((blk, hw), lambda b, c: (row(b, c), v0)),
            pl.BlockSpec((blk, hw), lambda b, c: (row(b, c), v0 + 1)),
            pl.BlockSpec((blk, hw), lambda b, c: (row(b, c), g0)),
            pl.BlockSpec((blk, hw), lambda b, c: (row(b, c), g0 + 1)),
            pl.BlockSpec((blk, dk), lambda b, c: (c, 0)),
            pl.BlockSpec((blk, dk), lambda b, c: (c, 0)),
            pl.BlockSpec((1, wv_ret), lambda b, c: (0, 0)),
            pl.BlockSpec((None, H_RET, dk, dv), lambda b, c: (b, 0, 0, 0)),
        ],
        out_specs=[
            pl.BlockSpec((blk, wv_ret), lambda b, c: (row(b, c), 0)),
            pl.BlockSpec((None, H_RET, dk, dv), lambda b, c: (b, 0, 0, 0)),
        ],
        out_shape=[jax.ShapeDtypeStruct((n, wv_ret), BF16),
                   jax.ShapeDtypeStruct((batch, H_RET, dk, dv), F32)],
        scratch_shapes=[pltpu.VMEM((H_RET, dk, dv), F32), pltpu.VMEM((H_RET, blk, blk), F32),
                        pltpu.VMEM((H_RET, blk, LANES), F32), pltpu.VMEM((H_RET, blk, LANES), F32)],
        compiler_params=_cparams(("parallel", "arbitrary"), est),
        name="retention",
    )(dec, proj, proj, proj, proj, proj, proj, cos, sin, gn_w.reshape(1, wv_ret), state0)


def _merge_kernel(osb_ref, or_ref, asb0_ref, asb1_ref, ar0_ref, ar1_ref, h_ref,
                  wsb_ref, wr_ref, wo_ref, out_ref):
    b_sb = jnp.dot(osb_ref[...], wsb_ref[...], preferred_element_type=F32)
    b_r = jnp.dot(or_ref[...], wr_ref[...], preferred_element_type=F32)
    a_sb = jnp.concatenate([asb0_ref[...], asb1_ref[...]], axis=1).astype(F32)
    a_r = jnp.concatenate([ar0_ref[...], ar1_ref[...]], axis=1).astype(F32)
    merged = jax.nn.sigmoid(a_sb) * b_sb + jax.nn.sigmoid(a_r) * b_r
    out_ref[...] = h_ref[...] + jnp.dot(merged.astype(BF16), wo_ref[...], preferred_element_type=F32)


def _merge(o_sb, o_r, proj, h, w_sb_o, w_ret_o, w_out, *, w_sb, wk_ret, wv_ret):
    n, d = h.shape
    tm = _row_tile(n, 512)
    half = d // 2
    a0 = (3 * w_sb + 2 * wk_ret + 2 * wv_ret) // half
    assert (3 * w_sb + 2 * wk_ret + 2 * wv_ret) % half == 0
    gate = lambda t: pl.BlockSpec((tm, half), lambda i: (i, a0 + t))
    full = lambda a: pl.BlockSpec(a.shape, lambda i: (0, 0))
    est = 2 * tm * (w_sb + wv_ret + 2 * d) * 2 + 4 * tm * d * 4 + 2 * (w_sb + wv_ret + d) * d * 2 + 6 * tm * d * 4
    return pl.pallas_call(
        _merge_kernel,
        grid=(n // tm,),
        in_specs=[
            pl.BlockSpec((tm, w_sb), lambda i: (i, 0)),
            pl.BlockSpec((tm, wv_ret), lambda i: (i, 0)),
            gate(0), gate(1), gate(2), gate(3),
            pl.BlockSpec((tm, d), lambda i: (i, 0)),
            full(w_sb_o), full(w_ret_o), full(w_out),
        ],
        out_specs=pl.BlockSpec((tm, d), lambda i: (i, 0)),
        out_shape=jax.ShapeDtypeStruct((n, d), F32),
        compiler_params=_cparams(("parallel",), est),
        name="merge",
    )(o_sb, o_r, proj, proj, proj, proj, h, w_sb_o, w_ret_o, w_out)


def _finish(h, acc, fw_ref, out_ref):
    y = h + acc
    if fw_ref is not None:
        y = _rms_scale(y, fw_ref[...])
    out_ref[...] = y


def _ffn_kernel(h_ref, nw_ref, wg_ref, wu_ref, wd_ref, *rest, final, tf):
    fw_ref = rest[0] if final else None
    out_ref = rest[-1]
    h = h_ref[...]
    hn = _rms_scale(h, nw_ref[...]).astype(BF16)
    acc = None
    for f in range(wg_ref.shape[1] // tf):
        cols = slice(f * tf, (f + 1) * tf)
        g = jnp.dot(hn, wg_ref[:, cols], preferred_element_type=F32)
        u = jnp.dot(hn, wu_ref[:, cols], preferred_element_type=F32)
        act = (g * jax.nn.sigmoid(g) * u).astype(BF16)
        t = jnp.dot(act, wd_ref[cols, :], preferred_element_type=F32)
        acc = t if acc is None else acc + t
    _finish(h, acc, fw_ref, out_ref)


def _ffn(h, norm_w, wg, wu, wd, final_w):
    n, d = h.shape
    ff = wg.shape[1]
    tm = _row_tile(n, 512)
    tf = 2 * LANES
    assert ff % tf == 0
    final = final_w is not None
    resident = lambda a: pl.BlockSpec(a.shape, lambda i: (0, 0), pipeline_mode=pl.Buffered(1))
    in_specs = [
        pl.BlockSpec((tm, d), lambda i: (i, 0)),
        pl.BlockSpec((1, d), lambda i: (0, 0)),
        resident(wg), resident(wu), resident(wd),
    ]
    args = [h, norm_w.reshape(1, d), wg, wu, wd]
    if final:
        in_specs.append(pl.BlockSpec((1, d), lambda i: (0, 0)))
        args.append(final_w.reshape(1, d))
    est = 4 * tm * d * 4 + tm * d * 2 + 3 * d * ff * 2 + 4 * tm * tf * 4 + 2 * tm * d * 4
    return pl.pallas_call(
        functools.partial(_ffn_kernel, final=final, tf=tf),
        grid=(n // tm,),
        in_specs=in_specs,
        out_specs=pl.BlockSpec((tm, d), lambda i: (i, 0)),
        out_shape=jax.ShapeDtypeStruct((n, d), F32),
        compiler_params=_cparams(("parallel",), est),
        name="ffn_dense",
    )(*args)


def _moe_kernel(h_ref, nw_ref, rw_ref, wg_ref, wu_ref, wd_ref, *rest, final, n_exp):
    fw_ref = rest[0] if final else None
    out_ref, hn_ref, acc_ref, comb_ref = rest[1:] if final else rest
    e = pl.program_id(1)

    @pl.when(e == 0)
    def _():
        hn = _rms_scale(h_ref[...], nw_ref[...])
        hn_ref[...] = hn.astype(BF16)
        acc_ref[...] = jnp.zeros_like(acc_ref)
        logits = jnp.dot(hn, rw_ref[...], preferred_element_type=F32, precision=lax.Precision.HIGHEST)
        col = lax.broadcasted_iota(jnp.int32, logits.shape, 1).astype(F32)
        logits = jnp.where(col < n_exp, logits, -jnp.inf)
        m1 = jnp.max(logits, axis=-1, keepdims=True)
        i1 = jnp.min(jnp.where(logits == m1, col, float(LANES)), axis=-1, keepdims=True)
        rest_l = jnp.where(col == i1, -jnp.inf, logits)
        m2 = jnp.max(rest_l, axis=-1, keepdims=True)
        i2 = jnp.min(jnp.where(rest_l == m2, col, float(LANES)), axis=-1, keepdims=True)
        ex = jnp.exp(m2 - m1)
        w1 = 1.0 / (1.0 + ex)
        w2 = ex / (1.0 + ex)
        for x in range(n_exp):
            wx = jnp.where(i1 == x, w1, 0.0) + jnp.where(i2 == x, w2, 0.0)
            comb_ref[x] = jnp.broadcast_to(wx, comb_ref.shape[1:])

    hn = hn_ref[...]
    g = jnp.dot(hn, wg_ref[...], preferred_element_type=F32)
    u = jnp.dot(hn, wu_ref[...], preferred_element_type=F32)
    c = comb_ref[e]
    cw = jnp.concatenate([c] * (g.shape[1] // LANES), axis=1)
    act = (g * jax.nn.sigmoid(g) * u * cw).astype(BF16)
    acc_ref[...] += jnp.dot(act, wd_ref[...], preferred_element_type=F32)

    @pl.when(e == n_exp - 1)
    def _():
        _finish(h_ref[...], acc_ref[...], fw_ref, out_ref)


def _moe(h, norm_w, router_w, wg, wu, wd, final_w):
    n, d = h.shape
    n_exp, _, fp = wg.shape
    tm = _row_tile(n, 1024)
    final = final_w is not None
    rw = jnp.pad(router_w.astype(F32), ((0, 0), (0, LANES - n_exp)))
    in_specs = [
        pl.BlockSpec((tm, d), lambda i, e: (i, 0)),
        pl.BlockSpec((1, d), lambda i, e: (0, 0)),
        pl.BlockSpec((d, LANES), lambda i, e: (0, 0)),
        pl.BlockSpec((None, d, fp), lambda i, e: (e, 0, 0)),
        pl.BlockSpec((None, d, fp), lambda i, e: (e, 0, 0)),
        pl.BlockSpec((None, fp, d), lambda i, e: (e, 0, 0)),
    ]
    args = [h, norm_w.reshape(1, d), rw, wg, wu, wd]
    if final:
        in_specs.append(pl.BlockSpec((1, d), lambda i, e: (0, 0)))
        args.append(final_w.reshape(1, d))
    est = 4 * tm * d * 4 + tm * d * 6 + 6 * d * fp * 2 + 4 * tm * fp * 4 + n_exp * tm * LANES * 4 + 2 * d * LANES * 4
    return pl.pallas_call(
        functools.partial(_moe_kernel, final=final, n_exp=n_exp),
        grid=(n // tm, n_exp),
        in_specs=in_specs,
        out_specs=pl.BlockSpec((tm, d), lambda i, e: (i, 0)),
        out_shape=jax.ShapeDtypeStruct((n, d), F32),
        scratch_shapes=[pltpu.VMEM((tm, d), BF16), pltpu.VMEM((tm, d), F32),
                        pltpu.VMEM((n_exp, tm, LANES), F32)],
        compiler_params=_cparams(("parallel", "arbitrary"), est),
        name="moe",
    )(*args)


def _pad_axis(a, axis, to):
    pad = [(0, 0)] * a.ndim
    pad[axis] = (0, to - a.shape[axis])
    return jnp.pad(a, pad)


def _run_group(x, pos0, caches, states, weights):
    (norm_mix_w, w_in, w_sb_o, w_ret_o, gn_ret_w, w_out, norm_ffn_w, ffn_w_gate, ffn_w_up,
     ffn_w_down, router_w, moe_w_gate, moe_w_up, moe_w_down, norm_final_w) = weights
    batch, seq, d = x.shape
    depth = w_in.shape[0]
    w_sb = w_sb_o.shape[1]
    wv_ret = w_ret_o.shape[1]
    d_in = w_in.shape[2]
    wk_ret = (d_in - 3 * w_sb - 2 * wv_ret - 2 * d) // 2
    dims = dict(w_sb=w_sb, wk_ret=wk_ret, wv_ret=wv_ret)
    dk, dv = wk_ret // H_RET, wv_ret // H_RET
    hd = w_sb // H_SB

    h = x.reshape(batch * seq, d)
    kv = None
    finals = []
    if caches is not None:
        kc = caches[0].reshape(depth * batch, -1, hd)
        vc = caches[1].reshape(depth * batch, -1, hd)
    for l in range(depth):
        proj, k_all, v_all = _in_projection(h, norm_mix_w[l], w_in[l], l, depth, kv, w_sb=w_sb)
        kv = (k_all, v_all)
        if caches is None:
            o_sb = _sb_attention_prompt(proj, batch, seq, w_sb=w_sb)
            s0 = jnp.zeros((batch, H_RET, dk, dv), F32)
        else:
            o_sb = _sb_attention_sample(proj, kc, vc, l, batch, seq, w_sb=w_sb)
            s0 = states[l].astype(F32)
        o_r, s_fin = _retention(proj, gn_ret_w[l], s0, batch, seq, pos0, **dims)
        finals.append(s_fin)
        h = _merge(o_sb, o_r, proj, h, w_sb_o[l], w_ret_o[l], w_out[l], **dims)
        fw = norm_final_w if l == depth - 1 else None
        if l % 2 == 0:
            h = _ffn(h, norm_ffn_w[l], ffn_w_gate[l // 2], ffn_w_up[l // 2], ffn_w_down[l // 2], fw)
        else:
            h = _moe(h, norm_ffn_w[l], router_w[l // 2], moe_w_gate[l // 2], moe_w_up[l // 2],
                     moe_w_down[l // 2], fw)
    y = h.reshape(batch, seq, d)
    new_k = kv[0].reshape(depth, batch, seq, H_SB, hd)
    new_v = kv[1].reshape(depth, batch, seq, H_SB, hd)
    return y, new_k, new_v, jnp.stack(finals)


def kernel(x_prompt, x_sample, cache_sb_k, cache_sb_v, state_ret, norm_mix_w, w_in, w_sb_o, w_ret_o,
           gn_ret_w, w_out, norm_ffn_w, ffn_w_gate, ffn_w_up, ffn_w_down, router_w, moe_w_gate,
           moe_w_up, moe_w_down, norm_final_w):
    fe = moe_w_gate.shape[-1]
    fp = -(-fe // LANES) * LANES
    weights = (
        norm_mix_w, w_in.astype(BF16), w_sb_o.astype(BF16), w_ret_o.astype(BF16), gn_ret_w,
        w_out.astype(BF16), norm_ffn_w, ffn_w_gate.astype(BF16), ffn_w_up.astype(BF16),
        ffn_w_down.astype(BF16), router_w,
        _pad_axis(moe_w_gate.astype(BF16), 3, fp), _pad_axis(moe_w_up.astype(BF16), 3, fp),
        _pad_axis(moe_w_down.astype(BF16), 2, fp), norm_final_w,
    )
    past = cache_sb_k.shape[2]
    y_p, k_p, v_p, r_p = _run_group(x_prompt, 0.0, None, None, weights)
    y_s, k_s, v_s, r_s = _run_group(x_sample, float(past), (cache_sb_k, cache_sb_v), state_ret, weights)
    return (y_p, y_s, k_p, v_p, r_p, k_s, v_s, r_s)
```

```python
import functools

import jax
import jax.numpy as jnp
from jax import lax
from jax.experimental import pallas as pl
from jax.experimental.pallas import tpu as pltpu

F32 = jnp.float32
BF16 = jnp.bfloat16

H_SB = 4
H_RET = 4
TOP_K = 2
ROPE_BASE = 10000.0
EPS = 1e-6
GN_EPS = 1e-5
RET_CHUNK_MAX = 256

LANES = 128
V7X_VMEM_BYTES = 64 * 1024 * 1024
VMEM_CAP_BYTES = V7X_VMEM_BYTES - 8 * 1024 * 1024

SB_EXIT_MASS = 104.0
SB_BLOCK = 256


def _cparams(sem, est_bytes):
    limit = int(min(VMEM_CAP_BYTES, max(32 * 1024 * 1024, 2 * est_bytes)))
    return pltpu.CompilerParams(dimension_semantics=sem, vmem_limit_bytes=limit)


def _row_tile(n, want):
    t = min(n, want)
    assert n % t == 0, (n, t)
    return t


def _rms_scale(x, w):
    ms = jnp.mean(x * x, axis=-1, keepdims=True)
    return x * lax.rsqrt(ms + EPS) * w


def _inproj_kernel(x_ref, nw_ref, w_ref, *rest, q_scale, n_alias, tm, tn):
    proj_ref, k_ref, v_ref = rest[n_alias:]
    hn = _rms_scale(x_ref[...], nw_ref[...]).astype(BF16)

    def store_heads(dst_ref, acc):
        for hh in range(H_SB):
            dst_ref[pl.ds(hh, tm, stride=H_SB), :] = acc[:, hh * LANES:(hh + 1) * LANES]

    for j in range(w_ref.shape[1] // tn):
        cols = slice(j * tn, (j + 1) * tn)
        acc = jnp.dot(hn, w_ref[:, cols], preferred_element_type=F32)
        if j == 1:
            store_heads(k_ref, acc)
        if j == 2:
            store_heads(v_ref, acc)
        proj_ref[:, cols] = (acc * q_scale if j == 0 else acc).astype(BF16)


def _in_projection(x, norm_w, w_bf, layer, depth, kv_prev, *, w_sb):
    n, d = x.shape
    d_in = w_bf.shape[1]
    tn = w_sb
    assert d_in % tn == 0
    tm = _row_tile(n, 512)
    hd = w_sb // H_SB
    assert hd == LANES
    nm = n // tm
    kv_shape = jax.ShapeDtypeStruct((depth * n * H_SB, hd), F32)
    kv_spec = pl.BlockSpec((tm * H_SB, hd), lambda i: (layer * nm + i, 0))
    in_specs = [
        pl.BlockSpec((tm, d), lambda i: (i, 0)),
        pl.BlockSpec((1, d), lambda i: (0, 0)),
        pl.BlockSpec((d, d_in), lambda i: (0, 0), pipeline_mode=pl.Buffered(1)),
    ]
    args = [x, norm_w.reshape(1, d), w_bf]
    aliases = {}
    n_alias = 0
    if kv_prev is not None:
        in_specs += [pl.BlockSpec(memory_space=pl.ANY), pl.BlockSpec(memory_space=pl.ANY)]
        args += [kv_prev[0], kv_prev[1]]
        aliases = {3: 1, 4: 2}
        n_alias = 2
    est = 2 * tm * d * 4 + tm * d * 2 + d * d_in * 2 + 2 * tm * d_in * 2 + 4 * tm * w_sb * 4 + 4 * tm * tn * 4
    proj, k_all, v_all = pl.pallas_call(
        functools.partial(_inproj_kernel, q_scale=hd ** -0.5, n_alias=n_alias, tm=tm, tn=tn),
        grid=(nm,),
        in_specs=in_specs,
        out_specs=[pl.BlockSpec((tm, d_in), lambda i: (i, 0)), kv_spec, kv_spec],
        out_shape=[jax.ShapeDtypeStruct((n, d_in), BF16), kv_shape, kv_shape],
        input_output_aliases=aliases,
        compiler_params=_cparams(("parallel",), est),
        name=f"in_projection_l{layer}",
    )(*args)
    return proj, k_all, v_all


def _cumsum_matrix():
    j = jnp.arange(LANES)[:, None]
    s = jnp.arange(LANES)[None, :]
    tri = (j >= s).astype(BF16)
    m = jnp.concatenate([tri, jnp.ones((LANES, LANES), BF16)], axis=1)
    return jnp.concatenate([m, m], axis=0)


def _sb_step_phases(q, chunks, r, cum):
    zs = [lax.dot_general(q, k, (((1,), (1,)), ((), ())), preferred_element_type=F32)
          for k, _, _, _ in chunks]
    yield
    groups = []
    for (k, _, masks, live), z in zip(chunks, zs):
        for g in range(k.shape[0] // LANES):
            zg = z[:, g * LANES:(g + 1) * LANES]
            sp = jnp.maximum(zg, 0.0) + jnp.log(1.0 + jnp.exp(-jnp.abs(zg)))
            m = None if masks is None else masks[g]
            if m is not None:
                sp = jnp.where(m, sp, 0.0)
            if live is not None:
                sp = jnp.where(live, sp, 0.0)
            hi = sp.astype(BF16)
            lo = (sp - hi.astype(F32)).astype(BF16)
            groups.append([zg, jnp.concatenate([hi, lo], axis=1), m, live])
    yield
    for grp in groups:
        grp[1] = jnp.dot(grp[1], cum, preferred_element_type=F32)
    yield
    weights = [None] * len(groups)
    for idx in reversed(range(len(groups))):
        zg, ct, m, live = groups[idx]
        a = jnp.exp(zg - ct[:, :LANES] - r)
        if m is not None:
            a = jnp.where(m, a, 0.0)
        if live is not None:
            a = jnp.where(live, a, 0.0)
        weights[idx] = a.astype(BF16)
        r = r + ct[:, LANES:]
    yield
    pv = None
    idx = 0
    for _, v, _, _ in chunks:
        n_g = v.shape[0] // LANES
        a = weights[idx] if n_g == 1 else jnp.concatenate(weights[idx:idx + n_g], axis=1)
        t = jnp.dot(a, v, preferred_element_type=F32)
        pv = t if pv is None else pv + t
        idx += n_g
    return pv, r


def _lockstep(gens):
    results = [None] * len(gens)
    running = list(range(len(gens)))
    while running:
        still = []
        for t in running:
            try:
                next(gens[t])
                still.append(t)
            except StopIteration as stop:
                results[t] = stop.value
        running = still
    return results


def _sb_step(q, chunks, r, cum):
    return _lockstep([_sb_step_phases(q, chunks, r, cum)])[0]


def _sb_prompt_kernel(q_ref, k_ref, v_ref, cum_ref, o_ref, r_ref, acc_ref, *, qb, heads):
    i = pl.program_id(2)
    cum = cum_ref[...]
    row = lax.broadcasted_iota(jnp.int32, (qb, LANES), 0)
    col = lax.broadcasted_iota(jnp.int32, (qb, LANES), 1)
    causal = [(col + g * LANES) < row for g in range(qb // LANES)]
    has_prev = i > 0
    prev0 = pl.multiple_of(jnp.maximum(i - 1, 0) * qb, qb)
    diag0 = pl.multiple_of(i * qb, qb)
    zero = jnp.zeros((qb, LANES), F32)

    steps = []
    for hh in range(heads):
        cs = slice(hh * LANES, (hh + 1) * LANES)
        chunks = [
            (k_ref[pl.ds(prev0, qb), cs], v_ref[pl.ds(prev0, qb), cs], None, has_prev),
            (k_ref[pl.ds(diag0, qb), cs], v_ref[pl.ds(diag0, qb), cs], causal, None),
        ]
        steps.append(_sb_step_phases(q_ref[:, cs], chunks, zero, cum))
    for hh, (pv, r) in enumerate(_lockstep(steps)):
        acc_ref[hh] = pv
        r_ref[hh] = r

    for hh in range(heads):
        cs = slice(hh * LANES, (hh + 1) * LANES)

        def cond(j, hh=hh):
            return jnp.logical_and(j >= 0, jnp.min(r_ref[hh]) < SB_EXIT_MASS)

        def body(j, hh=hh, cs=cs):
            start = pl.multiple_of(j * qb, qb)
            chunks = [(k_ref[pl.ds(start, qb), cs], v_ref[pl.ds(start, qb), cs], None, None)]
            pv, r = _sb_step(q_ref[:, cs], chunks, r_ref[hh], cum)
            acc_ref[hh] += pv
            r_ref[hh] = r
            return j - 1

        lax.while_loop(cond, body, i - 2)

    for hh in range(heads):
        o_ref[:, hh * LANES:(hh + 1) * LANES] = acc_ref[hh].astype(o_ref.dtype)


def _sb_attention_prompt(proj, batch, seq, *, w_sb):
    n = proj.shape[0]
    hd = w_sb // H_SB
    assert hd == LANES
    qb = SB_BLOCK
    assert seq % qb == 0
    nq = seq // qb
    heads = H_SB
    wb = heads * hd
    kcol = w_sb // wb
    vcol = 2 * w_sb // wb
    est = 4 * qb * wb * 2 + 2 * seq * wb * 2 + 4 * heads * qb * LANES * 4 + 40 * qb * LANES * 4 * heads
    resident = lambda c0: pl.BlockSpec((seq, wb), lambda b, h, i: (b, c0 + h), pipeline_mode=pl.Buffered(1))
    return pl.pallas_call(
        functools.partial(_sb_prompt_kernel, qb=qb, heads=heads),
        grid=(batch, H_SB // heads, nq),
        in_specs=[
            pl.BlockSpec((qb, wb), lambda b, h, i: (b * nq + i, h)),
            resident(kcol),
            resident(vcol),
            pl.BlockSpec((2 * LANES, 2 * LANES), lambda b, h, i: (0, 0)),
        ],
        out_specs=pl.BlockSpec((qb, wb), lambda b, h, i: (b * nq + i, h)),
        out_shape=jax.ShapeDtypeStruct((n, w_sb), BF16),
        scratch_shapes=[pltpu.VMEM((heads, qb, LANES), F32), pltpu.VMEM((heads, qb, LANES), F32)],
        compiler_params=_cparams(("parallel", "parallel", "arbitrary"), est),
        name="sb_attention_prompt",
    )(proj, proj, proj, _cumsum_matrix())


def _sb_sample_kernel(q_ref, kn_ref, vn_ref, kc_hbm, vc_hbm, cum_ref, o_ref, r_ref, acc_ref, buf_ref, sem_ref,
                      *, tq, past, cb, stream0, n_streams):
    b = pl.program_id(0)
    cum = cum_ref[...]
    row = lax.broadcasted_iota(jnp.int32, (tq, LANES), 0)
    col = lax.broadcasted_iota(jnp.int32, (tq, LANES), 1)
    zero = jnp.zeros((tq, LANES), F32)
    rows = cb * H_SB
    latest = past - cb

    def chunk_copy(stream, start, slot, which):
        src = (kc_hbm, vc_hbm)[which].at[stream0 + stream, pl.ds(start * H_SB, rows), :]
        return pltpu.make_async_copy(src, buf_ref.at[slot, which], sem_ref.at[slot, which])

    def cached(slot, which, hh):
        return buf_ref[slot, which, pl.ds(hh, cb, stride=H_SB), :].astype(BF16)

    slot = lax.rem(b, 2)

    @pl.when(b == 0)
    def _():
        for which in range(2):
            chunk_copy(0, latest, 0, which).start()

    for which in range(2):
        chunk_copy(b, latest, slot, which).wait()

    @pl.when(b + 1 < n_streams)
    def _():
        for which in range(2):
            chunk_copy(b + 1, latest, 1 - slot, which).start()

    steps = []
    for hh in range(H_SB):
        cs = slice(hh * LANES, (hh + 1) * LANES)
        chunks = [
            (cached(slot, 0, hh), cached(slot, 1, hh), None, None),
            (kn_ref[:, cs], vn_ref[:, cs], [col < row], None),
        ]
        steps.append(_sb_step_phases(q_ref[:, cs], chunks, zero, cum))
    for hh, (pv, r) in enumerate(_lockstep(steps)):
        acc_ref[hh] = pv
        r_ref[hh] = r

    def cond(j):
        return jnp.logical_and(j >= 0, jnp.min(r_ref[...]) < SB_EXIT_MASS)

    def body(j):
        for which in range(2):
            chunk_copy(b, j * cb, 2, which).start()
        for which in range(2):
            chunk_copy(b, j * cb, 2, which).wait()
        steps = []
        for hh in range(H_SB):
            cs = slice(hh * LANES, (hh + 1) * LANES)
            chunks = [(cached(2, 0, hh), cached(2, 1, hh), None, None)]
            steps.append(_sb_step_phases(q_ref[:, cs], chunks, r_ref[hh], cum))
        for hh, (pv, r) in enumerate(_lockstep(steps)):
            acc_ref[hh] += pv
            r_ref[hh] = r
        return j - 1

    lax.while_loop(cond, body, past // cb - 2)

    for hh in range(H_SB):
        o_ref[:, hh * LANES:(hh + 1) * LANES] = acc_ref[hh].astype(o_ref.dtype)


def _sb_attention_sample(proj, k_cache, v_cache, layer, batch, tq, *, w_sb):
    n = proj.shape[0]
    hd = w_sb // H_SB
    past = k_cache.shape[1] // H_SB
    cb = SB_BLOCK if past % SB_BLOCK == 0 else LANES
    assert hd == LANES and past % cb == 0 and tq <= LANES
    kv_new = proj[:, w_sb:3 * w_sb].reshape(batch, tq, 2 * w_sb)
    kv_new = jnp.pad(kv_new, ((0, 0), (0, LANES - tq), (0, 0)))
    est = 6 * cb * w_sb * 4 + 8 * LANES * w_sb * 2 + 40 * H_SB * max(tq, 8) * LANES * 4
    return pl.pallas_call(
        functools.partial(_sb_sample_kernel, tq=tq, past=past, cb=cb, stream0=layer * batch, n_streams=batch),
        grid=(batch,),
        in_specs=[
            pl.BlockSpec((tq, w_sb), lambda b: (b, 0)),
            pl.BlockSpec((None, LANES, w_sb), lambda b: (b, 0, 0)),
            pl.BlockSpec((None, LANES, w_sb), lambda b: (b, 0, 1)),
            pl.BlockSpec(memory_space=pl.ANY),
            pl.BlockSpec(memory_space=pl.ANY),
            pl.BlockSpec((2 * LANES, 2 * LANES), lambda b: (0, 0)),
        ],
        out_specs=pl.BlockSpec((tq, w_sb), lambda b: (b, 0)),
        out_shape=jax.ShapeDtypeStruct((n, w_sb), BF16),
        scratch_shapes=[pltpu.VMEM((H_SB, tq, LANES), F32), pltpu.VMEM((H_SB, tq, LANES), F32),
                        pltpu.VMEM((3, 2, cb * H_SB, hd), F32), pltpu.SemaphoreType.DMA((3, 2))],
        compiler_params=_cparams(("arbitrary",), est),
        name="sb_attention_sample",
    )(proj, kv_new, kv_new, k_cache, v_cache, _cumsum_matrix())


def _retention_kernel(dec_ref, q_ref, k_ref, va_ref, vb_ref, ga_ref, gb_ref, cos_ref, sin_ref, gnw_ref,
                      s0_ref, o_ref, sout_ref, s_ref, d_ref, qd_ref, kd_ref, *, blk, k_scale, dk, dv):
    c = pl.program_id(1)
    per_half = H_RET // 2

    @pl.when(c == 0)
    def _():
        s_ref[...] = s0_ref[...]
        n = lax.broadcasted_iota(jnp.int32, (blk, blk), 0)
        m = lax.broadcasted_iota(jnp.int32, (blk, blk), 1)
        causal = n >= m
        diff = jnp.where(causal, n - m, 0).astype(F32)
        rows = lax.broadcasted_iota(jnp.int32, (blk, LANES), 0).astype(F32)
        for hh in range(H_RET):
            lg = dec_ref[0, hh]
            d_ref[hh] = jnp.where(causal, jnp.exp(diff * lg), 0.0)
            qd_ref[hh] = jnp.exp((rows + 1.0) * lg)
            kd_ref[hh] = jnp.exp((blk - 1.0 - rows) * lg)

    cos = cos_ref[...]
    sin = sin_ref[...]
    half = dk // 2
    for hh in range(H_RET):
        ks = slice(hh * dk, (hh + 1) * dk)
        vs = slice(hh * dv, (hh + 1) * dv)
        qf = q_ref[:, ks].astype(F32)
        kf = k_ref[:, ks].astype(F32)
        qr = qf * cos + pltpu.roll(qf, half, 1) * sin
        kr = (kf * cos + pltpu.roll(kf, half, 1) * sin) * k_scale
        hs = slice((hh % per_half) * dv, (hh % per_half + 1) * dv)
        v = (va_ref if hh < per_half else vb_ref)[:, hs]

        scores = lax.dot_general(qr.astype(BF16), kr.astype(BF16), (((1,), (1,)), ((), ())),
                                 preferred_element_type=F32) * d_ref[hh]
        inner = jnp.dot(scores.astype(BF16), v, preferred_element_type=F32)
        s = s_ref[hh]
        cross = jnp.dot((qr * qd_ref[hh]).astype(BF16), s.astype(BF16), preferred_element_type=F32)
        o = inner + cross
        kv = lax.dot_general((kr * kd_ref[hh]).astype(BF16), v, (((0,), (0,)), ((), ())),
                             preferred_element_type=F32)
        s_new = dec_ref[1, hh] * s + kv
        s_ref[hh] = s_new
        sout_ref[hh] = s_new

        mu = jnp.mean(o, axis=-1, keepdims=True)
        dev = o - mu
        var = jnp.mean(dev * dev, axis=-1, keepdims=True)
        y = dev * lax.rsqrt(var + GN_EPS) * gnw_ref[:, vs]
        gf = (ga_ref if hh < per_half else gb_ref)[:, hs].astype(F32)
        o_ref[:, vs] = (gf * jax.nn.sigmoid(gf) * y).astype(o_ref.dtype)


def _retention(proj, gn_w, state0, batch, seq, pos0, *, w_sb, wk_ret, wv_ret):
    n = proj.shape[0]
    dk = wk_ret // H_RET
    dv = wv_ret // H_RET
    assert dk == LANES and dv % LANES == 0
    blk = min(seq, RET_CHUNK_MAX)
    assert seq % blk == 0
    nc = seq // blk
    hw = wv_ret // 2
    assert (3 * w_sb) % wk_ret == 0 and (3 * w_sb + 2 * wk_ret) % hw == 0 and H_RET % 2 == 0
    q0 = 3 * w_sb // wk_ret
    k0 = q0 + 1
    v0 = (3 * w_sb + 2 * wk_ret) // hw
    g0 = v0 + 2

    half = dk // 2
    inv = ROPE_BASE ** (-jnp.arange(half, dtype=F32) / half)
    ang = (pos0 + jnp.arange(seq, dtype=F32))[:, None] * inv[None, :]
    cos = jnp.concatenate([jnp.cos(ang), jnp.cos(ang)], axis=1)
    sin = jnp.concatenate([-jnp.sin(ang), jnp.sin(ang)], axis=1)
    log_gamma = jnp.log1p(-jnp.exp2(-5.0 - jnp.arange(H_RET, dtype=F32)))
    dec = jnp.stack([log_gamma, jnp.exp(blk * log_gamma)])

    est = 2 * blk * (2 * wk_ret + 2 * wv_ret) * 2 + 4 * blk * dk * 4 + 5 * H_RET * dk * dv * 4 \
        + 2 * blk * wv_ret * 2 + H_RET * (blk * blk + 2 * blk * LANES) * 4 + 12 * H_RET * blk * dv * 4
    row = lambda b, c: b * nc + c
    return pl.pallas_call(
        functools.partial(_retention_kernel, blk=blk, k_scale=dk ** -0.5, dk=dk, dv=dv),
        grid=(batch, nc),
        in_specs=[
            pl.BlockSpec(memory_space=pltpu.SMEM),
            pl.BlockSpec((blk, wk_ret), lambda b, c: (row(b, c), q0)),
            pl.BlockSpec((blk, wk_ret), lambda b, c: (row(b, c), k0)),
            pl.BlockSpec((blk, hw), lambda b, c: (row(b, c), v0)),
            pl.BlockSpec((blk, hw), lambda b, c: (row(b, c), v0 + 1)),
            pl.BlockSpec((blk, hw), lambda b, c: (row(b, c), g0)),
            pl.BlockSpec((blk, hw), lambda b, c: (row(b, c), g0 + 1)),
            pl.BlockSpec((blk, dk), lambda b, c: (c, 0)),
            pl.BlockSpec((blk, dk), lambda b, c: (c, 0)),
            pl.BlockSpec((1, wv_ret), lambda b, c: (0, 0)),
            pl.BlockSpec((None, H_RET, dk, dv), lambda b, c: (b, 0, 0, 0)),
        ],
        out_specs=[
            pl.BlockSpec((blk, wv_ret), lambda b, c: (row(b, c), 0)),
            pl.BlockSpec((None, H_RET, dk, dv), lambda b, c: (b, 0, 0, 0)),
        ],
        out_shape=[jax.ShapeDtypeStruct((n, wv_ret), BF16),
                   jax.ShapeDtypeStruct((batch, H_RET, dk, dv), F32)],
        scratch_shapes=[pltpu.VMEM((H_RET, dk, dv), F32), pltpu.VMEM((H_RET, blk, blk), F32),
                        pltpu.VMEM((H_RET, blk, LANES), F32), pltpu.VMEM((H_RET, blk, LANES), F32)],
        compiler_params=_cparams(("parallel", "arbitrary"), est),
        name="retention",
    )(dec, proj, proj, proj, proj, proj, proj, cos, sin, gn_w.reshape(1, wv_ret), state0)


def _merge_kernel(osb_ref, or_ref, asb0_ref, asb1_ref, ar0_ref, ar1_ref, h_ref,
                  wsb_ref, wr_ref, wo_ref, out_ref):
    b_sb = jnp.dot(osb_ref[...], wsb_ref[...], preferred_element_type=F32)
    b_r = jnp.dot(or_ref[...], wr_ref[...], preferred_element_type=F32)
    a_sb = jnp.concatenate([asb0_ref[...], asb1_ref[...]], axis=1).astype(F32)
    a_r = jnp.concatenate([ar0_ref[...], ar1_ref[...]], axis=1).astype(F32)
    merged = jax.nn.sigmoid(a_sb) * b_sb + jax.nn.sigmoid(a_r) * b_r
    out_ref[...] = h_ref[...] + jnp.dot(merged.astype(BF16), wo_ref[...], preferred_element_type=F32)


def _merge(o_sb, o_r, proj, h, w_sb_o, w_ret_o, w_out, *, w_sb, wk_ret, wv_ret):
    n, d = h.shape
    tm = _row_tile(n, 512)
    half = d // 2
    a0 = (3 * w_sb + 2 * wk_ret + 2 * wv_ret) // half
    assert (3 * w_sb + 2 * wk_ret + 2 * wv_ret) % half == 0
    gate = lambda t: pl.BlockSpec((tm, half), lambda i: (i, a0 + t))
    full = lambda a: pl.BlockSpec(a.shape, lambda i: (0, 0))
    est = 2 * tm * (w_sb + wv_ret + 2 * d) * 2 + 4 * tm * d * 4 + 2 * (w_sb + wv_ret + d) * d * 2 + 6 * tm * d * 4
    return pl.pallas_call(
        _merge_kernel,
        grid=(n // tm,),
        in_specs=[
            pl.BlockSpec((tm, w_sb), lambda i: (i, 0)),
            pl.BlockSpec((tm, wv_ret), lambda i: (i, 0)),
            gate(0), gate(1), gate(2), gate(3),
            pl.BlockSpec((tm, d), lambda i: (i, 0)),
            full(w_sb_o), full(w_ret_o), full(w_out),
        ],
        out_specs=pl.BlockSpec((tm, d), lambda i: (i, 0)),
        out_shape=jax.ShapeDtypeStruct((n, d), F32),
        compiler_params=_cparams(("parallel",), est),
        name="merge",
    )(o_sb, o_r, proj, proj, proj, proj, h, w_sb_o, w_ret_o, w_out)


def _finish(h, acc, fw_ref, out_ref):
    y = h + acc
    if fw_ref is not None:
        y = _rms_scale(y, fw_ref[...])
    out_ref[...] = y


def _ffn_kernel(h_ref, nw_ref, wg_ref, wu_ref, wd_ref, *rest, final, tf):
    fw_ref = rest[0] if final else None
    out_ref = rest[-1]
    h = h_ref[...]
    hn = _rms_scale(h, nw_ref[...]).astype(BF16)
    acc = None
    for f in range(wg_ref.shape[1] // tf):
        cols = slice(f * tf, (f + 1) * tf)
        g = jnp.dot(hn, wg_ref[:, cols], preferred_element_type=F32)
        u = jnp.dot(hn, wu_ref[:, cols], preferred_element_type=F32)
        act = (g * jax.nn.sigmoid(g) * u).astype(BF16)
        t = jnp.dot(act, wd_ref[cols, :], preferred_element_type=F32)
        acc = t if acc is None else acc + t
    _finish(h, acc, fw_ref, out_ref)


def _ffn(h, norm_w, wg, wu, wd, final_w):
    n, d = h.shape
    ff = wg.shape[1]
    tm = _row_tile(n, 512)
    tf = 2 * LANES
    assert ff % tf == 0
    final = final_w is not None
    resident = lambda a: pl.BlockSpec(a.shape, lambda i: (0, 0), pipeline_mode=pl.Buffered(1))
    in_specs = [
        pl.BlockSpec((tm, d), lambda i: (i, 0)),
        pl.BlockSpec((1, d), lambda i: (0, 0)),
        resident(wg), resident(wu), resident(wd),
    ]
    args = [h, norm_w.reshape(1, d), wg, wu, wd]
    if final:
        in_specs.append(pl.BlockSpec((1, d), lambda i: (0, 0)))
        args.append(final_w.reshape(1, d))
    est = 4 * tm * d * 4 + tm * d * 2 + 3 * d * ff * 2 + 4 * tm * tf * 4 + 2 * tm * d * 4
    return pl.pallas_call(
        functools.partial(_ffn_kernel, final=final, tf=tf),
        grid=(n // tm,),
        in_specs=in_specs,
        out_specs=pl.BlockSpec((tm, d), lambda i: (i, 0)),
        out_shape=jax.ShapeDtypeStruct((n, d), F32),
        compiler_params=_cparams(("parallel",), est),
        name="ffn_dense",
    )(*args)


def _moe_kernel(h_ref, nw_ref, rw_ref, wg_ref, wu_ref, wd_ref, *rest, final, n_exp):
    fw_ref = rest[0] if final else None
    out_ref, hn_ref, acc_ref, comb_ref = rest[1:] if final else rest
    e = pl.program_id(1)

    @pl.when(e == 0)
    def _():
        hn = _rms_scale(h_ref[...], nw_ref[...])
        hn_ref[...] = hn.astype(BF16)
        acc_ref[...] = jnp.zeros_like(acc_ref)
        logits = jnp.dot(hn, rw_ref[...], preferred_element_type=F32, precision=lax.Precision.HIGHEST)
        col = lax.broadcasted_iota(jnp.int32, logits.shape, 1).astype(F32)
        logits = jnp.where(col < n_exp, logits, -jnp.inf)
        m1 = jnp.max(logits, axis=-1, keepdims=True)
        i1 = jnp.min(jnp.where(logits == m1, col, float(LANES)), axis=-1, keepdims=True)
        rest_l = jnp.where(col == i1, -jnp.inf, logits)
        m2 = jnp.max(rest_l, axis=-1, keepdims=True)
        i2 = jnp.min(jnp.where(rest_l == m2, col, float(LANES)), axis=-1, keepdims=True)
        ex = jnp.exp(m2 - m1)
        w1 = 1.0 / (1.0 + ex)
        w2 = ex / (1.0 + ex)
        for x in range(n_exp):
            wx = jnp.where(i1 == x, w1, 0.0) + jnp.where(i2 == x, w2, 0.0)
            comb_ref[x] = jnp.broadcast_to(wx, comb_ref.shape[1:])

    hn = hn_ref[...]
    g = jnp.dot(hn, wg_ref[...], preferred_element_type=F32)
    u = jnp.dot(hn, wu_ref[...], preferred_element_type=F32)
    c = comb_ref[e]
    cw = jnp.concatenate([c] * (g.shape[1] // LANES), axis=1)
    act = (g * jax.nn.sigmoid(g) * u * cw).astype(BF16)
    acc_ref[...] += jnp.dot(act, wd_ref[...], preferred_element_type=F32)

    @pl.when(e == n_exp - 1)
    def _():
        _finish(h_ref[...], acc_ref[...], fw_ref, out_ref)


def _moe(h, norm_w, router_w, wg, wu, wd, final_w):
    n, d = h.shape
    n_exp, _, fp = wg.shape
    tm = _row_tile(n, 1024)
    final = final_w is not None
    rw = jnp.pad(router_w.astype(F32), ((0, 0), (0, LANES - n_exp)))
    in_specs = [
        pl.BlockSpec((tm, d), lambda i, e: (i, 0)),
        pl.BlockSpec((1, d), lambda i, e: (0, 0)),
        pl.BlockSpec((d, LANES), lambda i, e: (0, 0)),
        pl.BlockSpec((None, d, fp), lambda i, e: (e, 0, 0)),
        pl.BlockSpec((None, d, fp), lambda i, e: (e, 0, 0)),
        pl.BlockSpec((None, fp, d), lambda i, e: (e, 0, 0)),
    ]
    args = [h, norm_w.reshape(1, d), rw, wg, wu, wd]
    if final:
        in_specs.append(pl.BlockSpec((1, d), lambda i, e: (0, 0)))
        args.append(final_w.reshape(1, d))
    est = 4 * tm * d * 4 + tm * d * 6 + 6 * d * fp * 2 + 4 * tm * fp * 4 + n_exp * tm * LANES * 4 + 2 * d * LANES * 4
    return pl.pallas_call(
        functools.partial(_moe_kernel, final=final, n_exp=n_exp),
        grid=(n // tm, n_exp),
        in_specs=in_specs,
        out_specs=pl.BlockSpec((tm, d), lambda i, e: (i, 0)),
        out_shape=jax.ShapeDtypeStruct((n, d), F32),
        scratch_shapes=[pltpu.VMEM((tm, d), BF16), pltpu.VMEM((tm, d), F32),
                        pltpu.VMEM((n_exp, tm, LANES), F32)],
        compiler_params=_cparams(("parallel", "arbitrary"), est),
        name="moe",
    )(*args)


def _pad_axis(a, axis, to):
    pad = [(0, 0)] * a.ndim
    pad[axis] = (0, to - a.shape[axis])
    return jnp.pad(a, pad)


def _run_group(x, pos0, caches, states, weights):
    (norm_mix_w, w_in, w_sb_o, w_ret_o, gn_ret_w, w_out, norm_ffn_w, ffn_w_gate, ffn_w_up,
     ffn_w_down, router_w, moe_w_gate, moe_w_up, moe_w_down, norm_final_w) = weights
    batch, seq, d = x.shape
    depth = w_in.shape[0]
    w_sb = w_sb_o.shape[1]
    wv_ret = w_ret_o.shape[1]
    d_in = w_in.shape[2]
    wk_ret = (d_in - 3 * w_sb - 2 * wv_ret - 2 * d) // 2
    dims = dict(w_sb=w_sb, wk_ret=wk_ret, wv_ret=wv_ret)
    dk, dv = wk_ret // H_RET, wv_ret // H_RET
    hd = w_sb // H_SB

    h = x.reshape(batch * seq, d)
    kv = None
    finals = []
    if caches is not None:
        kc = caches[0].reshape(depth * batch, -1, hd)
        vc = caches[1].reshape(depth * batch, -1, hd)
    for l in range(depth):
        proj, k_all, v_all = _in_projection(h, norm_mix_w[l], w_in[l], l, depth, kv, w_sb=w_sb)
        kv = (k_all, v_all)
        if caches is None:
            o_sb = _sb_attention_prompt(proj, batch, seq, w_sb=w_sb)
            s0 = jnp.zeros((batch, H_RET, dk, dv), F32)
        else:
            o_sb = _sb_attention_sample(proj, kc, vc, l, batch, seq, w_sb=w_sb)
            s0 = states[l].astype(F32)
        o_r, s_fin = _retention(proj, gn_ret_w[l], s0, batch, seq, pos0, **dims)
        finals.append(s_fin)
        h = _merge(o_sb, o_r, proj, h, w_sb_o[l], w_ret_o[l], w_out[l], **dims)
        fw = norm_final_w if l == depth - 1 else None
        if l % 2 == 0:
            h = _ffn(h, norm_ffn_w[l], ffn_w_gate[l // 2], ffn_w_up[l // 2], ffn_w_down[l // 2], fw)
        else:
            h = _moe(h, norm_ffn_w[l], router_w[l // 2], moe_w_gate[l // 2], moe_w_up[l // 2],
                     moe_w_down[l // 2], fw)
    y = h.reshape(batch, seq, d)
    new_k = kv[0].reshape(depth, batch, seq, H_SB, hd)
    new_v = kv[1].reshape(depth, batch, seq, H_SB, hd)
    return y, new_k, new_v, jnp.stack(finals)


def kernel(x_prompt, x_sample, cache_sb_k, cache_sb_v, state_ret, norm_mix_w, w_in, w_sb_o, w_ret_o,
           gn_ret_w, w_out, norm_ffn_w, ffn_w_gate, ffn_w_up, ffn_w_down, router_w, moe_w_gate,
           moe_w_up, moe_w_down, norm_final_w):
    fe = moe_w_gate.shape[-1]
    fp = -(-fe // LANES) * LANES
    weights = (
        norm_mix_w, w_in.astype(BF16), w_sb_o.astype(BF16), w_ret_o.astype(BF16), gn_ret_w,
        w_out.astype(BF16), norm_ffn_w, ffn_w_gate.astype(BF16), ffn_w_up.astype(BF16),
        ffn_w_down.astype(BF16), router_w,
        _pad_axis(moe_w_gate.astype(BF16), 3, fp), _pad_axis(moe_w_up.astype(BF16), 3, fp),
        _pad_axis(moe_w_down.astype(BF16), 2, fp), norm_final_w,
    )
    past = cache_sb_k.shape[2]
    y_p, k_p, v_p, r_p = _run_group(x_prompt, 0.0, None, None, weights)
    y_s, k_s, v_s, r_s = _run_group(x_sample, float(past), (cache_sb_k, cache_sb_v), state_ret, weights)
    return (y_p, y_s, k_p, v_p, r_p, k_s, v_s, r_s)
```

```python
import functools

import jax
import jax.numpy as jnp
from jax import lax
from jax.experimental import pallas as pl
from jax.experimental.pallas import tpu as pltpu

F32 = jnp.float32
BF16 = jnp.bfloat16

H_SB = 4
H_RET = 4
TOP_K = 2
ROPE_BASE = 10000.0
EPS = 1e-6
GN_EPS = 1e-5
RET_CHUNK_MAX = 256

LANES = 128
V7X_VMEM_BYTES = 64 * 1024 * 1024
VMEM_CAP_BYTES = V7X_VMEM_BYTES - 8 * 1024 * 1024

SB_EXIT_MASS = 104.0
SB_BLOCK = 256


def _cparams(sem, est_bytes):
    limit = int(min(VMEM_CAP_BYTES, max(32 * 1024 * 1024, 2 * est_bytes)))
    return pltpu.CompilerParams(dimension_semantics=sem, vmem_limit_bytes=limit)


def _row_tile(n, want):
    t = min(n, want)
    assert n % t == 0, (n, t)
    return t


def _rms_scale(x, w):
    ms = jnp.mean(x * x, axis=-1, keepdims=True)
    return x * lax.rsqrt(ms + EPS) * w


def _inproj_kernel(x_ref, nw_ref, w_ref, *rest, q_scale, n_alias, tm, tn):
    proj_ref, k_ref, v_ref = rest[n_alias:]
    hn = _rms_scale(x_ref[...], nw_ref[...]).astype(BF16)

    def store_heads(dst_ref, acc):
        for hh in range(H_SB):
            dst_ref[pl.ds(hh, tm, stride=H_SB), :] = acc[:, hh * LANES:(hh + 1) * LANES]

    for j in range(w_ref.shape[1] // tn):
        cols = slice(j * tn, (j + 1) * tn)
        acc = jnp.dot(hn, w_ref[:, cols], preferred_element_type=F32)
        if j == 1:
            store_heads(k_ref, acc)
        if j == 2:
            store_heads(v_ref, acc)
        proj_ref[:, cols] = (acc * q_scale if j == 0 else acc).astype(BF16)


def _in_projection(x, norm_w, w_bf, layer, depth, kv_prev, *, w_sb):
    n, d = x.shape
    d_in = w_bf.shape[1]
    tn = w_sb
    assert d_in % tn == 0
    tm = _row_tile(n, 512)
    hd = w_sb // H_SB
    assert hd == LANES
    nm = n // tm
    kv_shape = jax.ShapeDtypeStruct((depth * n * H_SB, hd), F32)
    kv_spec = pl.BlockSpec((tm * H_SB, hd), lambda i: (layer * nm + i, 0))
    in_specs = [
        pl.BlockSpec((tm, d), lambda i: (i, 0)),
        pl.BlockSpec((1, d), lambda i: (0, 0)),
        pl.BlockSpec((d, d_in), lambda i: (0, 0), pipeline_mode=pl.Buffered(1)),
    ]
    args = [x, norm_w.reshape(1, d), w_bf]
    aliases = {}
    n_alias = 0
    if kv_prev is not None:
        in_specs += [pl.BlockSpec(memory_space=pl.ANY), pl.BlockSpec(memory_space=pl.ANY)]
        args += [kv_prev[0], kv_prev[1]]
        aliases = {3: 1, 4: 2}
        n_alias = 2
    est = 2 * tm * d * 4 + tm * d * 2 + d * d_in * 2 + 2 * tm * d_in * 2 + 4 * tm * w_sb * 4 + 4 * tm * tn * 4
    proj, k_all, v_all = pl.pallas_call(
        functools.partial(_inproj_kernel, q_scale=hd ** -0.5, n_alias=n_alias, tm=tm, tn=tn),
        grid=(nm,),
        in_specs=in_specs,
        out_specs=[pl.BlockSpec((tm, d_in), lambda i: (i, 0)), kv_spec, kv_spec],
        out_shape=[jax.ShapeDtypeStruct((n, d_in), BF16), kv_shape, kv_shape],
        input_output_aliases=aliases,
        compiler_params=_cparams(("parallel",), est),
        name=f"in_projection_l{layer}",
    )(*args)
    return proj, k_all, v_all


def _cumsum_matrix():
    j = jnp.arange(LANES)[:, None]
    s = jnp.arange(LANES)[None, :]
    tri = (j >= s).astype(BF16)
    m = jnp.concatenate([tri, jnp.ones((LANES, LANES), BF16)], axis=1)
    return jnp.concatenate([m, m], axis=0)


def _sb_step_phases(q, chunks, r, cum):
    zs = [lax.dot_general(q, k, (((1,), (1,)), ((), ())), preferred_element_type=F32)
          for k, _, _, _ in chunks]
    yield
    groups = []
    for (k, _, masks, live), z in zip(chunks, zs):
        for g in range(k.shape[0] // LANES):
            zg = z[:, g * LANES:(g + 1) * LANES]
            sp = jnp.maximum(zg, 0.0) + jnp.log(1.0 + jnp.exp(-jnp.abs(zg)))
            m = None if masks is None else masks[g]
            if m is not None:
                sp = jnp.where(m, sp, 0.0)
            if live is not None:
                sp = jnp.where(live, sp, 0.0)
            hi = sp.astype(BF16)
            lo = (sp - hi.astype(F32)).astype(BF16)
            groups.append([zg, jnp.concatenate([hi, lo], axis=1), m, live])
    yield
    for grp in groups:
        grp[1] = jnp.dot(grp[1], cum, preferred_element_type=F32)
    yield
    weights = [None] * len(groups)
    for idx in reversed(range(len(groups))):
        zg, ct, m, live = groups[idx]
        a = jnp.exp(zg - ct[:, :LANES] - r)
        if m is not None:
            a = jnp.where(m, a, 0.0)
        if live is not None:
            a = jnp.where(live, a, 0.0)
        weights[idx] = a.astype(BF16)
        r = r + ct[:, LANES:]
    yield
    pv = None
    idx = 0
    for _, v, _, _ in chunks:
        n_g = v.shape[0] // LANES
        a = weights[idx] if n_g == 1 else jnp.concatenate(weights[idx:idx + n_g], axis=1)
        t = jnp.dot(a, v, preferred_element_type=F32)
        pv = t if pv is None else pv + t
        idx += n_g
    return pv, r


def _lockstep(gens):
    results = [None] * len(gens)
    running = list(range(len(gens)))
    while running:
        still = []
        for t in running:
            try:
                next(gens[t])
                still.append(t)
            except StopIteration as stop:
                results[t] = stop.value
        running = still
    return results


def _sb_step(q, chunks, r, cum):
    return _lockstep([_sb_step_phases(q, chunks, r, cum)])[0]


def _sb_prompt_kernel(q_ref, k_ref, v_ref, cum_ref, o_ref, r_ref, acc_ref, *, qb, heads):
    i = pl.program_id(2)
    cum = cum_ref[...]
    row = lax.broadcasted_iota(jnp.int32, (qb, LANES), 0)
    col = lax.broadcasted_iota(jnp.int32, (qb, LANES), 1)
    causal = [(col + g * LANES) < row for g in range(qb // LANES)]
    has_prev = i > 0
    prev0 = pl.multiple_of(jnp.maximum(i - 1, 0) * qb, qb)
    diag0 = pl.multiple_of(i * qb, qb)
    zero = jnp.zeros((qb, LANES), F32)

    steps = []
    for hh in range(heads):
        cs = slice(hh * LANES, (hh + 1) * LANES)
        chunks = [
            (k_ref[pl.ds(prev0, qb), cs], v_ref[pl.ds(prev0, qb), cs], None, has_prev),
            (k_ref[pl.ds(diag0, qb), cs], v_ref[pl.ds(diag0, qb), cs], causal, None),
        ]
        steps.append(_sb_step_phases(q_ref[:, cs], chunks, zero, cum))
    for hh, (pv, r) in enumerate(_lockstep(steps)):
        acc_ref[hh] = pv
        r_ref[hh] = r

    for hh in range(heads):
        cs = slice(hh * LANES, (hh + 1) * LANES)

        def cond(j, hh=hh):
            return jnp.logical_and(j >= 0, jnp.min(r_ref[hh]) < SB_EXIT_MASS)

        def body(j, hh=hh, cs=cs):
            start = pl.multiple_of(j * qb, qb)
            chunks = [(k_ref[pl.ds(start, qb), cs], v_ref[pl.ds(start, qb), cs], None, None)]
            pv, r = _sb_step(q_ref[:, cs], chunks, r_ref[hh], cum)
            acc_ref[hh] += pv
            r_ref[hh] = r
            return j - 1

        lax.while_loop(cond, body, i - 2)

    for hh in range(heads):
        o_ref[:, hh * LANES:(hh + 1) * LANES] = acc_ref[hh].astype(o_ref.dtype)


def _sb_attention_prompt(proj, batch, seq, *, w_sb):
    n = proj.shape[0]
    hd = w_sb // H_SB
    assert hd == LANES
    qb = SB_BLOCK
    assert seq % qb == 0
    nq = seq // qb
    heads = H_SB
    wb = heads * hd
    kcol = w_sb // wb
    vcol = 2 * w_sb // wb
    est = 4 * qb * wb * 2 + 2 * seq * wb * 2 + 4 * heads * qb * LANES * 4 + 40 * qb * LANES * 4 * heads
    resident = lambda c0: pl.BlockSpec((seq, wb), lambda b, h, i: (b, c0 + h), pipeline_mode=pl.Buffered(1))
    return pl.pallas_call(
        functools.partial(_sb_prompt_kernel, qb=qb, heads=heads),
        grid=(batch, H_SB // heads, nq),
        in_specs=[
            pl.BlockSpec((qb, wb), lambda b, h, i: (b * nq + i, h)),
            resident(kcol),
            resident(vcol),
            pl.BlockSpec((2 * LANES, 2 * LANES), lambda b, h, i: (0, 0)),
        ],
        out_specs=pl.BlockSpec((qb, wb), lambda b, h, i: (b * nq + i, h)),
        out_shape=jax.ShapeDtypeStruct((n, w_sb), BF16),
        scratch_shapes=[pltpu.VMEM((heads, qb, LANES), F32), pltpu.VMEM((heads, qb, LANES), F32)],
        compiler_params=_cparams(("parallel", "parallel", "arbitrary"), est),
        name="sb_attention_prompt",
    )(proj, proj, proj, _cumsum_matrix())


def _sb_sample_kernel(q_ref, kn_ref, vn_ref, kc_hbm, vc_hbm, cum_ref, o_ref, r_ref, acc_ref, buf_ref, sem_ref,
                      *, tq, past, cb, stream0, n_streams):
    b = pl.program_id(0)
    cum = cum_ref[...]
    row = lax.broadcasted_iota(jnp.int32, (tq, LANES), 0)
    col = lax.broadcasted_iota(jnp.int32, (tq, LANES), 1)
    zero = jnp.zeros((tq, LANES), F32)
    rows = cb * H_SB
    latest = past - cb

    def chunk_copy(stream, start, slot, which):
        src = (kc_hbm, vc_hbm)[which].at[stream0 + stream, pl.ds(start * H_SB, rows), :]
        return pltpu.make_async_copy(src, buf_ref.at[slot, which], sem_ref.at[slot, which])

    def cached(slot, which, hh):
        return buf_ref[slot, which, pl.ds(hh, cb, stride=H_SB), :].astype(BF16)

    slot = lax.rem(b, 2)

    @pl.when(b == 0)
    def _():
        for which in range(2):
            chunk_copy(0, latest, 0, which).start()

    for which in range(2):
        chunk_copy(b, latest, slot, which).wait()

    @pl.when(b + 1 < n_streams)
    def _():
        for which in range(2):
            chunk_copy(b + 1, latest, 1 - slot, which).start()

    steps = []
    for hh in range(H_SB):
        cs = slice(hh * LANES, (hh + 1) * LANES)
        chunks = [
            (cached(slot, 0, hh), cached(slot, 1, hh), None, None),
            (kn_ref[:, cs], vn_ref[:, cs], [col < row], None),
        ]
        steps.append(_sb_step_phases(q_ref[:, cs], chunks, zero, cum))
    for hh, (pv, r) in enumerate(_lockstep(steps)):
        acc_ref[hh] = pv
        r_ref[hh] = r

    def cond(j):
        return jnp.logical_and(j >= 0, jnp.min(r_ref[...]) < SB_EXIT_MASS)

    def body(j):
        for which in range(2):
            chunk_copy(b, j * cb, 2, which).start()
        for which in range(2):
            chunk_copy(b, j * cb, 2, which).wait()
        steps = []
        for hh in range(H_SB):
            cs = slice(hh * LANES, (hh + 1) * LANES)
            chunks = [(cached(2, 0, hh), cached(2, 1, hh), None, None)]
            steps.append(_sb_step_phases(q_ref[:, cs], chunks, r_ref[hh], cum))
        for hh, (pv, r) in enumerate(_lockstep(steps)):
            acc_ref[hh] += pv
            r_ref[hh] = r
        return j - 1

    lax.while_loop(cond, body, past // cb - 2)

    for hh in range(H_SB):
        o_ref[:, hh * LANES:(hh + 1) * LANES] = acc_ref[hh].astype(o_ref.dtype)


def _sb_attention_sample(proj, k_cache, v_cache, layer, batch, tq, *, w_sb):
    n = proj.shape[0]
    hd = w_sb // H_SB
    past = k_cache.shape[1] // H_SB
    cb = SB_BLOCK if past % SB_BLOCK == 0 else LANES
    assert hd == LANES and past % cb == 0 and tq <= LANES
    kv_new = proj[:, w_sb:3 * w_sb].reshape(batch, tq, 2 * w_sb)
    kv_new = jnp.pad(kv_new, ((0, 0), (0, LANES - tq), (0, 0)))
    est = 6 * cb * w_sb * 4 + 8 * LANES * w_sb * 2 + 40 * H_SB * max(tq, 8) * LANES * 4
    return pl.pallas_call(
        functools.partial(_sb_sample_kernel, tq=tq, past=past, cb=cb, stream0=layer * batch, n_streams=batch),
        grid=(batch,),
        in_specs=[
            pl.BlockSpec((tq, w_sb), lambda b: (b, 0)),
            pl.BlockSpec((None, LANES, w_sb), lambda b: (b, 0, 0)),
            pl.BlockSpec((None, LANES, w_sb), lambda b: (b, 0, 1)),
            pl.BlockSpec(memory_space=pl.ANY),
            pl.BlockSpec(memory_space=pl.ANY),
            pl.BlockSpec((2 * LANES, 2 * LANES), lambda b: (0, 0)),
        ],
        out_specs=pl.BlockSpec((tq, w_sb), lambda b: (b, 0)),
        out_shape=jax.ShapeDtypeStruct((n, w_sb), BF16),
        scratch_shapes=[pltpu.VMEM((H_SB, tq, LANES), F32), pltpu.VMEM((H_SB, tq, LANES), F32),
                        pltpu.VMEM((3, 2, cb * H_SB, hd), F32), pltpu.SemaphoreType.DMA((3, 2))],
        compiler_params=_cparams(("arbitrary",), est),
        name="sb_attention_sample",
    )(proj, kv_new, kv_new, k_cache, v_cache, _cumsum_matrix())


def _retention_kernel(dec_ref, q_ref, k_ref, va_ref, vb_ref, ga_ref, gb_ref, cos_ref, sin_ref, gnw_ref,
                      s0_ref, o_ref, sout_ref, s_ref, d_ref, qd_ref, kd_ref, *, blk, k_scale, dk, dv):
    c = pl.program_id(1)
    per_half = H_RET // 2

    @pl.when(c == 0)
    def _():
        s_ref[...] = s0_ref[...]
        n = lax.broadcasted_iota(jnp.int32, (blk, blk), 0)
        m = lax.broadcasted_iota(jnp.int32, (blk, blk), 1)
        causal = n >= m
        diff = jnp.where(causal, n - m, 0).astype(F32)
        rows = lax.broadcasted_iota(jnp.int32, (blk, LANES), 0).astype(F32)
        for hh in range(H_RET):
            lg = dec_ref[0, hh]
            d_ref[hh] = jnp.where(causal, jnp.exp(diff * lg), 0.0)
            qd_ref[hh] = jnp.exp((rows + 1.0) * lg)
            kd_ref[hh] = jnp.exp((blk - 1.0 - rows) * lg)

    cos = cos_ref[...]
    sin = sin_ref[...]
    half = dk // 2
    for hh in range(H_RET):
        ks = slice(hh * dk, (hh + 1) * dk)
        vs = slice(hh * dv, (hh + 1) * dv)
        qf = q_ref[:, ks].astype(F32)
        kf = k_ref[:, ks].astype(F32)
        qr = qf * cos + pltpu.roll(qf, half, 1) * sin
        kr = (kf * cos + pltpu.roll(kf, half, 1) * sin) * k_scale
        hs = slice((hh % per_half) * dv, (hh % per_half + 1) * dv)
        v = (va_ref if hh < per_half else vb_ref)[:, hs]

        scores = lax.dot_general(qr.astype(BF16), kr.astype(BF16), (((1,), (1,)), ((), ())),
                                 preferred_element_type=F32) * d_ref[hh]
        inner = jnp.dot(scores.astype(BF16), v, preferred_element_type=F32)
        s = s_ref[hh]
        cross = jnp.dot((qr * qd_ref[hh]).astype(BF16), s.astype(BF16), preferred_element_type=F32)
        o = inner + cross
        kv = lax.dot_general((kr * kd_ref[hh]).astype(BF16), v, (((0,), (0,)), ((), ())),
                             preferred_element_type=F32)
        s_new = dec_ref[1, hh] * s + kv
        s_ref[hh] = s_new
        sout_ref[hh] = s_new

        mu = jnp.mean(o, axis=-1, keepdims=True)
        dev = o - mu
        var = jnp.mean(dev * dev, axis=-1, keepdims=True)
        y = dev * lax.rsqrt(var + GN_EPS) * gnw_ref[:, vs]
        gf = (ga_ref if hh < per_half else gb_ref)[:, hs].astype(F32)
        o_ref[:, vs] = (gf * jax.nn.sigmoid(gf) * y).astype(o_ref.dtype)


def _retention(proj, gn_w, state0, batch, seq, pos0, *, w_sb, wk_ret, wv_ret):
    n = proj.shape[0]
    dk = wk_ret // H_RET
    dv = wv_ret // H_RET
    assert dk == LANES and dv % LANES == 0
    blk = min(seq, RET_CHUNK_MAX)
    assert seq % blk == 0
    nc = seq // blk
    hw = wv_ret // 2
    assert (3 * w_sb) % wk_ret == 0 and (3 * w_sb + 2 * wk_ret) % hw == 0 and H_RET % 2 == 0
    q0 = 3 * w_sb // wk_ret
    k0 = q0 + 1
    v0 = (3 * w_sb + 2 * wk_ret) // hw
    g0 = v0 + 2

    half = dk // 2
    inv = ROPE_BASE ** (-jnp.arange(half, dtype=F32) / half)
    ang = (pos0 + jnp.arange(seq, dtype=F32))[:, None] * inv[None, :]
    cos = jnp.concatenate([jnp.cos(ang), jnp.cos(ang)], axis=1)
    sin = jnp.concatenate([-jnp.sin(ang), jnp.sin(ang)], axis=1)
    log_gamma = jnp.log1p(-jnp.exp2(-5.0 - jnp.arange(H_RET, dtype=F32)))
    dec = jnp.stack([log_gamma, jnp.exp(blk * log_gamma)])

    est = 2 * blk * (2 * wk_ret + 2 * wv_ret) * 2 + 4 * blk * dk * 4 + 5 * H_RET * dk * dv * 4 \
        + 2 * blk * wv_ret * 2 + H_RET * (blk * blk + 2 * blk * LANES) * 4 + 12 * H_RET * blk * dv * 4
    row = lambda b, c: b * nc + c
    return pl.pallas_call(
        functools.partial(_retention_kernel, blk=blk, k_scale=dk ** -0.5, dk=dk, dv=dv),
        grid=(batch, nc),
        in_specs=[
            pl.BlockSpec(memory_space=pltpu.SMEM),
            pl.BlockSpec((blk, wk_ret), lambda b, c: (row(b, c), q0)),
            pl.BlockSpec((blk, wk_ret), lambda b, c: (row(b, c), k0)),
            pl.BlockSpec((blk, hw), lambda b, c: (row(b, c), v0)),
            pl.BlockSpec((blk, hw), lambda b, c: (row(b, c), v0 + 1)),
            pl.BlockSpec((blk, hw), lambda b, c: (row(b, c), g0)),
            pl.BlockSpec((blk, hw), lambda b, c: (row(b, c), g0 + 1)),
            pl.BlockSpec((blk, dk), lambda b, c: (c, 0)),
            pl.BlockSpec((blk, dk), lambda b, c: (c, 0)),
            pl.BlockSpec((1, wv_ret), lambda b, c: (0, 0)),
            pl.BlockSpec((None, H_RET, dk, dv), lambda b, c: (b, 0, 0, 0)),
        ],
        out_specs=[
            pl.BlockSpec((blk, wv_ret), lambda b, c: (row(b, c), 0)),
            pl.BlockSpec((None, H_RET, dk, dv), lambda b, c: (b, 0, 0, 0)),
        ],
        out_shape=[jax.ShapeDtypeStruct((n, wv_ret), BF16),
                   jax.ShapeDtypeStruct((batch, H_RET, dk, dv), F32)],
        scratch_shapes=[pltpu.VMEM((H_RET, dk, dv), F32), pltpu.VMEM((H_RET, blk, blk), F32),
                        pltpu.VMEM((H_RET, blk, LANES), F32), pltpu.VMEM((H_RET, blk, LANES), F32)],
        compiler_params=_cparams(("parallel", "arbitrary"), est),
        name="retention",
    )(dec, proj, proj, proj, proj, proj, proj, cos, sin, gn_w.reshape(1, wv_ret), state0)


def _merge_kernel(osb_ref, or_ref, asb0_ref, asb1_ref, ar0_ref, ar1_ref, h_ref,
                  wsb_ref, wr_ref, wo_ref, out_ref):
    b_sb = jnp.dot(osb_ref[...], wsb_ref[...], preferred_element_type=F32)
    b_r = jnp.dot(or_ref[...], wr_ref[...], preferred_element_type=F32)
    a_sb = jnp.concatenate([asb0_ref[...], asb1_ref[...]], axis=1).astype(F32)
    a_r = jnp.concatenate([ar0_ref[...], ar1_ref[...]], axis=1).astype(F32)
    merged = jax.nn.sigmoid(a_sb) * b_sb + jax.nn.sigmoid(a_r) * b_r
    out_ref[...] = h_ref[...] + jnp.dot(merged.astype(BF16), wo_ref[...], preferred_element_type=F32)


def _merge(o_sb, o_r, proj, h, w_sb_o, w_ret_o, w_out, *, w_sb, wk_ret, wv_ret):
    n, d = h.shape
    tm = _row_tile(n, 512)
    half = d // 2
    a0 = (3 * w_sb + 2 * wk_ret + 2 * wv_ret) // half
    assert (3 * w_sb + 2 * wk_ret + 2 * wv_ret) % half == 0
    gate = lambda t: pl.BlockSpec((tm, half), lambda i: (i, a0 + t))
    full = lambda a: pl.BlockSpec(a.shape, lambda i: (0, 0))
    est = 2 * tm * (w_sb + wv_ret + 2 * d) * 2 + 4 * tm * d * 4 + 2 * (w_sb + wv_ret + d) * d * 2 + 6 * tm * d * 4
    return pl.pallas_call(
        _merge_kernel,
        grid=(n // tm,),
        in_specs=[
            pl.BlockSpec((tm, w_sb), lambda i: (i, 0)),
            pl.BlockSpec((tm, wv_ret), lambda i: (i, 0)),
            gate(0), gate(1), gate(2), gate(3),
            pl.BlockSpec((tm, d), lambda i: (i, 0)),
            full(w_sb_o), full(w_ret_o), full(w_out),
        ],
        out_specs=pl.BlockSpec((tm, d), lambda i: (i, 0)),
        out_shape=jax.ShapeDtypeStruct((n, d), F32),
        compiler_params=_cparams(("parallel",), est),
        name="merge",
    )(o_sb, o_r, proj, proj, proj, proj, h, w_sb_o, w_ret_o, w_out)


def _finish(h, acc, fw_ref, out_ref):
    y = h + acc
    if fw_ref is not None:
        y = _rms_scale(y, fw_ref[...])
    out_ref[...] = y


def _ffn_kernel(h_ref, nw_ref, wg_ref, wu_ref, wd_ref, *rest, final, tf):
    fw_ref = rest[0] if final else None
    out_ref = rest[-1]
    h = h_ref[...]
    hn = _rms_scale(h, nw_ref[...]).astype(BF16)
    acc = None
    for f in range(wg_ref.shape[1] // tf):
        cols = slice(f * tf, (f + 1) * tf)
        g = jnp.dot(hn, wg_ref[:, cols], preferred_element_type=F32)
        u = jnp.dot(hn, wu_ref[:, cols], preferred_element_type=F32)
        act = (g * jax.nn.sigmoid(g) * u).astype(BF16)
        t = jnp.dot(act, wd_ref[cols, :], preferred_element_type=F32)
        acc = t if acc is None else acc + t
    _finish(h, acc, fw_ref, out_ref)


def _ffn(h, norm_w, wg, wu, wd, final_w):
    n, d = h.shape
    ff = wg.shape[1]
    tm = _row_tile(n, 512)
    tf = 2 * LANES
    assert ff % tf == 0
    final = final_w is not None
    resident = lambda a: pl.BlockSpec(a.shape, lambda i: (0, 0), pipeline_mode=pl.Buffered(1))
    in_specs = [
        pl.BlockSpec((tm, d), lambda i: (i, 0)),
        pl.BlockSpec((1, d), lambda i: (0, 0)),
        resident(wg), resident(wu), resident(wd),
    ]
    args = [h, norm_w.reshape(1, d), wg, wu, wd]
    if final:
        in_specs.append(pl.BlockSpec((1, d), lambda i: (0, 0)))
        args.append(final_w.reshape(1, d))
    est = 4 * tm * d * 4 + tm * d * 2 + 3 * d * ff * 2 + 4 * tm * tf * 4 + 2 * tm * d * 4
    return pl.pallas_call(
        functools.partial(_ffn_kernel, final=final, tf=tf),
        grid=(n // tm,),
        in_specs=in_specs,
        out_specs=pl.BlockSpec((tm, d), lambda i: (i, 0)),
        out_shape=jax.ShapeDtypeStruct((n, d), F32),
        compiler_params=_cparams(("parallel",), est),
        name="ffn_dense",
    )(*args)


MOE_TILE = 512


def _interleave_store(dst_ref, lead, x, n_rows):
    groups = x.shape[1] // LANES
    for g in range(groups):
        dst_ref[lead + (pl.ds(g, n_rows, stride=groups), slice(None))] = x[:, g * LANES:(g + 1) * LANES]


def _interleave_load(src_ref, lead, n_rows, groups):
    parts = [src_ref[lead + (pl.ds(g, n_rows, stride=groups), slice(None))] for g in range(groups)]
    return jnp.concatenate(parts, axis=1)


def _moe_route_kernel(h_ref, nw_ref, rwh_ref, rwl_ref, tri_ref, xs_hbm, pos_ref, w_ref, cnt_ref,
                      x3_ref, run_ref, posv_ref, poss_ref, zero_ref, sem_ref, psem_ref, zsem_ref,
                      *, n_exp, cap, pad_to):
    i = pl.program_id(0)
    nt = pl.num_programs(0)
    tm, d = h_ref.shape
    groups = d // LANES
    slot = lax.rem(i, 2)

    def slot_copies(s):
        return pltpu.make_async_copy(x3_ref.at[s], xs_hbm.at[pl.ds(0, tm * groups), :], sem_ref.at[s])

    def row_copy(s, r, p):
        return pltpu.make_async_copy(x3_ref.at[s, pl.ds(r * groups, groups), :],
                                     xs_hbm.at[pl.ds(p * groups, groups), :], sem_ref.at[s])

    @pl.when(i == 0)
    def _():
        run_ref[...] = jnp.zeros_like(run_ref)
        zero_ref[...] = jnp.zeros_like(zero_ref)

    @pl.when(i >= 2)
    def _():
        slot_copies(slot).wait()
        slot_copies(slot).wait()

    hn = _rms_scale(h_ref[...], nw_ref[...])
    _interleave_store(x3_ref, (slot,), hn, tm)

    hn_hi = hn.astype(BF16)
    hn_lo = (hn - hn_hi.astype(F32)).astype(BF16)
    logits = (jnp.dot(hn_hi, rwh_ref[...], preferred_element_type=F32)
              + jnp.dot(hn_lo, rwh_ref[...], preferred_element_type=F32)
              + jnp.dot(hn_hi, rwl_ref[...], preferred_element_type=F32))
    col = lax.broadcasted_iota(jnp.int32, logits.shape, 1).astype(F32)
    logits = jnp.where(col < n_exp, logits, -jnp.inf)
    m1 = jnp.max(logits, axis=-1, keepdims=True)
    i1 = jnp.min(jnp.where(logits == m1, col, float(LANES)), axis=-1, keepdims=True)
    rest_l = jnp.where(col == i1, -jnp.inf, logits)
    m2 = jnp.max(rest_l, axis=-1, keepdims=True)
    i2 = jnp.min(jnp.where(rest_l == m2, col, float(LANES)), axis=-1, keepdims=True)
    ex = jnp.exp(m2 - m1)
    w_ref[...] = jnp.concatenate([jnp.broadcast_to(1.0 / (1.0 + ex), (tm, LANES)),
                                  jnp.broadcast_to(ex / (1.0 + ex), (tm, LANES))], axis=1)

    onehot = jnp.where(jnp.logical_or(col == i1, col == i2), 1.0, 0.0)
    ranks = jnp.dot(tri_ref[...], onehot.astype(BF16), preferred_element_type=F32) + run_ref[...]
    pos1 = i1 * float(cap) + jnp.sum(jnp.where(col == i1, ranks, 0.0), axis=-1, keepdims=True)
    pos2 = i2 * float(cap) + jnp.sum(jnp.where(col == i2, ranks, 0.0), axis=-1, keepdims=True)
    run = run_ref[...] + jnp.sum(onehot, axis=0, keepdims=True)
    run_ref[...] = run

    meta = jnp.where(col == 0.0, pos1, jnp.where(col == 1.0, pos2, 0.0))
    posv = jnp.transpose(meta)[0:TOP_K, :].astype(jnp.int32)
    pos_ref[...] = posv
    posv_ref[...] = posv
    to_smem = pltpu.make_async_copy(posv_ref, poss_ref, psem_ref.at[0])
    to_smem.start()
    to_smem.wait()

    def issue(r, carry):
        row_copy(slot, r, poss_ref[0, r]).start()
        row_copy(slot, r, poss_ref[1, r]).start()
        return carry

    lax.fori_loop(0, tm, issue, 0, unroll=8)

    @pl.when(i == nt - 1)
    def _():
        slot_copies(slot).wait()
        slot_copies(slot).wait()

        @pl.when(nt > 1)
        def _():
            slot_copies(1 - slot).wait()
            slot_copies(1 - slot).wait()

        cnt_ref[...] = run
        lane = lax.broadcasted_iota(jnp.int32, run.shape, 1)
        for x in range(n_exp):
            c = jnp.sum(jnp.where(lane == x, run, 0.0)).astype(jnp.int32)
            end = lax.div(c + (pad_to - 1), pad_to) * pad_to

            def zero_row(r, x=x):
                return pltpu.make_async_copy(zero_ref, xs_hbm.at[pl.ds((x * cap + r) * groups, groups), :],
                                             zsem_ref.at[0])

            def z_start(r, carry, zero_row=zero_row):
                zero_row(r).start()
                return carry

            def z_wait(r, carry, zero_row=zero_row):
                zero_row(r).wait()
                return carry

            lax.fori_loop(c, end, z_start, 0)
            lax.fori_loop(c, end, z_wait, 0)


def _moe_expert_kernel(te_ref, tb_ref, ok_ref, xs_ref, wg_ref, wu_ref, wd_ref, ys_ref, *, rows, groups):
    t = pl.program_id(0)

    @pl.when(ok_ref[t] == 1)
    def _():
        x = _interleave_load(xs_ref, (), rows, groups).astype(BF16)
        g = jnp.dot(x, wg_ref[...], preferred_element_type=F32)
        u = jnp.dot(x, wu_ref[...], preferred_element_type=F32)
        act = (g * jax.nn.sigmoid(g) * u).astype(BF16)
        _interleave_store(ys_ref, (), jnp.dot(act, wd_ref[...], preferred_element_type=F32), rows)


def _moe_combine_kernel(pos_ref, h_ref, w_ref, *rest, final):
    fw_ref = rest[0] if final else None
    ys_hbm, out_ref, g_ref, sem_ref = rest[1:] if final else rest
    tm, d = h_ref.shape
    groups = d // LANES

    def row_copy(k, r):
        return pltpu.make_async_copy(ys_hbm.at[pl.ds(pos_ref[k, r] * groups, groups), :],
                                     g_ref.at[k, pl.ds(r * groups, groups), :], sem_ref.at[k])

    def issue(r, carry):
        row_copy(0, r).start()
        row_copy(1, r).start()
        return carry

    lax.fori_loop(0, tm, issue, 0, unroll=8)
    for k in range(TOP_K):
        pltpu.make_async_copy(ys_hbm.at[pl.ds(0, tm * groups), :], g_ref.at[k], sem_ref.at[k]).wait()

    w = w_ref[...]
    w1 = jnp.concatenate([w[:, :LANES]] * groups, axis=1)
    w2 = jnp.concatenate([w[:, LANES:]] * groups, axis=1)
    y = w1 * _interleave_load(g_ref, (0,), tm, groups) + w2 * _interleave_load(g_ref, (1,), tm, groups)
    _finish(h_ref[...], y, fw_ref, out_ref)


def _moe(h, norm_w, router_w, wg, wu, wd, final_w):
    n, d = h.shape
    n_exp, _, fp = wg.shape
    groups = d // LANES
    assert d % LANES == 0 and groups == 8, "a row must fill one f32 (8,128) tile to move as one contiguous copy"
    tm = _row_tile(n, 1024)
    nt = n // tm
    tmx = min(MOE_TILE, n)
    cap = -(-n // tmx) * tmx
    blocks_per_expert = cap // tmx
    dummy_block = n_exp * blocks_per_expert
    sorted_rows = (n_exp * cap + tmx) * groups
    final = final_w is not None
    rw = jnp.pad(router_w.astype(F32), ((0, 0), (0, LANES - n_exp)))
    rw_hi = rw.astype(BF16)
    rw_lo = (rw - rw_hi.astype(F32)).astype(BF16)
    tri = (jnp.arange(tm)[None, :] < jnp.arange(tm)[:, None]).astype(BF16)

    est = 4 * tm * d * 4 + 4 * d * LANES * 2 + 2 * tm * tm * 2 + 2 * tm * d * 4 + 8 * tm * LANES * 4 + 6 * tm * d * 4
    xs, pos, wts, cnt = pl.pallas_call(
        functools.partial(_moe_route_kernel, n_exp=n_exp, cap=cap, pad_to=tmx),
        grid=(nt,),
        in_specs=[
            pl.BlockSpec((tm, d), lambda i: (i, 0)),
            pl.BlockSpec((1, d), lambda i: (0, 0)),
            pl.BlockSpec((d, LANES), lambda i: (0, 0)),
            pl.BlockSpec((d, LANES), lambda i: (0, 0)),
            pl.BlockSpec((tm, tm), lambda i: (0, 0)),
        ],
        out_specs=[
            pl.BlockSpec(memory_space=pl.ANY),
            pl.BlockSpec((None, TOP_K, tm), lambda i: (i, 0, 0)),
            pl.BlockSpec((tm, TOP_K * LANES), lambda i: (i, 0)),
            pl.BlockSpec((1, LANES), lambda i: (0, 0)),
        ],
        out_shape=[
            jax.ShapeDtypeStruct((sorted_rows, LANES), F32),
            jax.ShapeDtypeStruct((nt, TOP_K, tm), jnp.int32),
            jax.ShapeDtypeStruct((n, TOP_K * LANES), F32),
            jax.ShapeDtypeStruct((1, LANES), F32),
        ],
        scratch_shapes=[
            pltpu.VMEM((2, tm * groups, LANES), F32), pltpu.VMEM((1, LANES), F32),
            pltpu.VMEM((TOP_K, tm), jnp.int32), pltpu.SMEM((TOP_K, tm), jnp.int32),
            pltpu.VMEM((groups, LANES), F32),
            pltpu.SemaphoreType.DMA((2,)), pltpu.SemaphoreType.DMA((1,)), pltpu.SemaphoreType.DMA((1,)),
        ],
        compiler_params=_cparams(("arbitrary",), est),
        name="moe_route",
    )(h, norm_w.reshape(1, d), rw_hi, rw_lo, tri)

    counts = cnt[0, :n_exp].astype(jnp.int32)
    tiles = (counts + (tmx - 1)) // tmx
    first = jnp.cumsum(tiles) - tiles
    n_steps = 2 * n // tmx + n_exp
    step = jnp.arange(n_steps, dtype=jnp.int32)
    te = jnp.clip(jnp.sum((step[:, None] >= (first + tiles)[None, :]).astype(jnp.int32), axis=1), 0, n_exp - 1)
    ok = (step < jnp.sum(tiles)).astype(jnp.int32)
    tb = jnp.where(ok == 1, te * blocks_per_expert + step - first[te], dummy_block).astype(jnp.int32)

    est = 4 * tmx * d * 4 + 6 * d * fp * 2 + 4 * tmx * fp * 4 + 2 * tmx * d * 4
    sorted_spec = pl.BlockSpec((tmx * groups, LANES), lambda t, te, tb, ok: (tb[t], 0))
    ys = pl.pallas_call(
        functools.partial(_moe_expert_kernel, rows=tmx, groups=groups),
        grid_spec=pltpu.PrefetchScalarGridSpec(
            num_scalar_prefetch=3,
            grid=(n_steps,),
            in_specs=[
                sorted_spec,
                pl.BlockSpec((None, d, fp), lambda t, te, tb, ok: (te[t], 0, 0)),
                pl.BlockSpec((None, d, fp), lambda t, te, tb, ok: (te[t], 0, 0)),
                pl.BlockSpec((None, fp, d), lambda t, te, tb, ok: (te[t], 0, 0)),
            ],
            out_specs=sorted_spec,
        ),
        out_shape=jax.ShapeDtypeStruct((sorted_rows, LANES), F32),
        compiler_params=_cparams(("arbitrary",), est),
        name="moe_experts",
    )(te, tb, ok, xs, wg, wu, wd)

    in_specs = [
        pl.BlockSpec((None, TOP_K, tm), lambda i: (i, 0, 0), memory_space=pltpu.SMEM),
        pl.BlockSpec((tm, d), lambda i: (i, 0)),
        pl.BlockSpec((tm, TOP_K * LANES), lambda i: (i, 0)),
    ]
    args = [pos, h, wts]
    if final:
        in_specs.append(pl.BlockSpec((1, d), lambda i: (0, 0)))
        args.append(final_w.reshape(1, d))
    in_specs.append(pl.BlockSpec(memory_space=pl.ANY))
    args.append(ys)
    est = 4 * tm * d * 4 + 2 * tm * TOP_K * LANES * 4 + TOP_K * tm * d * 4 + 6 * tm * d * 4
    return pl.pallas_call(
        functools.partial(_moe_combine_kernel, final=final),
        grid=(nt,),
        in_specs=in_specs,
        out_specs=pl.BlockSpec((tm, d), lambda i: (i, 0)),
        out_shape=jax.ShapeDtypeStruct((n, d), F32),
        scratch_shapes=[pltpu.VMEM((TOP_K, tm * groups, LANES), F32), pltpu.SemaphoreType.DMA((TOP_K,))],
        compiler_params=_cparams(("arbitrary",), est),
        name="moe_combine",
    )(*args)


def _pad_axis(a, axis, to):
    pad = [(0, 0)] * a.ndim
    pad[axis] = (0, to - a.shape[axis])
    return jnp.pad(a, pad)


def _run_group(x, pos0, caches, states, weights):
    (norm_mix_w, w_in, w_sb_o, w_ret_o, gn_ret_w, w_out, norm_ffn_w, ffn_w_gate, ffn_w_up,
     ffn_w_down, router_w, moe_w_gate, moe_w_up, moe_w_down, norm_final_w) = weights
    batch, seq, d = x.shape
    depth = w_in.shape[0]
    w_sb = w_sb_o.shape[1]
    wv_ret = w_ret_o.shape[1]
    d_in = w_in.shape[2]
    wk_ret = (d_in - 3 * w_sb - 2 * wv_ret - 2 * d) // 2
    dims = dict(w_sb=w_sb, wk_ret=wk_ret, wv_ret=wv_ret)
    dk, dv = wk_ret // H_RET, wv_ret // H_RET
    hd = w_sb // H_SB

    h = x.reshape(batch * seq, d)
    kv = None
    finals = []
    if caches is not None:
        kc = caches[0].reshape(depth * batch, -1, hd)
        vc = caches[1].reshape(depth * batch, -1, hd)
    for l in range(depth):
        proj, k_all, v_all = _in_projection(h, norm_mix_w[l], w_in[l], l, depth, kv, w_sb=w_sb)
        kv = (k_all, v_all)
        if caches is None:
            o_sb = _sb_attention_prompt(proj, batch, seq, w_sb=w_sb)
            s0 = jnp.zeros((batch, H_RET, dk, dv), F32)
        else:
            o_sb = _sb_attention_sample(proj, kc, vc, l, batch, seq, w_sb=w_sb)
            s0 = states[l].astype(F32)
        o_r, s_fin = _retention(proj, gn_ret_w[l], s0, batch, seq, pos0, **dims)
        finals.append(s_fin)
        h = _merge(o_sb, o_r, proj, h, w_sb_o[l], w_ret_o[l], w_out[l], **dims)
        fw = norm_final_w if l == depth - 1 else None
        if l % 2 == 0:
            h = _ffn(h, norm_ffn_w[l], ffn_w_gate[l // 2], ffn_w_up[l // 2], ffn_w_down[l // 2], fw)
        else:
            h = _moe(h, norm_ffn_w[l], router_w[l // 2], moe_w_gate[l // 2], moe_w_up[l // 2],
                     moe_w_down[l // 2], fw)
    y = h.reshape(batch, seq, d)
    new_k = kv[0].reshape(depth, batch, seq, H_SB, hd)
    new_v = kv[1].reshape(depth, batch, seq, H_SB, hd)
    return y, new_k, new_v, jnp.stack(finals)


def kernel(x_prompt, x_sample, cache_sb_k, cache_sb_v, state_ret, norm_mix_w, w_in, w_sb_o, w_ret_o,
           gn_ret_w, w_out, norm_ffn_w, ffn_w_gate, ffn_w_up, ffn_w_down, router_w, moe_w_gate,
           moe_w_up, moe_w_down, norm_final_w):
    fe = moe_w_gate.shape[-1]
    fp = -(-fe // LANES) * LANES
    weights = (
        norm_mix_w, w_in.astype(BF16), w_sb_o.astype(BF16), w_ret_o.astype(BF16), gn_ret_w,
        w_out.astype(BF16), norm_ffn_w, ffn_w_gate.astype(BF16), ffn_w_up.astype(BF16),
        ffn_w_down.astype(BF16), router_w,
        _pad_axis(moe_w_gate.astype(BF16), 3, fp), _pad_axis(moe_w_up.astype(BF16), 3, fp),
        _pad_axis(moe_w_down.astype(BF16), 2, fp), norm_final_w,
    )
    past = cache_sb_k.shape[2]
    y_p, k_p, v_p, r_p = _run_group(x_prompt, 0.0, None, None, weights)
    y_s, k_s, v_s, r_s = _run_group(x_sample, float(past), (cache_sb_k, cache_sb_v), state_ret, weights)
    return (y_p, y_s, k_p, v_p, r_p, k_s, v_s, r_s)
```

```python
import functools

import jax
import jax.numpy as jnp
from jax import lax
from jax.experimental import pallas as pl
from jax.experimental.pallas import tpu as pltpu

F32 = jnp.float32
BF16 = jnp.bfloat16

H_SB = 4
H_RET = 4
TOP_K = 2
ROPE_BASE = 10000.0
EPS = 1e-6
GN_EPS = 1e-5
RET_CHUNK_MAX = 256

LANES = 128
V7X_VMEM_BYTES = 64 * 1024 * 1024
VMEM_CAP_BYTES = V7X_VMEM_BYTES - 8 * 1024 * 1024

SB_EXIT_MASS = 104.0
SB_BLOCK = 256


def _cparams(sem, est_bytes):
    limit = int(min(VMEM_CAP_BYTES, max(32 * 1024 * 1024, 2 * est_bytes)))
    return pltpu.CompilerParams(dimension_semantics=sem, vmem_limit_bytes=limit)


def _row_tile(n, want):
    t = min(n, want)
    assert n % t == 0, (n, t)
    return t


def _rms_scale(x, w):
    ms = jnp.mean(x * x, axis=-1, keepdims=True)
    return x * lax.rsqrt(ms + EPS) * w


def _inproj_kernel(x_ref, nw_ref, w_ref, *rest, q_scale, n_alias, tm, tn):
    proj_ref, k_ref, v_ref = rest[n_alias:]
    hn = _rms_scale(x_ref[...], nw_ref[...]).astype(BF16)

    def store_heads(dst_ref, acc):
        for hh in range(H_SB):
            dst_ref[pl.ds(hh, tm, stride=H_SB), :] = acc[:, hh * LANES:(hh + 1) * LANES]

    for j in range(w_ref.shape[1] // tn):
        cols = slice(j * tn, (j + 1) * tn)
        acc = jnp.dot(hn, w_ref[:, cols], preferred_element_type=F32)
        if j == 1:
            store_heads(k_ref, acc)
        if j == 2:
            store_heads(v_ref, acc)
        proj_ref[:, cols] = (acc * q_scale if j == 0 else acc).astype(BF16)


def _in_projection(x, norm_w, w_bf, layer, depth, kv_prev, *, w_sb):
    n, d = x.shape
    d_in = w_bf.shape[1]
    tn = w_sb
    assert d_in % tn == 0
    tm = _row_tile(n, 512)
    hd = w_sb // H_SB
    assert hd == LANES
    nm = n // tm
    kv_shape = jax.ShapeDtypeStruct((depth * n * H_SB, hd), F32)
    kv_spec = pl.BlockSpec((tm * H_SB, hd), lambda i: (layer * nm + i, 0))
    in_specs = [
        pl.BlockSpec((tm, d), lambda i: (i, 0)),
        pl.BlockSpec((1, d), lambda i: (0, 0)),
        pl.BlockSpec((d, d_in), lambda i: (0, 0), pipeline_mode=pl.Buffered(1)),
    ]
    args = [x, norm_w.reshape(1, d), w_bf]
    aliases = {}
    n_alias = 0
    if kv_prev is not None:
        in_specs += [pl.BlockSpec(memory_space=pl.ANY), pl.BlockSpec(memory_space=pl.ANY)]
        args += [kv_prev[0], kv_prev[1]]
        aliases = {3: 1, 4: 2}
        n_alias = 2
    est = 2 * tm * d * 4 + tm * d * 2 + d * d_in * 2 + 2 * tm * d_in * 2 + 4 * tm * w_sb * 4 + 4 * tm * tn * 4
    proj, k_all, v_all = pl.pallas_call(
        functools.partial(_inproj_kernel, q_scale=hd ** -0.5, n_alias=n_alias, tm=tm, tn=tn),
        grid=(nm,),
        in_specs=in_specs,
        out_specs=[pl.BlockSpec((tm, d_in), lambda i: (i, 0)), kv_spec, kv_spec],
        out_shape=[jax.ShapeDtypeStruct((n, d_in), BF16), kv_shape, kv_shape],
        input_output_aliases=aliases,
        compiler_params=_cparams(("parallel",), est),
        name=f"in_projection_l{layer}",
    )(*args)
    return proj, k_all, v_all


def _cumsum_matrix():
    j = jnp.arange(LANES)[:, None]
    s = jnp.arange(LANES)[None, :]
    tri = (j >= s).astype(BF16)
    m = jnp.concatenate([tri, jnp.ones((LANES, LANES), BF16)], axis=1)
    return jnp.concatenate([m, m], axis=0)


def _sb_step_phases(q, chunks, r, cum):
    zs = [lax.dot_general(q, k, (((1,), (1,)), ((), ())), preferred_element_type=F32)
          for k, _, _, _ in chunks]
    yield
    groups = []
    for (k, _, masks, live), z in zip(chunks, zs):
        for g in range(k.shape[0] // LANES):
            zg = z[:, g * LANES:(g + 1) * LANES]
            sp = jnp.maximum(zg, 0.0) + jnp.log(1.0 + jnp.exp(-jnp.abs(zg)))
            m = None if masks is None else masks[g]
            if m is not None:
                sp = jnp.where(m, sp, 0.0)
            if live is not None:
                sp = jnp.where(live, sp, 0.0)
            hi = sp.astype(BF16)
            lo = (sp - hi.astype(F32)).astype(BF16)
            groups.append([zg, jnp.concatenate([hi, lo], axis=1), m, live])
    yield
    for grp in groups:
        grp[1] = jnp.dot(grp[1], cum, preferred_element_type=F32)
    yield
    weights = [None] * len(groups)
    for idx in reversed(range(len(groups))):
        zg, ct, m, live = groups[idx]
        a = jnp.exp(zg - ct[:, :LANES] - r)
        if m is not None:
            a = jnp.where(m, a, 0.0)
        if live is not None:
            a = jnp.where(live, a, 0.0)
        weights[idx] = a.astype(BF16)
        r = r + ct[:, LANES:]
    yield
    pv = None
    idx = 0
    for _, v, _, _ in chunks:
        n_g = v.shape[0] // LANES
        a = weights[idx] if n_g == 1 else jnp.concatenate(weights[idx:idx + n_g], axis=1)
        t = jnp.dot(a, v, preferred_element_type=F32)
        pv = t if pv is None else pv + t
        idx += n_g
    return pv, r


def _lockstep(gens):
    results = [None] * len(gens)
    running = list(range(len(gens)))
    while running:
        still = []
        for t in running:
            try:
                next(gens[t])
                still.append(t)
            except StopIteration as stop:
                results[t] = stop.value
        running = still
    return results


def _sb_step(q, chunks, r, cum):
    return _lockstep([_sb_step_phases(q, chunks, r, cum)])[0]


def _sb_prompt_kernel(q_ref, k_ref, v_ref, cum_ref, o_ref, r_ref, acc_ref, *, qb, heads):
    i = pl.program_id(2)
    cum = cum_ref[...]
    row = lax.broadcasted_iota(jnp.int32, (qb, LANES), 0)
    col = lax.broadcasted_iota(jnp.int32, (qb, LANES), 1)
    causal = [(col + g * LANES) < row for g in range(qb // LANES)]
    has_prev = i > 0
    prev0 = pl.multiple_of(jnp.maximum(i - 1, 0) * qb, qb)
    diag0 = pl.multiple_of(i * qb, qb)
    zero = jnp.zeros((qb, LANES), F32)

    steps = []
    for hh in range(heads):
        cs = slice(hh * LANES, (hh + 1) * LANES)
        chunks = [
            (k_ref[pl.ds(prev0, qb), cs], v_ref[pl.ds(prev0, qb), cs], None, has_prev),
            (k_ref[pl.ds(diag0, qb), cs], v_ref[pl.ds(diag0, qb), cs], causal, None),
        ]
        steps.append(_sb_step_phases(q_ref[:, cs], chunks, zero, cum))
    for hh, (pv, r) in enumerate(_lockstep(steps)):
        acc_ref[hh] = pv
        r_ref[hh] = r

    for hh in range(heads):
        cs = slice(hh * LANES, (hh + 1) * LANES)

        def cond(j, hh=hh):
            return jnp.logical_and(j >= 0, jnp.min(r_ref[hh]) < SB_EXIT_MASS)

        def body(j, hh=hh, cs=cs):
            start = pl.multiple_of(j * qb, qb)
            chunks = [(k_ref[pl.ds(start, qb), cs], v_ref[pl.ds(start, qb), cs], None, None)]
            pv, r = _sb_step(q_ref[:, cs], chunks, r_ref[hh], cum)
            acc_ref[hh] += pv
            r_ref[hh] = r
            return j - 1

        lax.while_loop(cond, body, i - 2)

    for hh in range(heads):
        o_ref[:, hh * LANES:(hh + 1) * LANES] = acc_ref[hh].astype(o_ref.dtype)


def _sb_attention_prompt(proj, batch, seq, *, w_sb):
    n = proj.shape[0]
    hd = w_sb // H_SB
    assert hd == LANES
    qb = SB_BLOCK
    assert seq % qb == 0
    nq = seq // qb
    heads = H_SB
    wb = heads * hd
    kcol = w_sb // wb
    vcol = 2 * w_sb // wb
    est = 4 * qb * wb * 2 + 2 * seq * wb * 2 + 4 * heads * qb * LANES * 4 + 40 * qb * LANES * 4 * heads
    resident = lambda c0: pl.BlockSpec((seq, wb), lambda b, h, i: (b, c0 + h), pipeline_mode=pl.Buffered(1))
    return pl.pallas_call(
        functools.partial(_sb_prompt_kernel, qb=qb, heads=heads),
        grid=(batch, H_SB // heads, nq),
        in_specs=[
            pl.BlockSpec((qb, wb), lambda b, h, i: (b * nq + i, h)),
            resident(kcol),
            resident(vcol),
            pl.BlockSpec((2 * LANES, 2 * LANES), lambda b, h, i: (0, 0)),
        ],
        out_specs=pl.BlockSpec((qb, wb), lambda b, h, i: (b * nq + i, h)),
        out_shape=jax.ShapeDtypeStruct((n, w_sb), BF16),
        scratch_shapes=[pltpu.VMEM((heads, qb, LANES), F32), pltpu.VMEM((heads, qb, LANES), F32)],
        compiler_params=_cparams(("parallel", "parallel", "arbitrary"), est),
        name="sb_attention_prompt",
    )(proj, proj, proj, _cumsum_matrix())


def _sb_sample_kernel(q_ref, kn_ref, vn_ref, kc_hbm, vc_hbm, cum_ref, o_ref, r_ref, acc_ref, buf_ref, sem_ref,
                      *, tq, past, cb, stream0, n_streams):
    b = pl.program_id(0)
    cum = cum_ref[...]
    row = lax.broadcasted_iota(jnp.int32, (tq, LANES), 0)
    col = lax.broadcasted_iota(jnp.int32, (tq, LANES), 1)
    zero = jnp.zeros((tq, LANES), F32)
    rows = cb * H_SB
    latest = past - cb

    def chunk_copy(stream, start, slot, which):
        src = (kc_hbm, vc_hbm)[which].at[stream0 + stream, pl.ds(start * H_SB, rows), :]
        return pltpu.make_async_copy(src, buf_ref.at[slot, which], sem_ref.at[slot, which])

    def cached(slot, which, hh):
        return buf_ref[slot, which, pl.ds(hh, cb, stride=H_SB), :].astype(BF16)

    slot = lax.rem(b, 2)

    @pl.when(b == 0)
    def _():
        for which in range(2):
            chunk_copy(0, latest, 0, which).start()

    for which in range(2):
        chunk_copy(b, latest, slot, which).wait()

    @pl.when(b + 1 < n_streams)
    def _():
        for which in range(2):
            chunk_copy(b + 1, latest, 1 - slot, which).start()

    steps = []
    for hh in range(H_SB):
        cs = slice(hh * LANES, (hh + 1) * LANES)
        chunks = [
            (cached(slot, 0, hh), cached(slot, 1, hh), None, None),
            (kn_ref[:, cs], vn_ref[:, cs], [col < row], None),
        ]
        steps.append(_sb_step_phases(q_ref[:, cs], chunks, zero, cum))
    for hh, (pv, r) in enumerate(_lockstep(steps)):
        acc_ref[hh] = pv
        r_ref[hh] = r

    def cond(j):
        return jnp.logical_and(j >= 0, jnp.min(r_ref[...]) < SB_EXIT_MASS)

    def body(j):
        for which in range(2):
            chunk_copy(b, j * cb, 2, which).start()
        for which in range(2):
            chunk_copy(b, j * cb, 2, which).wait()
        steps = []
        for hh in range(H_SB):
            cs = slice(hh * LANES, (hh + 1) * LANES)
            chunks = [(cached(2, 0, hh), cached(2, 1, hh), None, None)]
            steps.append(_sb_step_phases(q_ref[:, cs], chunks, r_ref[hh], cum))
        for hh, (pv, r) in enumerate(_lockstep(steps)):
            acc_ref[hh] += pv
            r_ref[hh] = r
        return j - 1

    lax.while_loop(cond, body, past // cb - 2)

    for hh in range(H_SB):
        o_ref[:, hh * LANES:(hh + 1) * LANES] = acc_ref[hh].astype(o_ref.dtype)


def _sb_attention_sample(proj, k_cache, v_cache, layer, batch, tq, *, w_sb):
    n = proj.shape[0]
    hd = w_sb // H_SB
    past = k_cache.shape[1] // H_SB
    cb = SB_BLOCK if past % SB_BLOCK == 0 else LANES
    assert hd == LANES and past % cb == 0 and tq <= LANES
    kv_new = proj[:, w_sb:3 * w_sb].reshape(batch, tq, 2 * w_sb)
    kv_new = jnp.pad(kv_new, ((0, 0), (0, LANES - tq), (0, 0)))
    est = 6 * cb * w_sb * 4 + 8 * LANES * w_sb * 2 + 40 * H_SB * max(tq, 8) * LANES * 4
    return pl.pallas_call(
        functools.partial(_sb_sample_kernel, tq=tq, past=past, cb=cb, stream0=layer * batch, n_streams=batch),
        grid=(batch,),
        in_specs=[
            pl.BlockSpec((tq, w_sb), lambda b: (b, 0)),
            pl.BlockSpec((None, LANES, w_sb), lambda b: (b, 0, 0)),
            pl.BlockSpec((None, LANES, w_sb), lambda b: (b, 0, 1)),
            pl.BlockSpec(memory_space=pl.ANY),
            pl.BlockSpec(memory_space=pl.ANY),
            pl.BlockSpec((2 * LANES, 2 * LANES), lambda b: (0, 0)),
        ],
        out_specs=pl.BlockSpec((tq, w_sb), lambda b: (b, 0)),
        out_shape=jax.ShapeDtypeStruct((n, w_sb), BF16),
        scratch_shapes=[pltpu.VMEM((H_SB, tq, LANES), F32), pltpu.VMEM((H_SB, tq, LANES), F32),
                        pltpu.VMEM((3, 2, cb * H_SB, hd), F32), pltpu.SemaphoreType.DMA((3, 2))],
        compiler_params=_cparams(("arbitrary",), est),
        name="sb_attention_sample",
    )(proj, kv_new, kv_new, k_cache, v_cache, _cumsum_matrix())


def _retention_kernel(dec_ref, q_ref, k_ref, va_ref, vb_ref, ga_ref, gb_ref, cos_ref, sin_ref, gnw_ref,
                      s0_ref, o_ref, sout_ref, s_ref, d_ref, qd_ref, kd_ref, *, blk, k_scale, dk, dv):
    c = pl.program_id(1)
    per_half = H_RET // 2

    @pl.when(c == 0)
    def _():
        s_ref[...] = s0_ref[...]
        n = lax.broadcasted_iota(jnp.int32, (blk, blk), 0)
        m = lax.broadcasted_iota(jnp.int32, (blk, blk), 1)
        causal = n >= m
        diff = jnp.where(causal, n - m, 0).astype(F32)
        rows = lax.broadcasted_iota(jnp.int32, (blk, LANES), 0).astype(F32)
        for hh in range(H_RET):
            lg = dec_ref[0, hh]
            d_ref[hh] = jnp.where(causal, jnp.exp(diff * lg), 0.0)
            qd_ref[hh] = jnp.exp((rows + 1.0) * lg)
            kd_ref[hh] = jnp.exp((blk - 1.0 - rows) * lg)

    cos = cos_ref[...]
    sin = sin_ref[...]
    half = dk // 2
    for hh in range(H_RET):
        ks = slice(hh * dk, (hh + 1) * dk)
        vs = slice(hh * dv, (hh + 1) * dv)
        qf = q_ref[:, ks].astype(F32)
        kf = k_ref[:, ks].astype(F32)
        qr = qf * cos + pltpu.roll(qf, half, 1) * sin
        kr = (kf * cos + pltpu.roll(kf, half, 1) * sin) * k_scale
        hs = slice((hh % per_half) * dv, (hh % per_half + 1) * dv)
        v = (va_ref if hh < per_half else vb_ref)[:, hs]

        scores = lax.dot_general(qr.astype(BF16), kr.astype(BF16), (((1,), (1,)), ((), ())),
                                 preferred_element_type=F32) * d_ref[hh]
        inner = jnp.dot(scores.astype(BF16), v, preferred_element_type=F32)
        s = s_ref[hh]
        cross = jnp.dot((qr * qd_ref[hh]).astype(BF16), s.astype(BF16), preferred_element_type=F32)
        o = inner + cross
        kv = lax.dot_general((kr * kd_ref[hh]).astype(BF16), v, (((0,), (0,)), ((), ())),
                             preferred_element_type=F32)
        s_new = dec_ref[1, hh] * s + kv
        s_ref[hh] = s_new
        sout_ref[hh] = s_new

        mu = jnp.mean(o, axis=-1, keepdims=True)
        dev = o - mu
        var = jnp.mean(dev * dev, axis=-1, keepdims=True)
        y = dev * lax.rsqrt(var + GN_EPS) * gnw_ref[:, vs]
        gf = (ga_ref if hh < per_half else gb_ref)[:, hs].astype(F32)
        o_ref[:, vs] = (gf * jax.nn.sigmoid(gf) * y).astype(o_ref.dtype)


def _retention(proj, gn_w, state0, batch, seq, pos0, *, w_sb, wk_ret, wv_ret):
    n = proj.shape[0]
    dk = wk_ret // H_RET
    dv = wv_ret // H_RET
    assert dk == LANES and dv % LANES == 0
    blk = min(seq, RET_CHUNK_MAX)
    assert seq % blk == 0
    nc = seq // blk
    hw = wv_ret // 2
    assert (3 * w_sb) % wk_ret == 0 and (3 * w_sb + 2 * wk_ret) % hw == 0 and H_RET % 2 == 0
    q0 = 3 * w_sb // wk_ret
    k0 = q0 + 1
    v0 = (3 * w_sb + 2 * wk_ret) // hw
    g0 = v0 + 2

    half = dk // 2
    inv = ROPE_BASE ** (-jnp.arange(half, dtype=F32) / half)
    ang = (pos0 + jnp.arange(seq, dtype=F32))[:, None] * inv[None, :]
    cos = jnp.concatenate([jnp.cos(ang), jnp.cos(ang)], axis=1)
    sin = jnp.concatenate([-jnp.sin(ang), jnp.sin(ang)], axis=1)
    log_gamma = jnp.log1p(-jnp.exp2(-5.0 - jnp.arange(H_RET, dtype=F32)))
    dec = jnp.stack([log_gamma, jnp.exp(blk * log_gamma)])

    est = 2 * blk * (2 * wk_ret + 2 * wv_ret) * 2 + 4 * blk * dk * 4 + 5 * H_RET * dk * dv * 4 \
        + 2 * blk * wv_ret * 2 + H_RET * (blk * blk + 2 * blk * LANES) * 4 + 12 * H_RET * blk * dv * 4
    row = lambda b, c: b * nc + c
    return pl.pallas_call(
        functools.partial(_retention_kernel, blk=blk, k_scale=dk ** -0.5, dk=dk, dv=dv),
        grid=(batch, nc),
        in_specs=[
            pl.BlockSpec(memory_space=pltpu.SMEM),
            pl.BlockSpec((blk, wk_ret), lambda b, c: (row(b, c), q0)),
            pl.BlockSpec((blk, wk_ret), lambda b, c: (row(b, c), k0)),
            pl.BlockSpec((blk, hw), lambda b, c: (row(b, c), v0)),
            pl.BlockSpec((blk, hw), lambda b, c: (row(b, c), v0 + 1)),
            pl.BlockSpec((blk, hw), lambda b, c: (row(b, c), g0)),
            pl.BlockSpec((blk, hw), lambda b, c: (row(b, c), g0 + 1)),
            pl.BlockSpec((blk, dk), lambda b, c: (c, 0)),
            pl.BlockSpec((blk, dk), lambda b, c: (c, 0)),
            pl.BlockSpec((1, wv_ret), lambda b, c: (0, 0)),
            pl.BlockSpec((None, H_RET, dk, dv), lambda b, c: (b, 0, 0, 0)),
        ],
        out_specs=[
            pl.BlockSpec((blk, wv_ret), lambda b, c: (row(b, c), 0)),
            pl.BlockSpec((None, H_RET, dk, dv), lambda b, c: (b, 0, 0, 0)),
        ],
        out_shape=[jax.ShapeDtypeStruct((n, wv_ret), BF16),
                   jax.ShapeDtypeStruct((batch, H_RET, dk, dv), F32)],
        scratch_shapes=[pltpu.VMEM((H_RET, dk, dv), F32), pltpu.VMEM((H_RET, blk, blk), F32),
                        pltpu.VMEM((H_RET, blk, LANES), F32), pltpu.VMEM((H_RET, blk, LANES), F32)],
        compiler_params=_cparams(("parallel", "arbitrary"), est),
        name="retention",
    )(dec, proj, proj, proj, proj, proj, proj, cos, sin, gn_w.reshape(1, wv_ret), state0)


def _merge_kernel(osb_ref, or_ref, asb0_ref, asb1_ref, ar0_ref, ar1_ref, h_ref,
                  wsb_ref, wr_ref, wo_ref, out_ref):
    b_sb = jnp.dot(osb_ref[...], wsb_ref[...], preferred_element_type=F32)
    b_r = jnp.dot(or_ref[...], wr_ref[...], preferred_element_type=F32)
    a_sb = jnp.concatenate([asb0_ref[...], asb1_ref[...]], axis=1).astype(F32)
    a_r = jnp.concatenate([ar0_ref[...], ar1_ref[...]], axis=1).astype(F32)
    merged = jax.nn.sigmoid(a_sb) * b_sb + jax.nn.sigmoid(a_r) * b_r
    out_ref[...] = h_ref[...] + jnp.dot(merged.astype(BF16), wo_ref[...], preferred_element_type=F32)


def _merge(o_sb, o_r, proj, h, w_sb_o, w_ret_o, w_out, *, w_sb, wk_ret, wv_ret):
    n, d = h.shape
    tm = _row_tile(n, 512)
    half = d // 2
    a0 = (3 * w_sb + 2 * wk_ret + 2 * wv_ret) // half
    assert (3 * w_sb + 2 * wk_ret + 2 * wv_ret) % half == 0
    gate = lambda t: pl.BlockSpec((tm, half), lambda i: (i, a0 + t))
    full = lambda a: pl.BlockSpec(a.shape, lambda i: (0, 0))
    est = 2 * tm * (w_sb + wv_ret + 2 * d) * 2 + 4 * tm * d * 4 + 2 * (w_sb + wv_ret + d) * d * 2 + 6 * tm * d * 4
    return pl.pallas_call(
        _merge_kernel,
        grid=(n // tm,),
        in_specs=[
            pl.BlockSpec((tm, w_sb), lambda i: (i, 0)),
            pl.BlockSpec((tm, wv_ret), lambda i: (i, 0)),
            gate(0), gate(1), gate(2), gate(3),
            pl.BlockSpec((tm, d), lambda i: (i, 0)),
            full(w_sb_o), full(w_ret_o), full(w_out),
        ],
        out_specs=pl.BlockSpec((tm, d), lambda i: (i, 0)),
        out_shape=jax.ShapeDtypeStruct((n, d), F32),
        compiler_params=_cparams(("parallel",), est),
        name="merge",
    )(o_sb, o_r, proj, proj, proj, proj, h, w_sb_o, w_ret_o, w_out)


def _finish(h, acc, fw_ref, out_ref):
    y = h + acc
    if fw_ref is not None:
        y = _rms_scale(y, fw_ref[...])
    out_ref[...] = y


def _ffn_kernel(h_ref, nw_ref, wg_ref, wu_ref, wd_ref, *rest, final, tf):
    fw_ref = rest[0] if final else None
    out_ref = rest[-1]
    h = h_ref[...]
    hn = _rms_scale(h, nw_ref[...]).astype(BF16)
    acc = None
    for f in range(wg_ref.shape[1] // tf):
        cols = slice(f * tf, (f + 1) * tf)
        g = jnp.dot(hn, wg_ref[:, cols], preferred_element_type=F32)
        u = jnp.dot(hn, wu_ref[:, cols], preferred_element_type=F32)
        act = (g * jax.nn.sigmoid(g) * u).astype(BF16)
        t = jnp.dot(act, wd_ref[cols, :], preferred_element_type=F32)
        acc = t if acc is None else acc + t
    _finish(h, acc, fw_ref, out_ref)


def _ffn(h, norm_w, wg, wu, wd, final_w):
    n, d = h.shape
    ff = wg.shape[1]
    tm = _row_tile(n, 512)
    tf = 2 * LANES
    assert ff % tf == 0
    final = final_w is not None
    resident = lambda a: pl.BlockSpec(a.shape, lambda i: (0, 0), pipeline_mode=pl.Buffered(1))
    in_specs = [
        pl.BlockSpec((tm, d), lambda i: (i, 0)),
        pl.BlockSpec((1, d), lambda i: (0, 0)),
        resident(wg), resident(wu), resident(wd),
    ]
    args = [h, norm_w.reshape(1, d), wg, wu, wd]
    if final:
        in_specs.append(pl.BlockSpec((1, d), lambda i: (0, 0)))
        args.append(final_w.reshape(1, d))
    est = 4 * tm * d * 4 + tm * d * 2 + 3 * d * ff * 2 + 4 * tm * tf * 4 + 2 * tm * d * 4
    return pl.pallas_call(
        functools.partial(_ffn_kernel, final=final, tf=tf),
        grid=(n // tm,),
        in_specs=in_specs,
        out_specs=pl.BlockSpec((tm, d), lambda i: (i, 0)),
        out_shape=jax.ShapeDtypeStruct((n, d), F32),
        compiler_params=_cparams(("parallel",), est),
        name="ffn_dense",
    )(*args)


MOE_TILE = 512


def _interleave_store(dst_ref, lead, x, n_rows):
    groups = x.shape[1] // LANES
    for g in range(groups):
        dst_ref[lead + (pl.ds(g, n_rows, stride=groups), slice(None))] = x[:, g * LANES:(g + 1) * LANES]


def _interleave_load(src_ref, lead, n_rows, groups):
    parts = [src_ref[lead + (pl.ds(g, n_rows, stride=groups), slice(None))] for g in range(groups)]
    return jnp.concatenate(parts, axis=1)


def _moe_route_kernel(h_ref, nw_ref, rwh_ref, rwl_ref, tri_ref, xs_hbm, pos_ref, w_ref, cnt_ref,
                      x3_ref, run_ref, posv_ref, poss_ref, zero_ref, sem_ref, psem_ref, zsem_ref,
                      *, n_exp, cap, pad_to):
    i = pl.program_id(0)
    nt = pl.num_programs(0)
    tm, d = h_ref.shape
    groups = d // LANES
    slot = lax.rem(i, 2)

    def slot_copies(s):
        return pltpu.make_async_copy(x3_ref.at[s], xs_hbm.at[pl.ds(0, tm * groups), :], sem_ref.at[s])

    def row_copy(s, r, p):
        return pltpu.make_async_copy(x3_ref.at[s, pl.ds(r * groups, groups), :],
                                     xs_hbm.at[pl.ds(p * groups, groups), :], sem_ref.at[s])

    @pl.when(i == 0)
    def _():
        run_ref[...] = jnp.zeros_like(run_ref)
        zero_ref[...] = jnp.zeros_like(zero_ref)

    @pl.when(i >= 2)
    def _():
        slot_copies(slot).wait()
        slot_copies(slot).wait()

    hn = _rms_scale(h_ref[...], nw_ref[...])
    _interleave_store(x3_ref, (slot,), hn, tm)

    hn_hi = hn.astype(BF16)
    hn_lo = (hn - hn_hi.astype(F32)).astype(BF16)
    logits = (jnp.dot(hn_hi, rwh_ref[...], preferred_element_type=F32)
              + jnp.dot(hn_lo, rwh_ref[...], preferred_element_type=F32)
              + jnp.dot(hn_hi, rwl_ref[...], preferred_element_type=F32))
    col = lax.broadcasted_iota(jnp.int32, logits.shape, 1).astype(F32)
    logits = jnp.where(col < n_exp, logits, -jnp.inf)
    m1 = jnp.max(logits, axis=-1, keepdims=True)
    i1 = jnp.min(jnp.where(logits == m1, col, float(LANES)), axis=-1, keepdims=True)
    rest_l = jnp.where(col == i1, -jnp.inf, logits)
    m2 = jnp.max(rest_l, axis=-1, keepdims=True)
    i2 = jnp.min(jnp.where(rest_l == m2, col, float(LANES)), axis=-1, keepdims=True)
    ex = jnp.exp(m2 - m1)
    w_ref[...] = jnp.concatenate([jnp.broadcast_to(1.0 / (1.0 + ex), (tm, LANES)),
                                  jnp.broadcast_to(ex / (1.0 + ex), (tm, LANES))], axis=1)

    onehot = jnp.where(jnp.logical_or(col == i1, col == i2), 1.0, 0.0)
    ranks = jnp.dot(tri_ref[...], onehot.astype(BF16), preferred_element_type=F32) + run_ref[...]
    pos1 = i1 * float(cap) + jnp.sum(jnp.where(col == i1, ranks, 0.0), axis=-1, keepdims=True)
    pos2 = i2 * float(cap) + jnp.sum(jnp.where(col == i2, ranks, 0.0), axis=-1, keepdims=True)
    run = run_ref[...] + jnp.sum(onehot, axis=0, keepdims=True)
    run_ref[...] = run

    meta = jnp.where(col == 0.0, pos1, jnp.where(col == 1.0, pos2, 0.0))
    posv = jnp.transpose(meta)[0:TOP_K, :].astype(jnp.int32)
    pos_ref[...] = posv
    posv_ref[...] = posv
    to_smem = pltpu.make_async_copy(posv_ref, poss_ref, psem_ref.at[0])
    to_smem.start()
    to_smem.wait()

    def issue(r, carry):
        row_copy(slot, r, poss_ref[0, r]).start()
        row_copy(slot, r, poss_ref[1, r]).start()
        return carry

    lax.fori_loop(0, tm, issue, 0, unroll=8)

    @pl.when(i == nt - 1)
    def _():
        slot_copies(slot).wait()
        slot_copies(slot).wait()

        @pl.when(nt > 1)
        def _():
            slot_copies(1 - slot).wait()
            slot_copies(1 - slot).wait()

        cnt_ref[...] = run
        lane = lax.broadcasted_iota(jnp.int32, run.shape, 1)
        fills = []
        for x in range(n_exp):
            c = jnp.sum(jnp.where(lane == x, run, 0.0)).astype(jnp.int32)
            gap = lax.div(c + (pad_to - 1), pad_to) * pad_to - c
            start = x * cap + c
            for bit in reversed(range(pad_to.bit_length() - 1)):
                size = 1 << bit
                blk = pltpu.make_async_copy(zero_ref.at[pl.ds(0, size * groups), :],
                                            xs_hbm.at[pl.ds(start * groups, size * groups), :], zsem_ref.at[0])
                has = jnp.bitwise_and(lax.shift_right_logical(gap, bit), 1)
                fills.append((has, blk))
                start = start + has * size
        for has, blk in fills:
            pl.when(has == 1)(blk.start)
        for has, blk in fills:
            pl.when(has == 1)(blk.wait)


def _moe_expert_kernel(te_ref, tb_ref, ok_ref, xs_ref, wg_ref, wu_ref, wd_ref, ys_ref, *, rows, groups):
    t = pl.program_id(0)

    @pl.when(ok_ref[t] == 1)
    def _():
        x = _interleave_load(xs_ref, (), rows, groups).astype(BF16)
        g = jnp.dot(x, wg_ref[...], preferred_element_type=F32)
        u = jnp.dot(x, wu_ref[...], preferred_element_type=F32)
        act = (g * jax.nn.sigmoid(g) * u).astype(BF16)
        _interleave_store(ys_ref, (), jnp.dot(act, wd_ref[...], preferred_element_type=F32), rows)


def _moe_combine_kernel(pos_ref, nxt_ref, h_ref, w_ref, *rest, final):
    fw_ref = rest[0] if final else None
    ys_hbm, out_ref, g_ref, sem_ref = rest[1:] if final else rest
    i = pl.program_id(0)
    nt = pl.num_programs(0)
    tm, d = h_ref.shape
    groups = d // LANES
    slot = lax.rem(i, 2)

    def gather_tile(p_ref, s):
        def issue(r, carry):
            for k in range(TOP_K):
                pltpu.make_async_copy(ys_hbm.at[pl.ds(p_ref[k, r] * groups, groups), :],
                                      g_ref.at[s, k, pl.ds(r * groups, groups), :], sem_ref.at[s, k]).start()
            return carry

        lax.fori_loop(0, tm, issue, 0, unroll=8)

    @pl.when(i == 0)
    def _():
        gather_tile(pos_ref, 0)

    @pl.when(i + 1 < nt)
    def _():
        gather_tile(nxt_ref, 1 - slot)

    for k in range(TOP_K):
        pltpu.make_async_copy(ys_hbm.at[pl.ds(0, tm * groups), :], g_ref.at[slot, k], sem_ref.at[slot, k]).wait()

    w = w_ref[...]
    w1 = jnp.concatenate([w[:, :LANES]] * groups, axis=1)
    w2 = jnp.concatenate([w[:, LANES:]] * groups, axis=1)
    y = (w1 * _interleave_load(g_ref, (slot, 0), tm, groups)
         + w2 * _interleave_load(g_ref, (slot, 1), tm, groups))
    _finish(h_ref[...], y, fw_ref, out_ref)


def _moe(h, norm_w, router_w, wg, wu, wd, final_w):
    n, d = h.shape
    n_exp, _, fp = wg.shape
    groups = d // LANES
    assert d % LANES == 0 and groups == 8, "a row must fill one f32 (8,128) tile to move as one contiguous copy"
    tm = _row_tile(n, 1024)
    nt = n // tm
    tmx = min(MOE_TILE, n)
    assert tmx & (tmx - 1) == 0, "the zero fill writes power-of-two blocks below one tile"
    cap = -(-n // tmx) * tmx
    blocks_per_expert = cap // tmx
    dummy_block = n_exp * blocks_per_expert
    sorted_rows = (n_exp * cap + tmx) * groups
    final = final_w is not None
    rw = jnp.pad(router_w.astype(F32), ((0, 0), (0, LANES - n_exp)))
    rw_hi = rw.astype(BF16)
    rw_lo = (rw - rw_hi.astype(F32)).astype(BF16)
    tri = (jnp.arange(tm)[None, :] < jnp.arange(tm)[:, None]).astype(BF16)

    est = 4 * tm * d * 4 + 4 * d * LANES * 2 + 2 * tm * tm * 2 + 2 * tm * d * 4 + 8 * tm * LANES * 4 + 6 * tm * d * 4
    xs, pos, wts, cnt = pl.pallas_call(
        functools.partial(_moe_route_kernel, n_exp=n_exp, cap=cap, pad_to=tmx),
        grid=(nt,),
        in_specs=[
            pl.BlockSpec((tm, d), lambda i: (i, 0)),
            pl.BlockSpec((1, d), lambda i: (0, 0)),
            pl.BlockSpec((d, LANES), lambda i: (0, 0)),
            pl.BlockSpec((d, LANES), lambda i: (0, 0)),
            pl.BlockSpec((tm, tm), lambda i: (0, 0)),
        ],
        out_specs=[
            pl.BlockSpec(memory_space=pl.ANY),
            pl.BlockSpec((None, TOP_K, tm), lambda i: (i, 0, 0)),
            pl.BlockSpec((tm, TOP_K * LANES), lambda i: (i, 0)),
            pl.BlockSpec((1, LANES), lambda i: (0, 0)),
        ],
        out_shape=[
            jax.ShapeDtypeStruct((sorted_rows, LANES), F32),
            jax.ShapeDtypeStruct((nt, TOP_K, tm), jnp.int32),
            jax.ShapeDtypeStruct((n, TOP_K * LANES), F32),
            jax.ShapeDtypeStruct((1, LANES), F32),
        ],
        scratch_shapes=[
            pltpu.VMEM((2, tm * groups, LANES), F32), pltpu.VMEM((1, LANES), F32),
            pltpu.VMEM((TOP_K, tm), jnp.int32), pltpu.SMEM((TOP_K, tm), jnp.int32),
            pltpu.VMEM((max(tmx // 2, 1) * groups, LANES), F32),
            pltpu.SemaphoreType.DMA((2,)), pltpu.SemaphoreType.DMA((1,)), pltpu.SemaphoreType.DMA((1,)),
        ],
        compiler_params=_cparams(("arbitrary",), est),
        name="moe_route",
    )(h, norm_w.reshape(1, d), rw_hi, rw_lo, tri)

    counts = cnt[0, :n_exp].astype(jnp.int32)
    tiles = (counts + (tmx - 1)) // tmx
    first = jnp.cumsum(tiles) - tiles
    n_steps = 2 * n // tmx + n_exp
    step = jnp.arange(n_steps, dtype=jnp.int32)
    te = jnp.clip(jnp.sum((step[:, None] >= (first + tiles)[None, :]).astype(jnp.int32), axis=1), 0, n_exp - 1)
    ok = (step < jnp.sum(tiles)).astype(jnp.int32)
    tb = jnp.where(ok == 1, te * blocks_per_expert + step - first[te], dummy_block).astype(jnp.int32)

    est = 4 * tmx * d * 4 + 6 * d * fp * 2 + 4 * tmx * fp * 4 + 2 * tmx * d * 4
    sorted_spec = pl.BlockSpec((tmx * groups, LANES), lambda t, te, tb, ok: (tb[t], 0))
    ys = pl.pallas_call(
        functools.partial(_moe_expert_kernel, rows=tmx, groups=groups),
        grid_spec=pltpu.PrefetchScalarGridSpec(
            num_scalar_prefetch=3,
            grid=(n_steps,),
            in_specs=[
                sorted_spec,
                pl.BlockSpec((None, d, fp), lambda t, te, tb, ok: (te[t], 0, 0)),
                pl.BlockSpec((None, d, fp), lambda t, te, tb, ok: (te[t], 0, 0)),
                pl.BlockSpec((None, fp, d), lambda t, te, tb, ok: (te[t], 0, 0)),
            ],
            out_specs=sorted_spec,
        ),
        out_shape=jax.ShapeDtypeStruct((sorted_rows, LANES), F32),
        compiler_params=_cparams(("arbitrary",), est),
        name="moe_experts",
    )(te, tb, ok, xs, wg, wu, wd)

    in_specs = [
        pl.BlockSpec((None, TOP_K, tm), lambda i: (i, 0, 0), memory_space=pltpu.SMEM),
        pl.BlockSpec((None, TOP_K, tm), lambda i: (jnp.minimum(i + 1, nt - 1), 0, 0), memory_space=pltpu.SMEM),
        pl.BlockSpec((tm, d), lambda i: (i, 0)),
        pl.BlockSpec((tm, TOP_K * LANES), lambda i: (i, 0)),
    ]
    args = [pos, pos, h, wts]
    if final:
        in_specs.append(pl.BlockSpec((1, d), lambda i: (0, 0)))
        args.append(final_w.reshape(1, d))
    in_specs.append(pl.BlockSpec(memory_space=pl.ANY))
    args.append(ys)
    est = 4 * tm * d * 4 + 2 * tm * TOP_K * LANES * 4 + 2 * TOP_K * tm * d * 4 + 4 * tm * d * 4
    return pl.pallas_call(
        functools.partial(_moe_combine_kernel, final=final),
        grid=(nt,),
        in_specs=in_specs,
        out_specs=pl.BlockSpec((tm, d), lambda i: (i, 0)),
        out_shape=jax.ShapeDtypeStruct((n, d), F32),
        scratch_shapes=[pltpu.VMEM((2, TOP_K, tm * groups, LANES), F32), pltpu.SemaphoreType.DMA((2, TOP_K))],
        compiler_params=_cparams(("arbitrary",), est),
        name="moe_combine",
    )(*args)


def _pad_axis(a, axis, to):
    pad = [(0, 0)] * a.ndim
    pad[axis] = (0, to - a.shape[axis])
    return jnp.pad(a, pad)


def _run_group(x, pos0, caches, states, weights):
    (norm_mix_w, w_in, w_sb_o, w_ret_o, gn_ret_w, w_out, norm_ffn_w, ffn_w_gate, ffn_w_up,
     ffn_w_down, router_w, moe_w_gate, moe_w_up, moe_w_down, norm_final_w) = weights
    batch, seq, d = x.shape
    depth = w_in.shape[0]
    w_sb = w_sb_o.shape[1]
    wv_ret = w_ret_o.shape[1]
    d_in = w_in.shape[2]
    wk_ret = (d_in - 3 * w_sb - 2 * wv_ret - 2 * d) // 2
    dims = dict(w_sb=w_sb, wk_ret=wk_ret, wv_ret=wv_ret)
    dk, dv = wk_ret // H_RET, wv_ret // H_RET
    hd = w_sb // H_SB

    h = x.reshape(batch * seq, d)
    kv = None
    finals = []
    if caches is not None:
        kc = caches[0].reshape(depth * batch, -1, hd)
        vc = caches[1].reshape(depth * batch, -1, hd)
    for l in range(depth):
        proj, k_all, v_all = _in_projection(h, norm_mix_w[l], w_in[l], l, depth, kv, w_sb=w_sb)
        kv = (k_all, v_all)
        if caches is None:
            o_sb = _sb_attention_prompt(proj, batch, seq, w_sb=w_sb)
            s0 = jnp.zeros((batch, H_RET, dk, dv), F32)
        else:
            o_sb = _sb_attention_sample(proj, kc, vc, l, batch, seq, w_sb=w_sb)
            s0 = states[l].astype(F32)
        o_r, s_fin = _retention(proj, gn_ret_w[l], s0, batch, seq, pos0, **dims)
        finals.append(s_fin)
        h = _merge(o_sb, o_r, proj, h, w_sb_o[l], w_ret_o[l], w_out[l], **dims)
        fw = norm_final_w if l == depth - 1 else None
        if l % 2 == 0:
            h = _ffn(h, norm_ffn_w[l], ffn_w_gate[l // 2], ffn_w_up[l // 2], ffn_w_down[l // 2], fw)
        else:
            h = _moe(h, norm_ffn_w[l], router_w[l // 2], moe_w_gate[l // 2], moe_w_up[l // 2],
                     moe_w_down[l // 2], fw)
    y = h.reshape(batch, seq, d)
    new_k = kv[0].reshape(depth, batch, seq, H_SB, hd)
    new_v = kv[1].reshape(depth, batch, seq, H_SB, hd)
    return y, new_k, new_v, jnp.stack(finals)


def kernel(x_prompt, x_sample, cache_sb_k, cache_sb_v, state_ret, norm_mix_w, w_in, w_sb_o, w_ret_o,
           gn_ret_w, w_out, norm_ffn_w, ffn_w_gate, ffn_w_up, ffn_w_down, router_w, moe_w_gate,
           moe_w_up, moe_w_down, norm_final_w):
    fe = moe_w_gate.shape[-1]
    fp = -(-fe // LANES) * LANES
    weights = (
        norm_mix_w, w_in.astype(BF16), w_sb_o.astype(BF16), w_ret_o.astype(BF16), gn_ret_w,
        w_out.astype(BF16), norm_ffn_w, ffn_w_gate.astype(BF16), ffn_w_up.astype(BF16),
        ffn_w_down.astype(BF16), router_w,
        _pad_axis(moe_w_gate.astype(BF16), 3, fp), _pad_axis(moe_w_up.astype(BF16), 3, fp),
        _pad_axis(moe_w_down.astype(BF16), 2, fp), norm_final_w,
    )
    past = cache_sb_k.shape[2]
    y_p, k_p, v_p, r_p = _run_group(x_prompt, 0.0, None, None, weights)
    y_s, k_s, v_s, r_s = _run_group(x_sample, float(past), (cache_sb_k, cache_sb_v), state_ret, weights)
    return (y_p, y_s, k_p, v_p, r_p, k_s, v_s, r_s)
```

```python
import functools

import jax
import jax.numpy as jnp
from jax import lax
from jax.experimental import pallas as pl
from jax.experimental.pallas import tpu as pltpu

F32 = jnp.float32
BF16 = jnp.bfloat16

H_SB = 4
H_RET = 4
TOP_K = 2
ROPE_BASE = 10000.0
EPS = 1e-6
GN_EPS = 1e-5
RET_CHUNK_MAX = 256

LANES = 128
V7X_VMEM_BYTES = 64 * 1024 * 1024
VMEM_CAP_BYTES = V7X_VMEM_BYTES - 8 * 1024 * 1024

SB_EXIT_MASS = 104.0
SB_BLOCK = 256


def _cparams(sem, est_bytes):
    limit = int(min(VMEM_CAP_BYTES, max(32 * 1024 * 1024, 2 * est_bytes)))
    return pltpu.CompilerParams(dimension_semantics=sem, vmem_limit_bytes=limit)


def _row_tile(n, want):
    t = min(n, want)
    assert n % t == 0, (n, t)
    return t


def _rms_scale(x, w):
    ms = jnp.mean(x * x, axis=-1, keepdims=True)
    return x * lax.rsqrt(ms + EPS) * w


def _inproj_kernel(x_ref, nw_ref, w_ref, *rest, q_scale, n_alias, tm, tn):
    proj_ref, k_ref, v_ref = rest[n_alias:]
    hn = _rms_scale(x_ref[...], nw_ref[...]).astype(BF16)

    def store_heads(dst_ref, acc):
        for hh in range(H_SB):
            dst_ref[pl.ds(hh, tm, stride=H_SB), :] = acc[:, hh * LANES:(hh + 1) * LANES]

    for j in range(w_ref.shape[1] // tn):
        cols = slice(j * tn, (j + 1) * tn)
        acc = jnp.dot(hn, w_ref[:, cols], preferred_element_type=F32)
        if j == 1:
            store_heads(k_ref, acc)
        if j == 2:
            store_heads(v_ref, acc)
        proj_ref[:, cols] = (acc * q_scale if j == 0 else acc).astype(BF16)


def _in_projection(x, norm_w, w_bf, layer, depth, kv_prev, *, w_sb):
    n, d = x.shape
    d_in = w_bf.shape[1]
    tn = w_sb
    assert d_in % tn == 0
    tm = _row_tile(n, 512)
    hd = w_sb // H_SB
    assert hd == LANES
    nm = n // tm
    kv_shape = jax.ShapeDtypeStruct((depth * n * H_SB, hd), F32)
    kv_spec = pl.BlockSpec((tm * H_SB, hd), lambda i: (layer * nm + i, 0))
    in_specs = [
        pl.BlockSpec((tm, d), lambda i: (i, 0)),
        pl.BlockSpec((1, d), lambda i: (0, 0)),
        pl.BlockSpec((d, d_in), lambda i: (0, 0), pipeline_mode=pl.Buffered(1)),
    ]
    args = [x, norm_w.reshape(1, d), w_bf]
    aliases = {}
    n_alias = 0
    if kv_prev is not None:
        in_specs += [pl.BlockSpec(memory_space=pl.ANY), pl.BlockSpec(memory_space=pl.ANY)]
        args += [kv_prev[0], kv_prev[1]]
        aliases = {3: 1, 4: 2}
        n_alias = 2
    est = 2 * tm * d * 4 + tm * d * 2 + d * d_in * 2 + 2 * tm * d_in * 2 + 4 * tm * w_sb * 4 + 4 * tm * tn * 4
    proj, k_all, v_all = pl.pallas_call(
        functools.partial(_inproj_kernel, q_scale=hd ** -0.5, n_alias=n_alias, tm=tm, tn=tn),
        grid=(nm,),
        in_specs=in_specs,
        out_specs=[pl.BlockSpec((tm, d_in), lambda i: (i, 0)), kv_spec, kv_spec],
        out_shape=[jax.ShapeDtypeStruct((n, d_in), BF16), kv_shape, kv_shape],
        input_output_aliases=aliases,
        compiler_params=_cparams(("parallel",), est),
        name=f"in_projection_l{layer}",
    )(*args)
    return proj, k_all, v_all


def _cumsum_matrix():
    j = jnp.arange(LANES)[:, None]
    s = jnp.arange(LANES)[None, :]
    tri = (j >= s).astype(BF16)
    m = jnp.concatenate([tri, jnp.ones((LANES, LANES), BF16)], axis=1)
    return jnp.concatenate([m, m], axis=0)


def _sb_step_phases(q, chunks, r, cum):
    zs = [lax.dot_general(q, k, (((1,), (1,)), ((), ())), preferred_element_type=F32)
          for k, _, _, _ in chunks]
    yield
    groups = []
    for (k, _, masks, live), z in zip(chunks, zs):
        for g in range(k.shape[0] // LANES):
            zg = z[:, g * LANES:(g + 1) * LANES]
            sp = jnp.maximum(zg, 0.0) + jnp.log(1.0 + jnp.exp(-jnp.abs(zg)))
            m = None if masks is None else masks[g]
            if m is not None:
                sp = jnp.where(m, sp, 0.0)
            if live is not None:
                sp = jnp.where(live, sp, 0.0)
            hi = sp.astype(BF16)
            lo = (sp - hi.astype(F32)).astype(BF16)
            groups.append([zg, jnp.concatenate([hi, lo], axis=1), m, live])
    yield
    for grp in groups:
        grp[1] = jnp.dot(grp[1], cum, preferred_element_type=F32)
    yield
    weights = [None] * len(groups)
    for idx in reversed(range(len(groups))):
        zg, ct, m, live = groups[idx]
        a = jnp.exp(zg - ct[:, :LANES] - r)
        if m is not None:
            a = jnp.where(m, a, 0.0)
        if live is not None:
            a = jnp.where(live, a, 0.0)
        weights[idx] = a.astype(BF16)
        r = r + ct[:, LANES:]
    yield
    pv = None
    idx = 0
    for _, v, _, _ in chunks:
        n_g = v.shape[0] // LANES
        a = weights[idx] if n_g == 1 else jnp.concatenate(weights[idx:idx + n_g], axis=1)
        t = jnp.dot(a, v, preferred_element_type=F32)
        pv = t if pv is None else pv + t
        idx += n_g
    return pv, r


def _lockstep(gens):
    results = [None] * len(gens)
    running = list(range(len(gens)))
    while running:
        still = []
        for t in running:
            try:
                next(gens[t])
                still.append(t)
            except StopIteration as stop:
                results[t] = stop.value
        running = still
    return results


def _sb_step(q, chunks, r, cum):
    return _lockstep([_sb_step_phases(q, chunks, r, cum)])[0]


def _sb_prompt_kernel(q_ref, k_ref, v_ref, cum_ref, o_ref, r_ref, acc_ref, *, qb, heads):
    i = pl.program_id(2)
    cum = cum_ref[...]
    row = lax.broadcasted_iota(jnp.int32, (qb, LANES), 0)
    col = lax.broadcasted_iota(jnp.int32, (qb, LANES), 1)
    causal = [(col + g * LANES) < row for g in range(qb // LANES)]
    has_prev = i > 0
    prev0 = pl.multiple_of(jnp.maximum(i - 1, 0) * qb, qb)
    diag0 = pl.multiple_of(i * qb, qb)
    zero = jnp.zeros((qb, LANES), F32)

    steps = []
    for hh in range(heads):
        cs = slice(hh * LANES, (hh + 1) * LANES)
        chunks = [
            (k_ref[pl.ds(prev0, qb), cs], v_ref[pl.ds(prev0, qb), cs], None, has_prev),
            (k_ref[pl.ds(diag0, qb), cs], v_ref[pl.ds(diag0, qb), cs], causal, None),
        ]
        steps.append(_sb_step_phases(q_ref[:, cs], chunks, zero, cum))
    for hh, (pv, r) in enumerate(_lockstep(steps)):
        acc_ref[hh] = pv
        r_ref[hh] = r

    for hh in range(heads):
        cs = slice(hh * LANES, (hh + 1) * LANES)

        def cond(j, hh=hh):
            return jnp.logical_and(j >= 0, jnp.min(r_ref[hh]) < SB_EXIT_MASS)

        def body(j, hh=hh, cs=cs):
            start = pl.multiple_of(j * qb, qb)
            chunks = [(k_ref[pl.ds(start, qb), cs], v_ref[pl.ds(start, qb), cs], None, None)]
            pv, r = _sb_step(q_ref[:, cs], chunks, r_ref[hh], cum)
            acc_ref[hh] += pv
            r_ref[hh] = r
            return j - 1

        lax.while_loop(cond, body, i - 2)

    for hh in range(heads):
        o_ref[:, hh * LANES:(hh + 1) * LANES] = acc_ref[hh].astype(o_ref.dtype)


def _sb_attention_prompt(proj, batch, seq, *, w_sb):
    n = proj.shape[0]
    hd = w_sb // H_SB
    assert hd == LANES
    qb = SB_BLOCK
    assert seq % qb == 0
    nq = seq // qb
    heads = H_SB
    wb = heads * hd
    kcol = w_sb // wb
    vcol = 2 * w_sb // wb
    est = 4 * qb * wb * 2 + 2 * seq * wb * 2 + 4 * heads * qb * LANES * 4 + 40 * qb * LANES * 4 * heads
    resident = lambda c0: pl.BlockSpec((seq, wb), lambda b, h, i: (b, c0 + h), pipeline_mode=pl.Buffered(1))
    return pl.pallas_call(
        functools.partial(_sb_prompt_kernel, qb=qb, heads=heads),
        grid=(batch, H_SB // heads, nq),
        in_specs=[
            pl.BlockSpec((qb, wb), lambda b, h, i: (b * nq + i, h)),
            resident(kcol),
            resident(vcol),
            pl.BlockSpec((2 * LANES, 2 * LANES), lambda b, h, i: (0, 0)),
        ],
        out_specs=pl.BlockSpec((qb, wb), lambda b, h, i: (b * nq + i, h)),
        out_shape=jax.ShapeDtypeStruct((n, w_sb), BF16),
        scratch_shapes=[pltpu.VMEM((heads, qb, LANES), F32), pltpu.VMEM((heads, qb, LANES), F32)],
        compiler_params=_cparams(("parallel", "parallel", "arbitrary"), est),
        name="sb_attention_prompt",
    )(proj, proj, proj, _cumsum_matrix())


def _sb_sample_kernel(q_ref, kn_ref, vn_ref, kc_hbm, vc_hbm, cum_ref, o_ref, r_ref, acc_ref, buf_ref, sem_ref,
                      *, tq, past, cb, stream0, n_streams):
    b = pl.program_id(0)
    cum = cum_ref[...]
    row = lax.broadcasted_iota(jnp.int32, (tq, LANES), 0)
    col = lax.broadcasted_iota(jnp.int32, (tq, LANES), 1)
    zero = jnp.zeros((tq, LANES), F32)
    rows = cb * H_SB
    latest = past - cb

    def chunk_copy(stream, start, slot, which):
        src = (kc_hbm, vc_hbm)[which].at[stream0 + stream, pl.ds(start * H_SB, rows), :]
        return pltpu.make_async_copy(src, buf_ref.at[slot, which], sem_ref.at[slot, which])

    def cached(slot, which, hh):
        return buf_ref[slot, which, pl.ds(hh, cb, stride=H_SB), :].astype(BF16)

    slot = lax.rem(b, 2)

    @pl.when(b == 0)
    def _():
        for which in range(2):
            chunk_copy(0, latest, 0, which).start()

    for which in range(2):
        chunk_copy(b, latest, slot, which).wait()

    @pl.when(b + 1 < n_streams)
    def _():
        for which in range(2):
            chunk_copy(b + 1, latest, 1 - slot, which).start()

    steps = []
    for hh in range(H_SB):
        cs = slice(hh * LANES, (hh + 1) * LANES)
        chunks = [
            (cached(slot, 0, hh), cached(slot, 1, hh), None, None),
            (kn_ref[:, cs], vn_ref[:, cs], [col < row], None),
        ]
        steps.append(_sb_step_phases(q_ref[:, cs], chunks, zero, cum))
    for hh, (pv, r) in enumerate(_lockstep(steps)):
        acc_ref[hh] = pv
        r_ref[hh] = r

    def cond(j):
        return jnp.logical_and(j >= 0, jnp.min(r_ref[...]) < SB_EXIT_MASS)

    def body(j):
        for which in range(2):
            chunk_copy(b, j * cb, 2, which).start()
        for which in range(2):
            chunk_copy(b, j * cb, 2, which).wait()
        steps = []
        for hh in range(H_SB):
            cs = slice(hh * LANES, (hh + 1) * LANES)
            chunks = [(cached(2, 0, hh), cached(2, 1, hh), None, None)]
            steps.append(_sb_step_phases(q_ref[:, cs], chunks, r_ref[hh], cum))
        for hh, (pv, r) in enumerate(_lockstep(steps)):
            acc_ref[hh] += pv
            r_ref[hh] = r
        return j - 1

    lax.while_loop(cond, body, past // cb - 2)

    for hh in range(H_SB):
        o_ref[:, hh * LANES:(hh + 1) * LANES] = acc_ref[hh].astype(o_ref.dtype)


def _sb_attention_sample(proj, k_cache, v_cache, layer, batch, tq, *, w_sb):
    n = proj.shape[0]
    hd = w_sb // H_SB
    past = k_cache.shape[1] // H_SB
    cb = SB_BLOCK if past % SB_BLOCK == 0 else LANES
    assert hd == LANES and past % cb == 0 and tq <= LANES
    kv_new = proj[:, w_sb:3 * w_sb].reshape(batch, tq, 2 * w_sb)
    kv_new = jnp.pad(kv_new, ((0, 0), (0, LANES - tq), (0, 0)))
    est = 6 * cb * w_sb * 4 + 8 * LANES * w_sb * 2 + 40 * H_SB * max(tq, 8) * LANES * 4
    return pl.pallas_call(
        functools.partial(_sb_sample_kernel, tq=tq, past=past, cb=cb, stream0=layer * batch, n_streams=batch),
        grid=(batch,),
        in_specs=[
            pl.BlockSpec((tq, w_sb), lambda b: (b, 0)),
            pl.BlockSpec((None, LANES, w_sb), lambda b: (b, 0, 0)),
            pl.BlockSpec((None, LANES, w_sb), lambda b: (b, 0, 1)),
            pl.BlockSpec(memory_space=pl.ANY),
            pl.BlockSpec(memory_space=pl.ANY),
            pl.BlockSpec((2 * LANES, 2 * LANES), lambda b: (0, 0)),
        ],
        out_specs=pl.BlockSpec((tq, w_sb), lambda b: (b, 0)),
        out_shape=jax.ShapeDtypeStruct((n, w_sb), BF16),
        scratch_shapes=[pltpu.VMEM((H_SB, tq, LANES), F32), pltpu.VMEM((H_SB, tq, LANES), F32),
                        pltpu.VMEM((3, 2, cb * H_SB, hd), F32), pltpu.SemaphoreType.DMA((3, 2))],
        compiler_params=_cparams(("arbitrary",), est),
        name="sb_attention_sample",
    )(proj, kv_new, kv_new, k_cache, v_cache, _cumsum_matrix())


def _retention_kernel(dec_ref, q_ref, k_ref, va_ref, vb_ref, ga_ref, gb_ref, cos_ref, sin_ref, gnw_ref,
                      s0_ref, o_ref, sout_ref, s_ref, d_ref, qd_ref, kd_ref, *, blk, k_scale, dk, dv):
    c = pl.program_id(1)
    per_half = H_RET // 2

    @pl.when(c == 0)
    def _():
        s_ref[...] = s0_ref[...]
        n = lax.broadcasted_iota(jnp.int32, (blk, blk), 0)
        m = lax.broadcasted_iota(jnp.int32, (blk, blk), 1)
        causal = n >= m
        diff = jnp.where(causal, n - m, 0).astype(F32)
        rows = lax.broadcasted_iota(jnp.int32, (blk, LANES), 0).astype(F32)
        for hh in range(H_RET):
            lg = dec_ref[0, hh]
            d_ref[hh] = jnp.where(causal, jnp.exp(diff * lg) * k_scale, 0.0)
            qd_ref[hh] = jnp.exp((rows + 1.0) * lg)
            kd_ref[hh] = jnp.exp((blk - 1.0 - rows) * lg) * k_scale

    cos = cos_ref[...]
    sin = sin_ref[...]
    half = dk // 2
    for hh in range(H_RET):
        ks = slice(hh * dk, (hh + 1) * dk)
        vs = slice(hh * dv, (hh + 1) * dv)
        qf = q_ref[:, ks].astype(F32)
        kf = k_ref[:, ks].astype(F32)
        qr = qf * cos + pltpu.roll(qf, half, 1) * sin
        kr = kf * cos + pltpu.roll(kf, half, 1) * sin
        hs = slice((hh % per_half) * dv, (hh % per_half + 1) * dv)
        v = (va_ref if hh < per_half else vb_ref)[:, hs]

        scores = lax.dot_general(qr.astype(BF16), kr.astype(BF16), (((1,), (1,)), ((), ())),
                                 preferred_element_type=F32) * d_ref[hh]
        inner = jnp.dot(scores.astype(BF16), v, preferred_element_type=F32)
        s = s_ref[hh]
        cross = jnp.dot((qr * qd_ref[hh]).astype(BF16), s.astype(BF16), preferred_element_type=F32)
        o = inner + cross
        kv = lax.dot_general((kr * kd_ref[hh]).astype(BF16), v, (((0,), (0,)), ((), ())),
                             preferred_element_type=F32)
        s_new = dec_ref[1, hh] * s + kv
        s_ref[hh] = s_new
        sout_ref[hh] = s_new

        mu = jnp.mean(o, axis=-1, keepdims=True)
        dev = o - mu
        var = jnp.mean(dev * dev, axis=-1, keepdims=True)
        y = dev * lax.rsqrt(var + GN_EPS) * gnw_ref[:, vs]
        gf = (ga_ref if hh < per_half else gb_ref)[:, hs].astype(F32)
        o_ref[:, vs] = (gf * jax.nn.sigmoid(gf) * y).astype(o_ref.dtype)


def _retention(proj, gn_w, state0, batch, seq, pos0, *, w_sb, wk_ret, wv_ret):
    n = proj.shape[0]
    dk = wk_ret // H_RET
    dv = wv_ret // H_RET
    assert dk == LANES and dv % LANES == 0
    blk = min(seq, RET_CHUNK_MAX)
    assert seq % blk == 0
    nc = seq // blk
    hw = wv_ret // 2
    assert (3 * w_sb) % wk_ret == 0 and (3 * w_sb + 2 * wk_ret) % hw == 0 and H_RET % 2 == 0
    q0 = 3 * w_sb // wk_ret
    k0 = q0 + 1
    v0 = (3 * w_sb + 2 * wk_ret) // hw
    g0 = v0 + 2

    half = dk // 2
    inv = ROPE_BASE ** (-jnp.arange(half, dtype=F32) / half)
    ang = (pos0 + jnp.arange(seq, dtype=F32))[:, None] * inv[None, :]
    cos = jnp.concatenate([jnp.cos(ang), jnp.cos(ang)], axis=1)
    sin = jnp.concatenate([-jnp.sin(ang), jnp.sin(ang)], axis=1)
    log_gamma = jnp.log1p(-jnp.exp2(-5.0 - jnp.arange(H_RET, dtype=F32)))
    dec = jnp.stack([log_gamma, jnp.exp(blk * log_gamma)])

    est = 2 * blk * (2 * wk_ret + 2 * wv_ret) * 2 + 4 * blk * dk * 4 + 5 * H_RET * dk * dv * 4 \
        + 2 * blk * wv_ret * 2 + H_RET * (blk * blk + 2 * blk * LANES) * 4 + 12 * H_RET * blk * dv * 4
    row = lambda b, c: b * nc + c
    return pl.pallas_call(
        functools.partial(_retention_kernel, blk=blk, k_scale=dk ** -0.5, dk=dk, dv=dv),
        grid=(batch, nc),
        in_specs=[
            pl.BlockSpec(memory_space=pltpu.SMEM),
            pl.BlockSpec((blk, wk_ret), lambda b, c: (row(b, c), q0)),
            pl.BlockSpec((blk, wk_ret), lambda b, c: (row(b, c), k0)),
            pl.BlockSpec((blk, hw), lambda b, c: (row(b, c), v0)),
            pl.BlockSpec((blk, hw), lambda b, c: (row(b, c), v0 + 1)),
            pl.BlockSpec((blk, hw), lambda b, c: (row(b, c), g0)),
            pl.BlockSpec((blk, hw), lambda b, c: (row(b, c), g0 + 1)),
            pl.BlockSpec((blk, dk), lambda b, c: (c, 0)),
            pl.BlockSpec((blk, dk), lambda b, c: (c, 0)),
            pl.BlockSpec((1, wv_ret), lambda b, c: (0, 0)),
            pl.BlockSpec((None, H_RET, dk, dv), lambda b, c: (b, 0, 0, 0)),
        ],
        out_specs=[
            pl.BlockSpec((blk, wv_ret), lambda b, c: (row(b, c), 0)),
            pl.BlockSpec((None, H_RET, dk, dv), lambda b, c: (b, 0, 0, 0)),
        ],
        out_shape=[jax.ShapeDtypeStruct((n, wv_ret), BF16),
                   jax.ShapeDtypeStruct((batch, H_RET, dk, dv), F32)],
        scratch_shapes=[pltpu.VMEM((H_RET, dk, dv), F32), pltpu.VMEM((H_RET, blk, blk), F32),
                        pltpu.VMEM((H_RET, blk, LANES), F32), pltpu.VMEM((H_RET, blk, LANES), F32)],
        compiler_params=_cparams(("parallel", "arbitrary"), est),
        name="retention",
    )(dec, proj, proj, proj, proj, proj, proj, cos, sin, gn_w.reshape(1, wv_ret), state0)


def _merge_kernel(osb_ref, or_ref, asb0_ref, asb1_ref, ar0_ref, ar1_ref, h_ref,
                  wsb_ref, wr_ref, wo_ref, out_ref):
    b_sb = jnp.dot(osb_ref[...], wsb_ref[...], preferred_element_type=F32)
    b_r = jnp.dot(or_ref[...], wr_ref[...], preferred_element_type=F32)
    a_sb = jnp.concatenate([asb0_ref[...], asb1_ref[...]], axis=1).astype(F32)
    a_r = jnp.concatenate([ar0_ref[...], ar1_ref[...]], axis=1).astype(F32)
    merged = jax.nn.sigmoid(a_sb) * b_sb + jax.nn.sigmoid(a_r) * b_r
    out_ref[...] = h_ref[...] + jnp.dot(merged.astype(BF16), wo_ref[...], preferred_element_type=F32)


def _merge(o_sb, o_r, proj, h, w_sb_o, w_ret_o, w_out, *, w_sb, wk_ret, wv_ret):
    n, d = h.shape
    tm = _row_tile(n, 512)
    half = d // 2
    a0 = (3 * w_sb + 2 * wk_ret + 2 * wv_ret) // half
    assert (3 * w_sb + 2 * wk_ret + 2 * wv_ret) % half == 0
    gate = lambda t: pl.BlockSpec((tm, half), lambda i: (i, a0 + t))
    full = lambda a: pl.BlockSpec(a.shape, lambda i: (0, 0))
    est = 2 * tm * (w_sb + wv_ret + 2 * d) * 2 + 4 * tm * d * 4 + 2 * (w_sb + wv_ret + d) * d * 2 + 6 * tm * d * 4
    return pl.pallas_call(
        _merge_kernel,
        grid=(n // tm,),
        in_specs=[
            pl.BlockSpec((tm, w_sb), lambda i: (i, 0)),
            pl.BlockSpec((tm, wv_ret), lambda i: (i, 0)),
            gate(0), gate(1), gate(2), gate(3),
            pl.BlockSpec((tm, d), lambda i: (i, 0)),
            full(w_sb_o), full(w_ret_o), full(w_out),
        ],
        out_specs=pl.BlockSpec((tm, d), lambda i: (i, 0)),
        out_shape=jax.ShapeDtypeStruct((n, d), F32),
        compiler_params=_cparams(("parallel",), est),
        name="merge",
    )(o_sb, o_r, proj, proj, proj, proj, h, w_sb_o, w_ret_o, w_out)


def _finish(h, acc, fw_ref, out_ref):
    y = h + acc
    if fw_ref is not None:
        y = _rms_scale(y, fw_ref[...])
    out_ref[...] = y


def _ffn_kernel(h_ref, nw_ref, wg_ref, wu_ref, wd_ref, *rest, final, tf):
    fw_ref = rest[0] if final else None
    out_ref = rest[-1]
    h = h_ref[...]
    hn = _rms_scale(h, nw_ref[...]).astype(BF16)
    acc = None
    for f in range(wg_ref.shape[1] // tf):
        cols = slice(f * tf, (f + 1) * tf)
        g = jnp.dot(hn, wg_ref[:, cols], preferred_element_type=F32)
        u = jnp.dot(hn, wu_ref[:, cols], preferred_element_type=F32)
        act = (g * jax.nn.sigmoid(g) * u).astype(BF16)
        t = jnp.dot(act, wd_ref[cols, :], preferred_element_type=F32)
        acc = t if acc is None else acc + t
    _finish(h, acc, fw_ref, out_ref)


def _ffn(h, norm_w, wg, wu, wd, final_w):
    n, d = h.shape
    ff = wg.shape[1]
    tm = _row_tile(n, 512)
    tf = 2 * LANES
    assert ff % tf == 0
    final = final_w is not None
    resident = lambda a: pl.BlockSpec(a.shape, lambda i: (0, 0), pipeline_mode=pl.Buffered(1))
    in_specs = [
        pl.BlockSpec((tm, d), lambda i: (i, 0)),
        pl.BlockSpec((1, d), lambda i: (0, 0)),
        resident(wg), resident(wu), resident(wd),
    ]
    args = [h, norm_w.reshape(1, d), wg, wu, wd]
    if final:
        in_specs.append(pl.BlockSpec((1, d), lambda i: (0, 0)))
        args.append(final_w.reshape(1, d))
    est = 4 * tm * d * 4 + tm * d * 2 + 3 * d * ff * 2 + 4 * tm * tf * 4 + 2 * tm * d * 4
    return pl.pallas_call(
        functools.partial(_ffn_kernel, final=final, tf=tf),
        grid=(n // tm,),
        in_specs=in_specs,
        out_specs=pl.BlockSpec((tm, d), lambda i: (i, 0)),
        out_shape=jax.ShapeDtypeStruct((n, d), F32),
        compiler_params=_cparams(("parallel",), est),
        name="ffn_dense",
    )(*args)


MOE_TILE = 512


def _interleave_store(dst_ref, lead, x, n_rows):
    groups = x.shape[1] // LANES
    for g in range(groups):
        dst_ref[lead + (pl.ds(g, n_rows, stride=groups), slice(None))] = x[:, g * LANES:(g + 1) * LANES]


def _interleave_load(src_ref, lead, n_rows, groups):
    parts = [src_ref[lead + (pl.ds(g, n_rows, stride=groups), slice(None))] for g in range(groups)]
    return jnp.concatenate(parts, axis=1)


def _moe_route_kernel(h_ref, nw_ref, rwh_ref, rwl_ref, tri_ref, xs_hbm, pos_ref, w_ref, cnt_ref,
                      x3_ref, run_ref, posv_ref, poss_ref, zero_ref, sem_ref, psem_ref, zsem_ref,
                      *, n_exp, cap, pad_to):
    i = pl.program_id(0)
    nt = pl.num_programs(0)
    tm, d = h_ref.shape
    groups = d // LANES
    slot = lax.rem(i, 2)

    def slot_copies(s):
        return pltpu.make_async_copy(x3_ref.at[s], xs_hbm.at[pl.ds(0, tm * groups), :], sem_ref.at[s])

    def row_copy(s, r, p):
        return pltpu.make_async_copy(x3_ref.at[s, pl.ds(r * groups, groups), :],
                                     xs_hbm.at[pl.ds(p * groups, groups), :], sem_ref.at[s])

    @pl.when(i == 0)
    def _():
        run_ref[...] = jnp.zeros_like(run_ref)
        zero_ref[...] = jnp.zeros_like(zero_ref)

    @pl.when(i >= 2)
    def _():
        slot_copies(slot).wait()
        slot_copies(slot).wait()

    hn = _rms_scale(h_ref[...], nw_ref[...])
    _interleave_store(x3_ref, (slot,), hn, tm)

    hn_hi = hn.astype(BF16)
    hn_lo = (hn - hn_hi.astype(F32)).astype(BF16)
    logits = (jnp.dot(hn_hi, rwh_ref[...], preferred_element_type=F32)
              + jnp.dot(hn_lo, rwh_ref[...], preferred_element_type=F32)
              + jnp.dot(hn_hi, rwl_ref[...], preferred_element_type=F32))
    col = lax.broadcasted_iota(jnp.int32, logits.shape, 1).astype(F32)
    logits = jnp.where(col < n_exp, logits, -jnp.inf)
    m1 = jnp.max(logits, axis=-1, keepdims=True)
    i1 = jnp.min(jnp.where(logits == m1, col, float(LANES)), axis=-1, keepdims=True)
    rest_l = jnp.where(col == i1, -jnp.inf, logits)
    m2 = jnp.max(rest_l, axis=-1, keepdims=True)
    i2 = jnp.min(jnp.where(rest_l == m2, col, float(LANES)), axis=-1, keepdims=True)
    ex = jnp.exp(m2 - m1)
    w_ref[...] = jnp.concatenate([jnp.broadcast_to(1.0 / (1.0 + ex), (tm, LANES)),
                                  jnp.broadcast_to(ex / (1.0 + ex), (tm, LANES))], axis=1)

    onehot = jnp.where(jnp.logical_or(col == i1, col == i2), 1.0, 0.0)
    ranks = jnp.dot(tri_ref[...], onehot.astype(BF16), preferred_element_type=F32) + run_ref[...]
    pos1 = i1 * float(cap) + jnp.sum(jnp.where(col == i1, ranks, 0.0), axis=-1, keepdims=True)
    pos2 = i2 * float(cap) + jnp.sum(jnp.where(col == i2, ranks, 0.0), axis=-1, keepdims=True)
    run = run_ref[...] + jnp.sum(onehot, axis=0, keepdims=True)
    run_ref[...] = run

    meta = jnp.where(col == 0.0, pos1, jnp.where(col == 1.0, pos2, 0.0))
    posv = jnp.transpose(meta)[0:TOP_K, :].astype(jnp.int32)
    pos_ref[...] = posv
    posv_ref[...] = posv
    to_smem = pltpu.make_async_copy(posv_ref, poss_ref, psem_ref.at[0])
    to_smem.start()
    to_smem.wait()

    def issue(r, carry):
        row_copy(slot, r, poss_ref[0, r]).start()
        row_copy(slot, r, poss_ref[1, r]).start()
        return carry

    lax.fori_loop(0, tm, issue, 0, unroll=8)

    @pl.when(i == nt - 1)
    def _():
        slot_copies(slot).wait()
        slot_copies(slot).wait()

        @pl.when(nt > 1)
        def _():
            slot_copies(1 - slot).wait()
            slot_copies(1 - slot).wait()

        cnt_ref[...] = run
        lane = lax.broadcasted_iota(jnp.int32, run.shape, 1)
        fills = []
        for x in range(n_exp):
            c = jnp.sum(jnp.where(lane == x, run, 0.0)).astype(jnp.int32)
            gap = lax.div(c + (pad_to - 1), pad_to) * pad_to - c
            start = x * cap + c
            for bit in reversed(range(pad_to.bit_length() - 1)):
                size = 1 << bit
                blk = pltpu.make_async_copy(zero_ref.at[pl.ds(0, size * groups), :],
                                            xs_hbm.at[pl.ds(start * groups, size * groups), :], zsem_ref.at[0])
                has = jnp.bitwise_and(lax.shift_right_logical(gap, bit), 1)
                fills.append((has, blk))
                start = start + has * size
        for has, blk in fills:
            pl.when(has == 1)(blk.start)
        for has, blk in fills:
            pl.when(has == 1)(blk.wait)


def _moe_expert_kernel(te_ref, tb_ref, ok_ref, xs_ref, wg_ref, wu_ref, wd_ref, ys_ref, *, rows, groups):
    t = pl.program_id(0)

    @pl.when(ok_ref[t] == 1)
    def _():
        x = _interleave_load(xs_ref, (), rows, groups).astype(BF16)
        g = jnp.dot(x, wg_ref[...], preferred_element_type=F32)
        u = jnp.dot(x, wu_ref[...], preferred_element_type=F32)
        act = (g * jax.nn.sigmoid(g) * u).astype(BF16)
        _interleave_store(ys_ref, (), jnp.dot(act, wd_ref[...], preferred_element_type=F32), rows)


def _moe_combine_kernel(pos_ref, nxt_ref, h_ref, w_ref, *rest, final):
    fw_ref = rest[0] if final else None
    ys_hbm, out_ref, g_ref, sem_ref = rest[1:] if final else rest
    i = pl.program_id(0)
    nt = pl.num_programs(0)
    tm, d = h_ref.shape
    groups = d // LANES
    slot = lax.rem(i, 2)

    def gather_tile(p_ref, s):
        def issue(r, carry):
            for k in range(TOP_K):
                pltpu.make_async_copy(ys_hbm.at[pl.ds(p_ref[k, r] * groups, groups), :],
                                      g_ref.at[s, k, pl.ds(r * groups, groups), :], sem_ref.at[s, k]).start()
            return carry

        lax.fori_loop(0, tm, issue, 0, unroll=8)

    @pl.when(i == 0)
    def _():
        gather_tile(pos_ref, 0)

    @pl.when(i + 1 < nt)
    def _():
        gather_tile(nxt_ref, 1 - slot)

    for k in range(TOP_K):
        pltpu.make_async_copy(ys_hbm.at[pl.ds(0, tm * groups), :], g_ref.at[slot, k], sem_ref.at[slot, k]).wait()

    w = w_ref[...]
    w1 = jnp.concatenate([w[:, :LANES]] * groups, axis=1)
    w2 = jnp.concatenate([w[:, LANES:]] * groups, axis=1)
    y = (w1 * _interleave_load(g_ref, (slot, 0), tm, groups)
         + w2 * _interleave_load(g_ref, (slot, 1), tm, groups))
    _finish(h_ref[...], y, fw_ref, out_ref)


def _moe(h, norm_w, router_w, wg, wu, wd, final_w):
    n, d = h.shape
    n_exp, _, fp = wg.shape
    groups = d // LANES
    assert d % LANES == 0 and groups == 8, "a row must fill one f32 (8,128) tile to move as one contiguous copy"
    tm = _row_tile(n, 1024)
    nt = n // tm
    tmx = min(MOE_TILE, n)
    assert tmx & (tmx - 1) == 0, "the zero fill writes power-of-two blocks below one tile"
    cap = -(-n // tmx) * tmx
    blocks_per_expert = cap // tmx
    dummy_block = n_exp * blocks_per_expert
    sorted_rows = (n_exp * cap + tmx) * groups
    final = final_w is not None
    rw = jnp.pad(router_w.astype(F32), ((0, 0), (0, LANES - n_exp)))
    rw_hi = rw.astype(BF16)
    rw_lo = (rw - rw_hi.astype(F32)).astype(BF16)
    tri = (jnp.arange(tm)[None, :] < jnp.arange(tm)[:, None]).astype(BF16)

    est = 4 * tm * d * 4 + 4 * d * LANES * 2 + 2 * tm * tm * 2 + 2 * tm * d * 4 + 8 * tm * LANES * 4 + 6 * tm * d * 4
    xs, pos, wts, cnt = pl.pallas_call(
        functools.partial(_moe_route_kernel, n_exp=n_exp, cap=cap, pad_to=tmx),
        grid=(nt,),
        in_specs=[
            pl.BlockSpec((tm, d), lambda i: (i, 0)),
            pl.BlockSpec((1, d), lambda i: (0, 0)),
            pl.BlockSpec((d, LANES), lambda i: (0, 0)),
            pl.BlockSpec((d, LANES), lambda i: (0, 0)),
            pl.BlockSpec((tm, tm), lambda i: (0, 0)),
        ],
        out_specs=[
            pl.BlockSpec(memory_space=pl.ANY),
            pl.BlockSpec((None, TOP_K, tm), lambda i: (i, 0, 0)),
            pl.BlockSpec((tm, TOP_K * LANES), lambda i: (i, 0)),
            pl.BlockSpec((1, LANES), lambda i: (0, 0)),
        ],
        out_shape=[
            jax.ShapeDtypeStruct((sorted_rows, LANES), F32),
            jax.ShapeDtypeStruct((nt, TOP_K, tm), jnp.int32),
            jax.ShapeDtypeStruct((n, TOP_K * LANES), F32),
            jax.ShapeDtypeStruct((1, LANES), F32),
        ],
        scratch_shapes=[
            pltpu.VMEM((2, tm * groups, LANES), F32), pltpu.VMEM((1, LANES), F32),
            pltpu.VMEM((TOP_K, tm), jnp.int32), pltpu.SMEM((TOP_K, tm), jnp.int32),
            pltpu.VMEM((max(tmx // 2, 1) * groups, LANES), F32),
            pltpu.SemaphoreType.DMA((2,)), pltpu.SemaphoreType.DMA((1,)), pltpu.SemaphoreType.DMA((1,)),
        ],
        compiler_params=_cparams(("arbitrary",), est),
        name="moe_route",
    )(h, norm_w.reshape(1, d), rw_hi, rw_lo, tri)

    counts = cnt[0, :n_exp].astype(jnp.int32)
    tiles = (counts + (tmx - 1)) // tmx
    first = jnp.cumsum(tiles) - tiles
    n_steps = 2 * n // tmx + n_exp
    step = jnp.arange(n_steps, dtype=jnp.int32)
    te = jnp.clip(jnp.sum((step[:, None] >= (first + tiles)[None, :]).astype(jnp.int32), axis=1), 0, n_exp - 1)
    ok = (step < jnp.sum(tiles)).astype(jnp.int32)
    tb = jnp.where(ok == 1, te * blocks_per_expert + step - first[te], dummy_block).astype(jnp.int32)

    est = 4 * tmx * d * 4 + 6 * d * fp * 2 + 4 * tmx * fp * 4 + 2 * tmx * d * 4
    sorted_spec = pl.BlockSpec((tmx * groups, LANES), lambda t, te, tb, ok: (tb[t], 0))
    ys = pl.pallas_call(
        functools.partial(_moe_expert_kernel, rows=tmx, groups=groups),
        grid_spec=pltpu.PrefetchScalarGridSpec(
            num_scalar_prefetch=3,
            grid=(n_steps,),
            in_specs=[
                sorted_spec,
                pl.BlockSpec((None, d, fp), lambda t, te, tb, ok: (te[t], 0, 0)),
                pl.BlockSpec((None, d, fp), lambda t, te, tb, ok: (te[t], 0, 0)),
                pl.BlockSpec((None, fp, d), lambda t, te, tb, ok: (te[t], 0, 0)),
            ],
            out_specs=sorted_spec,
        ),
        out_shape=jax.ShapeDtypeStruct((sorted_rows, LANES), F32),
        compiler_params=_cparams(("arbitrary",), est),
        name="moe_experts",
    )(te, tb, ok, xs, wg, wu, wd)

    in_specs = [
        pl.BlockSpec((None, TOP_K, tm), lambda i: (i, 0, 0), memory_space=pltpu.SMEM),
        pl.BlockSpec((None, TOP_K, tm), lambda i: (jnp.minimum(i + 1, nt - 1), 0, 0), memory_space=pltpu.SMEM),
        pl.BlockSpec((tm, d), lambda i: (i, 0)),
        pl.BlockSpec((tm, TOP_K * LANES), lambda i: (i, 0)),
    ]
    args = [pos, pos, h, wts]
    if final:
        in_specs.append(pl.BlockSpec((1, d), lambda i: (0, 0)))
        args.append(final_w.reshape(1, d))
    in_specs.append(pl.BlockSpec(memory_space=pl.ANY))
    args.append(ys)
    est = 4 * tm * d * 4 + 2 * tm * TOP_K * LANES * 4 + 2 * TOP_K * tm * d * 4 + 4 * tm * d * 4
    return pl.pallas_call(
        functools.partial(_moe_combine_kernel, final=final),
        grid=(nt,),
        in_specs=in_specs,
        out_specs=pl.BlockSpec((tm, d), lambda i: (i, 0)),
        out_shape=jax.ShapeDtypeStruct((n, d), F32),
        scratch_shapes=[pltpu.VMEM((2, TOP_K, tm * groups, LANES), F32), pltpu.SemaphoreType.DMA((2, TOP_K))],
        compiler_params=_cparams(("arbitrary",), est),
        name="moe_combine",
    )(*args)


def _run_group(x, pos0, caches, states, weights):
    (norm_mix_w, w_in, w_sb_o, w_ret_o, gn_ret_w, w_out, norm_ffn_w, ffn_w_gate, ffn_w_up,
     ffn_w_down, router_w, moe_w_gate, moe_w_up, moe_w_down, norm_final_w) = weights
    batch, seq, d = x.shape
    depth = len(w_in)
    w_sb = w_sb_o[0].shape[0]
    wv_ret = w_ret_o[0].shape[0]
    d_in = w_in[0].shape[1]
    wk_ret = (d_in - 3 * w_sb - 2 * wv_ret - 2 * d) // 2
    dims = dict(w_sb=w_sb, wk_ret=wk_ret, wv_ret=wv_ret)
    dk, dv = wk_ret // H_RET, wv_ret // H_RET
    hd = w_sb // H_SB

    h = x.reshape(batch * seq, d)
    kv = None
    finals = []
    if caches is not None:
        kc = caches[0].reshape(depth * batch, -1, hd)
        vc = caches[1].reshape(depth * batch, -1, hd)
    for l in range(depth):
        proj, k_all, v_all = _in_projection(h, norm_mix_w[l], w_in[l], l, depth, kv, w_sb=w_sb)
        kv = (k_all, v_all)
        if caches is None:
            o_sb = _sb_attention_prompt(proj, batch, seq, w_sb=w_sb)
            s0 = jnp.zeros((batch, H_RET, dk, dv), F32)
        else:
            o_sb = _sb_attention_sample(proj, kc, vc, l, batch, seq, w_sb=w_sb)
            s0 = states[l].astype(F32)
        o_r, s_fin = _retention(proj, gn_ret_w[l], s0, batch, seq, pos0, **dims)
        finals.append(s_fin)
        h = _merge(o_sb, o_r, proj, h, w_sb_o[l], w_ret_o[l], w_out[l], **dims)
        fw = norm_final_w if l == depth - 1 else None
        if l % 2 == 0:
            h = _ffn(h, norm_ffn_w[l], ffn_w_gate[l // 2], ffn_w_up[l // 2], ffn_w_down[l // 2], fw)
        else:
            h = _moe(h, norm_ffn_w[l], router_w[l // 2], moe_w_gate[l // 2], moe_w_up[l // 2],
                     moe_w_down[l // 2], fw)
    y = h.reshape(batch, seq, d)
    new_k = kv[0].reshape(depth, batch, seq, H_SB, hd)
    new_v = kv[1].reshape(depth, batch, seq, H_SB, hd)
    return y, new_k, new_v, jnp.stack(finals)


def kernel(x_prompt, x_sample, cache_sb_k, cache_sb_v, state_ret, norm_mix_w, w_in, w_sb_o, w_ret_o,
           gn_ret_w, w_out, norm_ffn_w, ffn_w_gate, ffn_w_up, ffn_w_down, router_w, moe_w_gate,
           moe_w_up, moe_w_down, norm_final_w):
    per_layer = lambda w: [w[l].astype(BF16) for l in range(w.shape[0])]
    weights = (
        norm_mix_w, per_layer(w_in), per_layer(w_sb_o), per_layer(w_ret_o), gn_ret_w,
        per_layer(w_out), norm_ffn_w, per_layer(ffn_w_gate), per_layer(ffn_w_up),
        per_layer(ffn_w_down), router_w,
        per_layer(moe_w_gate), per_layer(moe_w_up), per_layer(moe_w_down), norm_final_w,
    )
    past = cache_sb_k.shape[2]
    y_p, k_p, v_p, r_p = _run_group(x_prompt, 0.0, None, None, weights)
    y_s, k_s, v_s, r_s = _run_group(x_sample, float(past), (cache_sb_k, cache_sb_v), state_ret, weights)
    return (y_p, y_s, k_p, v_p, r_p, k_s, v_s, r_s)
```

```python
import functools

import jax
import jax.numpy as jnp
from jax import lax
from jax.experimental import pallas as pl
from jax.experimental.pallas import tpu as pltpu

F32 = jnp.float32
BF16 = jnp.bfloat16

H_SB = 4
H_RET = 4
TOP_K = 2
ROPE_BASE = 10000.0
EPS = 1e-6
GN_EPS = 1e-5
RET_CHUNK_MAX = 256

LANES = 128
V7X_VMEM_BYTES = 64 * 1024 * 1024
VMEM_CAP_BYTES = V7X_VMEM_BYTES - 8 * 1024 * 1024

SB_EXIT_MASS = 104.0
SB_BLOCK = 256


def _cparams(sem, est_bytes):
    limit = int(min(VMEM_CAP_BYTES, max(32 * 1024 * 1024, 2 * est_bytes)))
    return pltpu.CompilerParams(dimension_semantics=sem, vmem_limit_bytes=limit)


def _row_tile(n, want):
    t = min(n, want)
    assert n % t == 0, (n, t)
    return t


def _rms_scale(x, w):
    ms = jnp.mean(x * x, axis=-1, keepdims=True)
    return x * lax.rsqrt(ms + EPS) * w


def _inproj_kernel(x_ref, nw_ref, w_ref, *rest, q_scale, n_alias, tm, tn):
    proj_ref, k_ref, v_ref = rest[n_alias:]
    hn = _rms_scale(x_ref[...], nw_ref[...]).astype(BF16)

    def store_heads(dst_ref, acc):
        for hh in range(H_SB):
            dst_ref[pl.ds(hh, tm, stride=H_SB), :] = acc[:, hh * LANES:(hh + 1) * LANES]

    for j in range(w_ref.shape[1] // tn):
        cols = slice(j * tn, (j + 1) * tn)
        acc = jnp.dot(hn, w_ref[:, cols], preferred_element_type=F32)
        if j == 1:
            store_heads(k_ref, acc)
        if j == 2:
            store_heads(v_ref, acc)
        proj_ref[:, cols] = (acc * q_scale if j == 0 else acc).astype(BF16)


def _in_projection(x, norm_w, w_bf, layer, depth, kv_prev, *, w_sb):
    n, d = x.shape
    d_in = w_bf.shape[1]
    tn = w_sb
    assert d_in % tn == 0
    tm = _row_tile(n, 512)
    hd = w_sb // H_SB
    assert hd == LANES
    nm = n // tm
    kv_shape = jax.ShapeDtypeStruct((depth * n * H_SB, hd), F32)
    kv_spec = pl.BlockSpec((tm * H_SB, hd), lambda i: (layer * nm + i, 0))
    in_specs = [
        pl.BlockSpec((tm, d), lambda i: (i, 0)),
        pl.BlockSpec((1, d), lambda i: (0, 0)),
        pl.BlockSpec((d, d_in), lambda i: (0, 0), pipeline_mode=pl.Buffered(1)),
    ]
    args = [x, norm_w.reshape(1, d), w_bf]
    aliases = {}
    n_alias = 0
    if kv_prev is not None:
        in_specs += [pl.BlockSpec(memory_space=pl.ANY), pl.BlockSpec(memory_space=pl.ANY)]
        args += [kv_prev[0], kv_prev[1]]
        aliases = {3: 1, 4: 2}
        n_alias = 2
    est = 2 * tm * d * 4 + tm * d * 2 + d * d_in * 2 + 2 * tm * d_in * 2 + 4 * tm * w_sb * 4 + 4 * tm * tn * 4
    proj, k_all, v_all = pl.pallas_call(
        functools.partial(_inproj_kernel, q_scale=hd ** -0.5, n_alias=n_alias, tm=tm, tn=tn),
        grid=(nm,),
        in_specs=in_specs,
        out_specs=[pl.BlockSpec((tm, d_in), lambda i: (i, 0)), kv_spec, kv_spec],
        out_shape=[jax.ShapeDtypeStruct((n, d_in), BF16), kv_shape, kv_shape],
        input_output_aliases=aliases,
        compiler_params=_cparams(("parallel",), est),
        name=f"in_projection_l{layer}",
    )(*args)
    return proj, k_all, v_all


def _cumsum_matrix():
    j = jnp.arange(LANES)[:, None]
    s = jnp.arange(LANES)[None, :]
    tri = (j >= s).astype(BF16)
    m = jnp.concatenate([tri, jnp.ones((LANES, LANES), BF16)], axis=1)
    return jnp.concatenate([m, m], axis=0)


def _sb_step_phases(q, chunks, r, cum):
    zs = [lax.dot_general(q, k, (((1,), (1,)), ((), ())), preferred_element_type=F32)
          for k, _, _, _ in chunks]
    yield
    groups = []
    for (k, _, masks, live), z in zip(chunks, zs):
        for g in range(k.shape[0] // LANES):
            zg = z[:, g * LANES:(g + 1) * LANES]
            sp = jnp.maximum(zg, 0.0) + jnp.log(1.0 + jnp.exp(-jnp.abs(zg)))
            m = None if masks is None else masks[g]
            if m is not None:
                sp = jnp.where(m, sp, 0.0)
            if live is not None:
                sp = jnp.where(live, sp, 0.0)
            hi = sp.astype(BF16)
            lo = (sp - hi.astype(F32)).astype(BF16)
            groups.append([zg, jnp.concatenate([hi, lo], axis=1), m, live])
    yield
    for grp in groups:
        grp[1] = jnp.dot(grp[1], cum, preferred_element_type=F32)
    yield
    weights = [None] * len(groups)
    for idx in reversed(range(len(groups))):
        zg, ct, m, live = groups[idx]
        a = jnp.exp(zg - ct[:, :LANES] - r)
        if m is not None:
            a = jnp.where(m, a, 0.0)
        if live is not None:
            a = jnp.where(live, a, 0.0)
        weights[idx] = a.astype(BF16)
        r = r + ct[:, LANES:]
    yield
    pv = None
    idx = 0
    for _, v, _, _ in chunks:
        n_g = v.shape[0] // LANES
        a = weights[idx] if n_g == 1 else jnp.concatenate(weights[idx:idx + n_g], axis=1)
        t = jnp.dot(a, v, preferred_element_type=F32)
        pv = t if pv is None else pv + t
        idx += n_g
    return pv, r


def _lockstep(gens):
    results = [None] * len(gens)
    running = list(range(len(gens)))
    while running:
        still = []
        for t in running:
            try:
                next(gens[t])
                still.append(t)
            except StopIteration as stop:
                results[t] = stop.value
        running = still
    return results


def _sb_step(q, chunks, r, cum):
    return _lockstep([_sb_step_phases(q, chunks, r, cum)])[0]


def _sb_prompt_kernel(q_ref, k_ref, v_ref, cum_ref, o_ref, r_ref, acc_ref, *, qb, heads):
    i = pl.program_id(2)
    cum = cum_ref[...]
    row = lax.broadcasted_iota(jnp.int32, (qb, LANES), 0)
    col = lax.broadcasted_iota(jnp.int32, (qb, LANES), 1)
    causal = [(col + g * LANES) < row for g in range(qb // LANES)]
    has_prev = i > 0
    prev0 = pl.multiple_of(jnp.maximum(i - 1, 0) * qb, qb)
    diag0 = pl.multiple_of(i * qb, qb)
    zero = jnp.zeros((qb, LANES), F32)

    steps = []
    for hh in range(heads):
        cs = slice(hh * LANES, (hh + 1) * LANES)
        chunks = [
            (k_ref[pl.ds(prev0, qb), cs], v_ref[pl.ds(prev0, qb), cs], None, has_prev),
            (k_ref[pl.ds(diag0, qb), cs], v_ref[pl.ds(diag0, qb), cs], causal, None),
        ]
        steps.append(_sb_step_phases(q_ref[:, cs], chunks, zero, cum))
    for hh, (pv, r) in enumerate(_lockstep(steps)):
        acc_ref[hh] = pv
        r_ref[hh] = r

    for hh in range(heads):
        cs = slice(hh * LANES, (hh + 1) * LANES)

        def cond(j, hh=hh):
            return jnp.logical_and(j >= 0, jnp.min(r_ref[hh]) < SB_EXIT_MASS)

        def body(j, hh=hh, cs=cs):
            start = pl.multiple_of(j * qb, qb)
            chunks = [(k_ref[pl.ds(start, qb), cs], v_ref[pl.ds(start, qb), cs], None, None)]
            pv, r = _sb_step(q_ref[:, cs], chunks, r_ref[hh], cum)
            acc_ref[hh] += pv
            r_ref[hh] = r
            return j - 1

        lax.while_loop(cond, body, i - 2)

    for hh in range(heads):
        o_ref[:, hh * LANES:(hh + 1) * LANES] = acc_ref[hh].astype(o_ref.dtype)


def _sb_attention_prompt(proj, batch, seq, *, w_sb):
    n = proj.shape[0]
    hd = w_sb // H_SB
    assert hd == LANES
    qb = SB_BLOCK
    assert seq % qb == 0
    nq = seq // qb
    heads = H_SB
    wb = heads * hd
    kcol = w_sb // wb
    vcol = 2 * w_sb // wb
    est = 4 * qb * wb * 2 + 2 * seq * wb * 2 + 4 * heads * qb * LANES * 4 + 40 * qb * LANES * 4 * heads
    resident = lambda c0: pl.BlockSpec((seq, wb), lambda b, h, i: (b, c0 + h), pipeline_mode=pl.Buffered(1))
    return pl.pallas_call(
        functools.partial(_sb_prompt_kernel, qb=qb, heads=heads),
        grid=(batch, H_SB // heads, nq),
        in_specs=[
            pl.BlockSpec((qb, wb), lambda b, h, i: (b * nq + i, h)),
            resident(kcol),
            resident(vcol),
            pl.BlockSpec((2 * LANES, 2 * LANES), lambda b, h, i: (0, 0)),
        ],
        out_specs=pl.BlockSpec((qb, wb), lambda b, h, i: (b * nq + i, h)),
        out_shape=jax.ShapeDtypeStruct((n, w_sb), BF16),
        scratch_shapes=[pltpu.VMEM((heads, qb, LANES), F32), pltpu.VMEM((heads, qb, LANES), F32)],
        compiler_params=_cparams(("parallel", "parallel", "arbitrary"), est),
        name="sb_attention_prompt",
    )(proj, proj, proj, _cumsum_matrix())


def _sb_sample_kernel(q_ref, kn_ref, vn_ref, kc_hbm, vc_hbm, cum_ref, o_ref, r_ref, acc_ref, buf_ref, sem_ref,
                      *, tq, past, cb, stream0, n_streams):
    b = pl.program_id(0)
    cum = cum_ref[...]
    row = lax.broadcasted_iota(jnp.int32, (tq, LANES), 0)
    col = lax.broadcasted_iota(jnp.int32, (tq, LANES), 1)
    zero = jnp.zeros((tq, LANES), F32)
    rows = cb * H_SB
    latest = past - cb

    def chunk_copy(stream, start, slot, which):
        src = (kc_hbm, vc_hbm)[which].at[stream0 + stream, pl.ds(start * H_SB, rows), :]
        return pltpu.make_async_copy(src, buf_ref.at[slot, which], sem_ref.at[slot, which])

    def cached(slot, which, hh):
        return buf_ref[slot, which, pl.ds(hh, cb, stride=H_SB), :].astype(BF16)

    slot = lax.rem(b, 2)

    @pl.when(b == 0)
    def _():
        for which in range(2):
            chunk_copy(0, latest, 0, which).start()

    for which in range(2):
        chunk_copy(b, latest, slot, which).wait()

    @pl.when(b + 1 < n_streams)
    def _():
        for which in range(2):
            chunk_copy(b + 1, latest, 1 - slot, which).start()

    steps = []
    for hh in range(H_SB):
        cs = slice(hh * LANES, (hh + 1) * LANES)
        chunks = [
            (cached(slot, 0, hh), cached(slot, 1, hh), None, None),
            (kn_ref[:, cs], vn_ref[:, cs], [col < row], None),
        ]
        steps.append(_sb_step_phases(q_ref[:, cs], chunks, zero, cum))
    for hh, (pv, r) in enumerate(_lockstep(steps)):
        acc_ref[hh] = pv
        r_ref[hh] = r

    def cond(j):
        return jnp.logical_and(j >= 0, jnp.min(r_ref[...]) < SB_EXIT_MASS)

    def body(j):
        for which in range(2):
            chunk_copy(b, j * cb, 2, which).start()
        for which in range(2):
            chunk_copy(b, j * cb, 2, which).wait()
        steps = []
        for hh in range(H_SB):
            cs = slice(hh * LANES, (hh + 1) * LANES)
            chunks = [(cached(2, 0, hh), cached(2, 1, hh), None, None)]
            steps.append(_sb_step_phases(q_ref[:, cs], chunks, r_ref[hh], cum))
        for hh, (pv, r) in enumerate(_lockstep(steps)):
            acc_ref[hh] += pv
            r_ref[hh] = r
        return j - 1

    lax.while_loop(cond, body, past // cb - 2)

    for hh in range(H_SB):
        o_ref[:, hh * LANES:(hh + 1) * LANES] = acc_ref[hh].astype(o_ref.dtype)


def _sb_attention_sample(proj, k_cache, v_cache, layer, batch, tq, *, w_sb):
    n = proj.shape[0]
    hd = w_sb // H_SB
    past = k_cache.shape[1] // H_SB
    cb = SB_BLOCK if past % SB_BLOCK == 0 else LANES
    assert hd == LANES and past % cb == 0 and tq <= LANES
    kv_new = proj[:, w_sb:3 * w_sb].reshape(batch, tq, 2 * w_sb)
    kv_new = jnp.pad(kv_new, ((0, 0), (0, LANES - tq), (0, 0)))
    est = 6 * cb * w_sb * 4 + 8 * LANES * w_sb * 2 + 40 * H_SB * max(tq, 8) * LANES * 4
    return pl.pallas_call(
        functools.partial(_sb_sample_kernel, tq=tq, past=past, cb=cb, stream0=layer * batch, n_streams=batch),
        grid=(batch,),
        in_specs=[
            pl.BlockSpec((tq, w_sb), lambda b: (b, 0)),
            pl.BlockSpec((None, LANES, w_sb), lambda b: (b, 0, 0)),
            pl.BlockSpec((None, LANES, w_sb), lambda b: (b, 0, 1)),
            pl.BlockSpec(memory_space=pl.ANY),
            pl.BlockSpec(memory_space=pl.ANY),
            pl.BlockSpec((2 * LANES, 2 * LANES), lambda b: (0, 0)),
        ],
        out_specs=pl.BlockSpec((tq, w_sb), lambda b: (b, 0)),
        out_shape=jax.ShapeDtypeStruct((n, w_sb), BF16),
        scratch_shapes=[pltpu.VMEM((H_SB, tq, LANES), F32), pltpu.VMEM((H_SB, tq, LANES), F32),
                        pltpu.VMEM((3, 2, cb * H_SB, hd), F32), pltpu.SemaphoreType.DMA((3, 2))],
        compiler_params=_cparams(("arbitrary",), est),
        name="sb_attention_sample",
    )(proj, kv_new, kv_new, k_cache, v_cache, _cumsum_matrix())


def _retention_kernel(dec_ref, q_ref, k_ref, va_ref, vb_ref, ga_ref, gb_ref, cos_ref, sin_ref, gnw_ref,
                      s0_ref, o_ref, sout_ref, s_ref, d_ref, qd_ref, kd_ref, *, blk, k_scale, dk, dv):
    c = pl.program_id(1)
    per_half = H_RET // 2

    @pl.when(c == 0)
    def _():
        s_ref[...] = s0_ref[...]
        n = lax.broadcasted_iota(jnp.int32, (blk, blk), 0)
        m = lax.broadcasted_iota(jnp.int32, (blk, blk), 1)
        causal = n >= m
        diff = jnp.where(causal, n - m, 0).astype(F32)
        rows = lax.broadcasted_iota(jnp.int32, (blk, LANES), 0).astype(F32)
        for hh in range(H_RET):
            lg = dec_ref[0, hh]
            d_ref[hh] = jnp.where(causal, jnp.exp(diff * lg) * k_scale, 0.0)
            qd_ref[hh] = jnp.exp((rows + 1.0) * lg)
            kd_ref[hh] = jnp.exp((blk - 1.0 - rows) * lg) * k_scale

    cos = cos_ref[...]
    sin = sin_ref[...]
    half = dk // 2
    for hh in range(H_RET):
        ks = slice(hh * dk, (hh + 1) * dk)
        vs = slice(hh * dv, (hh + 1) * dv)
        qf = q_ref[:, ks].astype(F32)
        kf = k_ref[:, ks].astype(F32)
        qr = qf * cos + pltpu.roll(qf, half, 1) * sin
        kr = kf * cos + pltpu.roll(kf, half, 1) * sin
        hs = slice((hh % per_half) * dv, (hh % per_half + 1) * dv)
        v = (va_ref if hh < per_half else vb_ref)[:, hs]

        scores = lax.dot_general(qr.astype(BF16), kr.astype(BF16), (((1,), (1,)), ((), ())),
                                 preferred_element_type=F32) * d_ref[hh]
        inner = jnp.dot(scores.astype(BF16), v, preferred_element_type=F32)
        s = s_ref[hh]
        cross = jnp.dot((qr * qd_ref[hh]).astype(BF16), s.astype(BF16), preferred_element_type=F32)
        o = inner + cross
        kv = lax.dot_general((kr * kd_ref[hh]).astype(BF16), v, (((0,), (0,)), ((), ())),
                             preferred_element_type=F32)
        s_new = dec_ref[1, hh] * s + kv
        s_ref[hh] = s_new
        sout_ref[hh] = s_new

        mu = jnp.mean(o, axis=-1, keepdims=True)
        dev = o - mu
        var = jnp.mean(dev * dev, axis=-1, keepdims=True)
        y = dev * lax.rsqrt(var + GN_EPS) * gnw_ref[:, vs]
        gf = (ga_ref if hh < per_half else gb_ref)[:, hs].astype(F32)
        o_ref[:, vs] = (gf * jax.nn.sigmoid(gf) * y).astype(o_ref.dtype)


def _retention(proj, gn_w, state0, batch, seq, pos0, *, w_sb, wk_ret, wv_ret):
    n = proj.shape[0]
    dk = wk_ret // H_RET
    dv = wv_ret // H_RET
    assert dk == LANES and dv % LANES == 0
    blk = min(seq, RET_CHUNK_MAX)
    assert seq % blk == 0
    nc = seq // blk
    hw = wv_ret // 2
    assert (3 * w_sb) % wk_ret == 0 and (3 * w_sb + 2 * wk_ret) % hw == 0 and H_RET % 2 == 0
    q0 = 3 * w_sb // wk_ret
    k0 = q0 + 1
    v0 = (3 * w_sb + 2 * wk_ret) // hw
    g0 = v0 + 2

    half = dk // 2
    inv = ROPE_BASE ** (-jnp.arange(half, dtype=F32) / half)
    ang = (pos0 + jnp.arange(seq, dtype=F32))[:, None] * inv[None, :]
    cos = jnp.concatenate([jnp.cos(ang), jnp.cos(ang)], axis=1)
    sin = jnp.concatenate([-jnp.sin(ang), jnp.sin(ang)], axis=1)
    log_gamma = jnp.log1p(-jnp.exp2(-5.0 - jnp.arange(H_RET, dtype=F32)))
    dec = jnp.stack([log_gamma, jnp.exp(blk * log_gamma)])

    est = 2 * blk * (2 * wk_ret + 2 * wv_ret) * 2 + 4 * blk * dk * 4 + 5 * H_RET * dk * dv * 4 \
        + 2 * blk * wv_ret * 2 + H_RET * (blk * blk + 2 * blk * LANES) * 4 + 12 * H_RET * blk * dv * 4
    row = lambda b, c: b * nc + c
    return pl.pallas_call(
        functools.partial(_retention_kernel, blk=blk, k_scale=dk ** -0.5, dk=dk, dv=dv),
        grid=(batch, nc),
        in_specs=[
            pl.BlockSpec(memory_space=pltpu.SMEM),
            pl.BlockSpec((blk, wk_ret), lambda b, c: (row(b, c), q0)),
            pl.BlockSpec((blk, wk_ret), lambda b, c: (row(b, c), k0)),
            pl.BlockSpec((blk, hw), lambda b, c: (row(b, c), v0)),
            pl.BlockSpec((blk, hw), lambda b, c: (row(b, c), v0 + 1)),
            pl.BlockSpec((blk, hw), lambda b, c: (row(b, c), g0)),
            pl.BlockSpec((blk, hw), lambda b, c: (row(b, c), g0 + 1)),
            pl.BlockSpec((blk, dk), lambda b, c: (c, 0)),
            pl.BlockSpec((blk, dk), lambda b, c: (c, 0)),
            pl.BlockSpec((1, wv_ret), lambda b, c: (0, 0)),
            pl.BlockSpec((None, H_RET, dk, dv), lambda b, c: (b, 0, 0, 0)),
        ],
        out_specs=[
            pl.BlockSpec((blk, wv_ret), lambda b, c: (row(b, c), 0)),
            pl.BlockSpec((None, H_RET, dk, dv), lambda b, c: (b, 0, 0, 0)),
        ],
        out_shape=[jax.ShapeDtypeStruct((n, wv_ret), BF16),
                   jax.ShapeDtypeStruct((batch, H_RET, dk, dv), F32)],
        scratch_shapes=[pltpu.VMEM((H_RET, dk, dv), F32), pltpu.VMEM((H_RET, blk, blk), F32),
                        pltpu.VMEM((H_RET, blk, LANES), F32), pltpu.VMEM((H_RET, blk, LANES), F32)],
        compiler_params=_cparams(("parallel", "arbitrary"), est),
        name="retention",
    )(dec, proj, proj, proj, proj, proj, proj, cos, sin, gn_w.reshape(1, wv_ret), state0)


def _merge_kernel(osb_ref, or_ref, asb0_ref, asb1_ref, ar0_ref, ar1_ref, h_ref,
                  wsb_ref, wr_ref, wo_ref, out_ref):
    b_sb = jnp.dot(osb_ref[...], wsb_ref[...], preferred_element_type=F32)
    b_r = jnp.dot(or_ref[...], wr_ref[...], preferred_element_type=F32)
    a_sb = jnp.concatenate([asb0_ref[...], asb1_ref[...]], axis=1).astype(F32)
    a_r = jnp.concatenate([ar0_ref[...], ar1_ref[...]], axis=1).astype(F32)
    merged = jax.nn.sigmoid(a_sb) * b_sb + jax.nn.sigmoid(a_r) * b_r
    out_ref[...] = h_ref[...] + jnp.dot(merged.astype(BF16), wo_ref[...], preferred_element_type=F32)


def _merge(o_sb, o_r, proj, h, w_sb_o, w_ret_o, w_out, *, w_sb, wk_ret, wv_ret):
    n, d = h.shape
    tm = _row_tile(n, 512)
    half = d // 2
    a0 = (3 * w_sb + 2 * wk_ret + 2 * wv_ret) // half
    assert (3 * w_sb + 2 * wk_ret + 2 * wv_ret) % half == 0
    gate = lambda t: pl.BlockSpec((tm, half), lambda i: (i, a0 + t))
    full = lambda a: pl.BlockSpec(a.shape, lambda i: (0, 0))
    est = 2 * tm * (w_sb + wv_ret + 2 * d) * 2 + 4 * tm * d * 4 + 2 * (w_sb + wv_ret + d) * d * 2 + 6 * tm * d * 4
    return pl.pallas_call(
        _merge_kernel,
        grid=(n // tm,),
        in_specs=[
            pl.BlockSpec((tm, w_sb), lambda i: (i, 0)),
            pl.BlockSpec((tm, wv_ret), lambda i: (i, 0)),
            gate(0), gate(1), gate(2), gate(3),
            pl.BlockSpec((tm, d), lambda i: (i, 0)),
            full(w_sb_o), full(w_ret_o), full(w_out),
        ],
        out_specs=pl.BlockSpec((tm, d), lambda i: (i, 0)),
        out_shape=jax.ShapeDtypeStruct((n, d), F32),
        compiler_params=_cparams(("parallel",), est),
        name="merge",
    )(o_sb, o_r, proj, proj, proj, proj, h, w_sb_o, w_ret_o, w_out)


def _finish(h, acc, fw_ref, out_ref):
    y = h + acc
    if fw_ref is not None:
        y = _rms_scale(y, fw_ref[...])
    out_ref[...] = y


def _ffn_kernel(h_ref, nw_ref, wg_ref, wu_ref, wd_ref, *rest, final, tf):
    fw_ref = rest[0] if final else None
    out_ref = rest[-1]
    h = h_ref[...]
    hn = _rms_scale(h, nw_ref[...]).astype(BF16)
    acc = None
    for f in range(wg_ref.shape[1] // tf):
        cols = slice(f * tf, (f + 1) * tf)
        g = jnp.dot(hn, wg_ref[:, cols], preferred_element_type=F32)
        u = jnp.dot(hn, wu_ref[:, cols], preferred_element_type=F32)
        act = (g * jax.nn.sigmoid(g) * u).astype(BF16)
        t = jnp.dot(act, wd_ref[cols, :], preferred_element_type=F32)
        acc = t if acc is None else acc + t
    _finish(h, acc, fw_ref, out_ref)


def _ffn(h, norm_w, wg, wu, wd, final_w):
    n, d = h.shape
    ff = wg.shape[1]
    tm = _row_tile(n, 512)
    tf = 2 * LANES
    assert ff % tf == 0
    final = final_w is not None
    resident = lambda a: pl.BlockSpec(a.shape, lambda i: (0, 0), pipeline_mode=pl.Buffered(1))
    in_specs = [
        pl.BlockSpec((tm, d), lambda i: (i, 0)),
        pl.BlockSpec((1, d), lambda i: (0, 0)),
        resident(wg), resident(wu), resident(wd),
    ]
    args = [h, norm_w.reshape(1, d), wg, wu, wd]
    if final:
        in_specs.append(pl.BlockSpec((1, d), lambda i: (0, 0)))
        args.append(final_w.reshape(1, d))
    est = 4 * tm * d * 4 + tm * d * 2 + 3 * d * ff * 2 + 4 * tm * tf * 4 + 2 * tm * d * 4
    return pl.pallas_call(
        functools.partial(_ffn_kernel, final=final, tf=tf),
        grid=(n // tm,),
        in_specs=in_specs,
        out_specs=pl.BlockSpec((tm, d), lambda i: (i, 0)),
        out_shape=jax.ShapeDtypeStruct((n, d), F32),
        compiler_params=_cparams(("parallel",), est),
        name="ffn_dense",
    )(*args)


MOE_TILE = 512


def _interleave_store(dst_ref, lead, x, n_rows):
    groups = x.shape[1] // LANES
    for g in range(groups):
        dst_ref[lead + (pl.ds(g, n_rows, stride=groups), slice(None))] = x[:, g * LANES:(g + 1) * LANES]


def _interleave_load(src_ref, lead, n_rows, groups):
    parts = [src_ref[lead + (pl.ds(g, n_rows, stride=groups), slice(None))] for g in range(groups)]
    return jnp.concatenate(parts, axis=1)


def _moe_route_kernel(h_ref, nw_ref, rw_ref, tri_ref, xs_hbm, pos_ref, w_ref, cnt_ref,
                      x3_ref, run_ref, posv_ref, poss_ref, zero_ref, sem_ref, psem_ref, zsem_ref,
                      *, n_exp, cap, pad_to):
    i = pl.program_id(0)
    nt = pl.num_programs(0)
    tm, d = h_ref.shape
    groups = d // LANES
    slot = lax.rem(i, 2)

    def slot_copies(s):
        return pltpu.make_async_copy(x3_ref.at[s], xs_hbm.at[pl.ds(0, tm * groups), :], sem_ref.at[s])

    def row_copy(s, r, p):
        return pltpu.make_async_copy(x3_ref.at[s, pl.ds(r * groups, groups), :],
                                     xs_hbm.at[pl.ds(p * groups, groups), :], sem_ref.at[s])

    @pl.when(i == 0)
    def _():
        run_ref[...] = jnp.zeros_like(run_ref)
        zero_ref[...] = jnp.zeros_like(zero_ref)

    @pl.when(i >= 2)
    def _():
        slot_copies(slot).wait()
        slot_copies(slot).wait()

    hn = _rms_scale(h_ref[...], nw_ref[...])
    _interleave_store(x3_ref, (slot,), hn, tm)

    er = rw_ref.shape[0] // 2
    nt_dims = (((1,), (1,)), ((), ()))
    hn_hi = hn.astype(BF16)
    hn_lo = (hn - hn_hi.astype(F32)).astype(BF16)
    by_hi = lax.dot_general(rw_ref[...], hn_hi, nt_dims, preferred_element_type=F32)
    logits = by_hi[:er] + by_hi[er:] + lax.dot_general(rw_ref[:er], hn_lo, nt_dims, preferred_element_type=F32)
    eid = lax.broadcasted_iota(jnp.int32, logits.shape, 0).astype(F32)
    logits = jnp.where(eid < n_exp, logits, -jnp.inf)
    m1 = jnp.max(logits, axis=0, keepdims=True)
    i1 = jnp.min(jnp.where(logits == m1, eid, float(er)), axis=0, keepdims=True)
    rest_l = jnp.where(eid == i1, -jnp.inf, logits)
    m2 = jnp.max(rest_l, axis=0, keepdims=True)
    i2 = jnp.min(jnp.where(rest_l == m2, eid, float(er)), axis=0, keepdims=True)
    ex = jnp.exp(m2 - m1)
    srow = lax.broadcasted_iota(jnp.int32, (LANES, tm), 0)
    wslab = jnp.where(srow == 0, 1.0 / (1.0 + ex), jnp.where(srow == 1, ex / (1.0 + ex), 0.0))
    wt = jnp.transpose(wslab)
    w_ref[...] = jnp.concatenate([jnp.broadcast_to(wt[:, 0:1], (tm, LANES)),
                                  jnp.broadcast_to(wt[:, 1:2], (tm, LANES))], axis=1)

    onehot = jnp.where(jnp.logical_or(eid == i1, eid == i2), 1.0, 0.0)
    before = run_ref[...][:, 0:1]
    ranks = jnp.dot(onehot.astype(BF16), tri_ref[...], preferred_element_type=F32) + before
    pos1 = i1 * float(cap) + jnp.sum(jnp.where(eid == i1, ranks, 0.0), axis=0, keepdims=True)
    pos2 = i2 * float(cap) + jnp.sum(jnp.where(eid == i2, ranks, 0.0), axis=0, keepdims=True)
    run = run_ref[...] + jnp.sum(onehot, axis=1, keepdims=True)
    run_ref[...] = run

    posv = jnp.concatenate([pos1, pos2], axis=0).astype(jnp.int32)
    pos_ref[...] = posv
    posv_ref[...] = posv
    to_smem = pltpu.make_async_copy(posv_ref, poss_ref, psem_ref.at[0])
    to_smem.start()
    to_smem.wait()

    def issue(r, carry):
        row_copy(slot, r, poss_ref[0, r]).start()
        row_copy(slot, r, poss_ref[1, r]).start()
        return carry

    lax.fori_loop(0, tm, issue, 0, unroll=8)

    @pl.when(i == nt - 1)
    def _():
        slot_copies(slot).wait()
        slot_copies(slot).wait()

        @pl.when(nt > 1)
        def _():
            slot_copies(1 - slot).wait()
            slot_copies(1 - slot).wait()

        cnt_ref[...] = run
        erow = lax.broadcasted_iota(jnp.int32, run.shape, 0)
        fills = []
        for x in range(n_exp):
            c = (jnp.sum(jnp.where(erow == x, run, 0.0)) * (1.0 / LANES)).astype(jnp.int32)
            gap = lax.div(c + (pad_to - 1), pad_to) * pad_to - c
            start = x * cap + c
            for bit in reversed(range(pad_to.bit_length() - 1)):
                size = 1 << bit
                blk = pltpu.make_async_copy(zero_ref.at[pl.ds(0, size * groups), :],
                                            xs_hbm.at[pl.ds(start * groups, size * groups), :], zsem_ref.at[0])
                has = jnp.bitwise_and(lax.shift_right_logical(gap, bit), 1)
                fills.append((has, blk))
                start = start + has * size
        for has, blk in fills:
            pl.when(has == 1)(blk.start)
        for has, blk in fills:
            pl.when(has == 1)(blk.wait)


def _moe_expert_kernel(te_ref, tb_ref, ok_ref, xs_ref, wg_ref, wu_ref, wd_ref, ys_ref, *, rows, groups):
    t = pl.program_id(0)

    @pl.when(ok_ref[t] == 1)
    def _():
        x = _interleave_load(xs_ref, (), rows, groups).astype(BF16)
        g = jnp.dot(x, wg_ref[...], preferred_element_type=F32)
        u = jnp.dot(x, wu_ref[...], preferred_element_type=F32)
        act = (g * jax.nn.sigmoid(g) * u).astype(BF16)
        _interleave_store(ys_ref, (), jnp.dot(act, wd_ref[...], preferred_element_type=F32), rows)


def _moe_combine_kernel(pos_ref, nxt_ref, h_ref, w_ref, *rest, final):
    fw_ref = rest[0] if final else None
    ys_hbm, out_ref, g_ref, sem_ref = rest[1:] if final else rest
    i = pl.program_id(0)
    nt = pl.num_programs(0)
    tm, d = h_ref.shape
    groups = d // LANES
    slot = lax.rem(i, 2)

    def gather_tile(p_ref, s):
        def issue(r, carry):
            for k in range(TOP_K):
                pltpu.make_async_copy(ys_hbm.at[pl.ds(p_ref[k, r] * groups, groups), :],
                                      g_ref.at[s, k, pl.ds(r * groups, groups), :], sem_ref.at[s, k]).start()
            return carry

        lax.fori_loop(0, tm, issue, 0, unroll=8)

    @pl.when(i == 0)
    def _():
        gather_tile(pos_ref, 0)

    @pl.when(i + 1 < nt)
    def _():
        gather_tile(nxt_ref, 1 - slot)

    for k in range(TOP_K):
        pltpu.make_async_copy(ys_hbm.at[pl.ds(0, tm * groups), :], g_ref.at[slot, k], sem_ref.at[slot, k]).wait()

    w = w_ref[...]
    w1 = jnp.concatenate([w[:, :LANES]] * groups, axis=1)
    w2 = jnp.concatenate([w[:, LANES:]] * groups, axis=1)
    y = (w1 * _interleave_load(g_ref, (slot, 0), tm, groups)
         + w2 * _interleave_load(g_ref, (slot, 1), tm, groups))
    _finish(h_ref[...], y, fw_ref, out_ref)


def _moe(h, norm_w, router_w, wg, wu, wd, final_w):
    n, d = h.shape
    n_exp, _, fp = wg.shape
    groups = d // LANES
    assert d % LANES == 0 and groups == 8, "a row must fill one f32 (8,128) tile to move as one contiguous copy"
    tm = _row_tile(n, 1024)
    nt = n // tm
    tmx = min(MOE_TILE, n)
    assert tmx & (tmx - 1) == 0, "the zero fill writes power-of-two blocks below one tile"
    cap = -(-n // tmx) * tmx
    blocks_per_expert = cap // tmx
    dummy_block = n_exp * blocks_per_expert
    sorted_rows = (n_exp * cap + tmx) * groups
    final = final_w is not None
    er = -(-n_exp // 16) * 16
    rw = jnp.pad(router_w.astype(F32).T, ((0, er - n_exp), (0, 0)))
    rw_hi = rw.astype(BF16)
    rw2 = jnp.concatenate([rw_hi, (rw - rw_hi.astype(F32)).astype(BF16)], axis=0)
    tri = (jnp.arange(tm)[:, None] < jnp.arange(tm)[None, :]).astype(BF16)

    est = 4 * tm * d * 4 + 4 * er * d * 2 + 2 * tm * tm * 2 + 2 * tm * d * 4 + 8 * tm * LANES * 4 + 6 * tm * d * 4
    xs, pos, wts, cnt = pl.pallas_call(
        functools.partial(_moe_route_kernel, n_exp=n_exp, cap=cap, pad_to=tmx),
        grid=(nt,),
        in_specs=[
            pl.BlockSpec((tm, d), lambda i: (i, 0)),
            pl.BlockSpec((1, d), lambda i: (0, 0)),
            pl.BlockSpec((2 * er, d), lambda i: (0, 0)),
            pl.BlockSpec((tm, tm), lambda i: (0, 0)),
        ],
        out_specs=[
            pl.BlockSpec(memory_space=pl.ANY),
            pl.BlockSpec((None, TOP_K, tm), lambda i: (i, 0, 0)),
            pl.BlockSpec((tm, TOP_K * LANES), lambda i: (i, 0)),
            pl.BlockSpec((er, LANES), lambda i: (0, 0)),
        ],
        out_shape=[
            jax.ShapeDtypeStruct((sorted_rows, LANES), F32),
            jax.ShapeDtypeStruct((nt, TOP_K, tm), jnp.int32),
            jax.ShapeDtypeStruct((n, TOP_K * LANES), F32),
            jax.ShapeDtypeStruct((er, LANES), F32),
        ],
        scratch_shapes=[
            pltpu.VMEM((2, tm * groups, LANES), F32), pltpu.VMEM((er, LANES), F32),
            pltpu.VMEM((TOP_K, tm), jnp.int32), pltpu.SMEM((TOP_K, tm), jnp.int32),
            pltpu.VMEM((max(tmx // 2, 1) * groups, LANES), F32),
            pltpu.SemaphoreType.DMA((2,)), pltpu.SemaphoreType.DMA((1,)), pltpu.SemaphoreType.DMA((1,)),
        ],
        compiler_params=_cparams(("arbitrary",), est),
        name="moe_route",
    )(h, norm_w.reshape(1, d), rw2, tri)

    counts = cnt[:n_exp, 0].astype(jnp.int32)
    tiles = (counts + (tmx - 1)) // tmx
    first = jnp.cumsum(tiles) - tiles
    n_steps = 2 * n // tmx + n_exp
    step = jnp.arange(n_steps, dtype=jnp.int32)
    te = jnp.clip(jnp.sum((step[:, None] >= (first + tiles)[None, :]).astype(jnp.int32), axis=1), 0, n_exp - 1)
    ok = (step < jnp.sum(tiles)).astype(jnp.int32)
    tb = jnp.where(ok == 1, te * blocks_per_expert + step - first[te], dummy_block).astype(jnp.int32)

    est = 4 * tmx * d * 4 + 6 * d * fp * 2 + 4 * tmx * fp * 4 + 2 * tmx * d * 4
    sorted_spec = pl.BlockSpec((tmx * groups, LANES), lambda t, te, tb, ok: (tb[t], 0))
    ys = pl.pallas_call(
        functools.partial(_moe_expert_kernel, rows=tmx, groups=groups),
        grid_spec=pltpu.PrefetchScalarGridSpec(
            num_scalar_prefetch=3,
            grid=(n_steps,),
            in_specs=[
                sorted_spec,
                pl.BlockSpec((None, d, fp), lambda t, te, tb, ok: (te[t], 0, 0)),
                pl.BlockSpec((None, d, fp), lambda t, te, tb, ok: (te[t], 0, 0)),
                pl.BlockSpec((None, fp, d), lambda t, te, tb, ok: (te[t], 0, 0)),
            ],
            out_specs=sorted_spec,
        ),
        out_shape=jax.ShapeDtypeStruct((sorted_rows, LANES), F32),
        compiler_params=_cparams(("arbitrary",), est),
        name="moe_experts",
    )(te, tb, ok, xs, wg, wu, wd)

    in_specs = [
        pl.BlockSpec((None, TOP_K, tm), lambda i: (i, 0, 0), memory_space=pltpu.SMEM),
        pl.BlockSpec((None, TOP_K, tm), lambda i: (jnp.minimum(i + 1, nt - 1), 0, 0), memory_space=pltpu.SMEM),
        pl.BlockSpec((tm, d), lambda i: (i, 0)),
        pl.BlockSpec((tm, TOP_K * LANES), lambda i: (i, 0)),
    ]
    args = [pos, pos, h, wts]
    if final:
        in_specs.append(pl.BlockSpec((1, d), lambda i: (0, 0)))
        args.append(final_w.reshape(1, d))
    in_specs.append(pl.BlockSpec(memory_space=pl.ANY))
    args.append(ys)
    est = 4 * tm * d * 4 + 2 * tm * TOP_K * LANES * 4 + 2 * TOP_K * tm * d * 4 + 4 * tm * d * 4
    return pl.pallas_call(
        functools.partial(_moe_combine_kernel, final=final),
        grid=(nt,),
        in_specs=in_specs,
        out_specs=pl.BlockSpec((tm, d), lambda i: (i, 0)),
        out_shape=jax.ShapeDtypeStruct((n, d), F32),
        scratch_shapes=[pltpu.VMEM((2, TOP_K, tm * groups, LANES), F32), pltpu.SemaphoreType.DMA((2, TOP_K))],
        compiler_params=_cparams(("arbitrary",), est),
        name="moe_combine",
    )(*args)


def _run_group(x, pos0, caches, states, weights):
    (norm_mix_w, w_in, w_sb_o, w_ret_o, gn_ret_w, w_out, norm_ffn_w, ffn_w_gate, ffn_w_up,
     ffn_w_down, router_w, moe_w_gate, moe_w_up, moe_w_down, norm_final_w) = weights
    batch, seq, d = x.shape
    depth = len(w_in)
    w_sb = w_sb_o[0].shape[0]
    wv_ret = w_ret_o[0].shape[0]
    d_in = w_in[0].shape[1]
    wk_ret = (d_in - 3 * w_sb - 2 * wv_ret - 2 * d) // 2
    dims = dict(w_sb=w_sb, wk_ret=wk_ret, wv_ret=wv_ret)
    dk, dv = wk_ret // H_RET, wv_ret // H_RET
    hd = w_sb // H_SB

    h = x.reshape(batch * seq, d)
    kv = None
    finals = []
    if caches is not None:
        kc = caches[0].reshape(depth * batch, -1, hd)
        vc = caches[1].reshape(depth * batch, -1, hd)
    for l in range(depth):
        proj, k_all, v_all = _in_projection(h, norm_mix_w[l], w_in[l], l, depth, kv, w_sb=w_sb)
        kv = (k_all, v_all)
        if caches is None:
            o_sb = _sb_attention_prompt(proj, batch, seq, w_sb=w_sb)
            s0 = jnp.zeros((batch, H_RET, dk, dv), F32)
        else:
            o_sb = _sb_attention_sample(proj, kc, vc, l, batch, seq, w_sb=w_sb)
            s0 = states[l].astype(F32)
        o_r, s_fin = _retention(proj, gn_ret_w[l], s0, batch, seq, pos0, **dims)
        finals.append(s_fin)
        h = _merge(o_sb, o_r, proj, h, w_sb_o[l], w_ret_o[l], w_out[l], **dims)
        fw = norm_final_w if l == depth - 1 else None
        if l % 2 == 0:
            h = _ffn(h, norm_ffn_w[l], ffn_w_gate[l // 2], ffn_w_up[l // 2], ffn_w_down[l // 2], fw)
        else:
            h = _moe(h, norm_ffn_w[l], router_w[l // 2], moe_w_gate[l // 2], moe_w_up[l // 2],
                     moe_w_down[l // 2], fw)
    y = h.reshape(batch, seq, d)
    new_k = kv[0].reshape(depth, batch, seq, H_SB, hd)
    new_v = kv[1].reshape(depth, batch, seq, H_SB, hd)
    return y, new_k, new_v, jnp.stack(finals)


def kernel(x_prompt, x_sample, cache_sb_k, cache_sb_v, state_ret, norm_mix_w, w_in, w_sb_o, w_ret_o,
           gn_ret_w, w_out, norm_ffn_w, ffn_w_gate, ffn_w_up, ffn_w_down, router_w, moe_w_gate,
           moe_w_up, moe_w_down, norm_final_w):
    per_layer = lambda w: [w[l].astype(BF16) for l in range(w.shape[0])]
    weights = (
        norm_mix_w, per_layer(w_in), per_layer(w_sb_o), per_layer(w_ret_o), gn_ret_w,
        per_layer(w_out), norm_ffn_w, per_layer(ffn_w_gate), per_layer(ffn_w_up),
        per_layer(ffn_w_down), router_w,
        per_layer(moe_w_gate), per_layer(moe_w_up), per_layer(moe_w_down), norm_final_w,
    )
    past = cache_sb_k.shape[2]
    y_p, k_p, v_p, r_p = _run_group(x_prompt, 0.0, None, None, weights)
    y_s, k_s, v_s, r_s = _run_group(x_sample, float(past), (cache_sb_k, cache_sb_v), state_ret, weights)
    return (y_p, y_s, k_p, v_p, r_p, k_s, v_s, r_s)
```

```python
import functools

import jax
import jax.numpy as jnp
from jax import lax
from jax.experimental import pallas as pl
from jax.experimental.pallas import tpu as pltpu

F32 = jnp.float32
BF16 = jnp.bfloat16

H_SB = 4
H_RET = 4
TOP_K = 2
ROPE_BASE = 10000.0
EPS = 1e-6
GN_EPS = 1e-5
RET_CHUNK_MAX = 256

LANES = 128
V7X_VMEM_BYTES = 64 * 1024 * 1024
VMEM_CAP_BYTES = V7X_VMEM_BYTES - 8 * 1024 * 1024

SB_EXIT_MASS = 104.0
SB_BLOCK = 256


def _cparams(sem, est_bytes):
    limit = int(min(VMEM_CAP_BYTES, max(32 * 1024 * 1024, 2 * est_bytes)))
    return pltpu.CompilerParams(dimension_semantics=sem, vmem_limit_bytes=limit)


def _row_tile(n, want):
    t = min(n, want)
    assert n % t == 0, (n, t)
    return t


def _rms_scale(x, w):
    ms = jnp.mean(x * x, axis=-1, keepdims=True)
    return x * lax.rsqrt(ms + EPS) * w


def _inproj_kernel(x_ref, nw_ref, w_ref, *rest, q_scale, n_alias, tm, tn):
    proj_ref, k_ref, v_ref = rest[n_alias:]
    hn = _rms_scale(x_ref[...], nw_ref[...]).astype(BF16)

    def store_heads(dst_ref, acc):
        for hh in range(H_SB):
            dst_ref[pl.ds(hh, tm, stride=H_SB), :] = acc[:, hh * LANES:(hh + 1) * LANES]

    for j in range(w_ref.shape[1] // tn):
        cols = slice(j * tn, (j + 1) * tn)
        acc = jnp.dot(hn, w_ref[:, cols], preferred_element_type=F32)
        if j == 1:
            store_heads(k_ref, acc)
        if j == 2:
            store_heads(v_ref, acc)
        proj_ref[:, cols] = (acc * q_scale if j == 0 else acc).astype(BF16)


def _in_projection(x, norm_w, w_bf, layer, depth, kv_prev, *, w_sb):
    n, d = x.shape
    d_in = w_bf.shape[1]
    tn = w_sb
    assert d_in % tn == 0
    tm = _row_tile(n, 512)
    hd = w_sb // H_SB
    assert hd == LANES
    nm = n // tm
    kv_shape = jax.ShapeDtypeStruct((depth * n * H_SB, hd), F32)
    kv_spec = pl.BlockSpec((tm * H_SB, hd), lambda i: (layer * nm + i, 0))
    in_specs = [
        pl.BlockSpec((tm, d), lambda i: (i, 0)),
        pl.BlockSpec((1, d), lambda i: (0, 0)),
        pl.BlockSpec((d, d_in), lambda i: (0, 0), pipeline_mode=pl.Buffered(1)),
    ]
    args = [x, norm_w.reshape(1, d), w_bf]
    aliases = {}
    n_alias = 0
    if kv_prev is not None:
        in_specs += [pl.BlockSpec(memory_space=pl.ANY), pl.BlockSpec(memory_space=pl.ANY)]
        args += [kv_prev[0], kv_prev[1]]
        aliases = {3: 1, 4: 2}
        n_alias = 2
    est = 2 * tm * d * 4 + tm * d * 2 + d * d_in * 2 + 2 * tm * d_in * 2 + 4 * tm * w_sb * 4 + 4 * tm * tn * 4
    proj, k_all, v_all = pl.pallas_call(
        functools.partial(_inproj_kernel, q_scale=hd ** -0.5, n_alias=n_alias, tm=tm, tn=tn),
        grid=(nm,),
        in_specs=in_specs,
        out_specs=[pl.BlockSpec((tm, d_in), lambda i: (i, 0)), kv_spec, kv_spec],
        out_shape=[jax.ShapeDtypeStruct((n, d_in), BF16), kv_shape, kv_shape],
        input_output_aliases=aliases,
        compiler_params=_cparams(("parallel",), est),
        name=f"in_projection_l{layer}",
    )(*args)
    return proj, k_all, v_all


def _cumsum_matrix():
    j = jnp.arange(LANES)[:, None]
    s = jnp.arange(LANES)[None, :]
    tri = (j >= s).astype(BF16)
    m = jnp.concatenate([tri, jnp.ones((LANES, LANES), BF16)], axis=1)
    return jnp.concatenate([m, m], axis=0)


def _sb_step_phases(q, chunks, r, cum):
    zs = [lax.dot_general(q, k, (((1,), (1,)), ((), ())), preferred_element_type=F32)
          for k, _, _, _ in chunks]
    yield
    groups = []
    for (k, _, masks, live), z in zip(chunks, zs):
        for g in range(k.shape[0] // LANES):
            zg = z[:, g * LANES:(g + 1) * LANES]
            sp = jnp.maximum(zg, 0.0) + jnp.log(1.0 + jnp.exp(-jnp.abs(zg)))
            m = None if masks is None else masks[g]
            if m is not None:
                sp = jnp.where(m, sp, 0.0)
            if live is not None:
                sp = jnp.where(live, sp, 0.0)
            hi = sp.astype(BF16)
            lo = (sp - hi.astype(F32)).astype(BF16)
            groups.append([zg, jnp.concatenate([hi, lo], axis=1), m, live])
    yield
    for grp in groups:
        grp[1] = jnp.dot(grp[1], cum, preferred_element_type=F32)
    yield
    weights = [None] * len(groups)
    for idx in reversed(range(len(groups))):
        zg, ct, m, live = groups[idx]
        a = jnp.exp(zg - ct[:, :LANES] - r)
        if m is not None:
            a = jnp.where(m, a, 0.0)
        if live is not None:
            a = jnp.where(live, a, 0.0)
        weights[idx] = a.astype(BF16)
        r = r + ct[:, LANES:]
    yield
    pv = None
    idx = 0
    for _, v, _, _ in chunks:
        n_g = v.shape[0] // LANES
        a = weights[idx] if n_g == 1 else jnp.concatenate(weights[idx:idx + n_g], axis=1)
        t = jnp.dot(a, v, preferred_element_type=F32)
        pv = t if pv is None else pv + t
        idx += n_g
    return pv, r


def _lockstep(gens):
    results = [None] * len(gens)
    running = list(range(len(gens)))
    while running:
        still = []
        for t in running:
            try:
                next(gens[t])
                still.append(t)
            except StopIteration as stop:
                results[t] = stop.value
        running = still
    return results


def _sb_step(q, chunks, r, cum):
    return _lockstep([_sb_step_phases(q, chunks, r, cum)])[0]


def _sb_prompt_kernel(q_ref, k_ref, v_ref, cum_ref, o_ref, r_ref, acc_ref, *, qb, heads):
    i = pl.program_id(2)
    cum = cum_ref[...]
    row = lax.broadcasted_iota(jnp.int32, (qb, LANES), 0)
    col = lax.broadcasted_iota(jnp.int32, (qb, LANES), 1)
    causal = [(col + g * LANES) < row for g in range(qb // LANES)]
    has_prev = i > 0
    prev0 = pl.multiple_of(jnp.maximum(i - 1, 0) * qb, qb)
    diag0 = pl.multiple_of(i * qb, qb)
    zero = jnp.zeros((qb, LANES), F32)

    steps = []
    for hh in range(heads):
        cs = slice(hh * LANES, (hh + 1) * LANES)
        chunks = [
            (k_ref[pl.ds(prev0, qb), cs], v_ref[pl.ds(prev0, qb), cs], None, has_prev),
            (k_ref[pl.ds(diag0, qb), cs], v_ref[pl.ds(diag0, qb), cs], causal, None),
        ]
        steps.append(_sb_step_phases(q_ref[:, cs], chunks, zero, cum))
    for hh, (pv, r) in enumerate(_lockstep(steps)):
        acc_ref[hh] = pv
        r_ref[hh] = r

    for hh in range(heads):
        cs = slice(hh * LANES, (hh + 1) * LANES)

        def cond(j, hh=hh):
            return jnp.logical_and(j >= 0, jnp.min(r_ref[hh]) < SB_EXIT_MASS)

        def body(j, hh=hh, cs=cs):
            start = pl.multiple_of(j * qb, qb)
            chunks = [(k_ref[pl.ds(start, qb), cs], v_ref[pl.ds(start, qb), cs], None, None)]
            pv, r = _sb_step(q_ref[:, cs], chunks, r_ref[hh], cum)
            acc_ref[hh] += pv
            r_ref[hh] = r
            return j - 1

        lax.while_loop(cond, body, i - 2)

    for hh in range(heads):
        o_ref[:, hh * LANES:(hh + 1) * LANES] = acc_ref[hh].astype(o_ref.dtype)


def _sb_attention_prompt(proj, batch, seq, *, w_sb):
    n = proj.shape[0]
    hd = w_sb // H_SB
    assert hd == LANES
    qb = SB_BLOCK
    assert seq % qb == 0
    nq = seq // qb
    heads = H_SB
    wb = heads * hd
    kcol = w_sb // wb
    vcol = 2 * w_sb // wb
    est = 4 * qb * wb * 2 + 2 * seq * wb * 2 + 4 * heads * qb * LANES * 4 + 40 * qb * LANES * 4 * heads
    resident = lambda c0: pl.BlockSpec((seq, wb), lambda b, h, i: (b, c0 + h), pipeline_mode=pl.Buffered(1))
    return pl.pallas_call(
        functools.partial(_sb_prompt_kernel, qb=qb, heads=heads),
        grid=(batch, H_SB // heads, nq),
        in_specs=[
            pl.BlockSpec((qb, wb), lambda b, h, i: (b * nq + i, h)),
            resident(kcol),
            resident(vcol),
            pl.BlockSpec((2 * LANES, 2 * LANES), lambda b, h, i: (0, 0)),
        ],
        out_specs=pl.BlockSpec((qb, wb), lambda b, h, i: (b * nq + i, h)),
        out_shape=jax.ShapeDtypeStruct((n, w_sb), BF16),
        scratch_shapes=[pltpu.VMEM((heads, qb, LANES), F32), pltpu.VMEM((heads, qb, LANES), F32)],
        compiler_params=_cparams(("parallel", "parallel", "arbitrary"), est),
        name="sb_attention_prompt",
    )(proj, proj, proj, _cumsum_matrix())


def _sb_sample_kernel(q_ref, kn_ref, vn_ref, kc_hbm, vc_hbm, cum_ref, o_ref, r_ref, acc_ref, buf_ref, sem_ref,
                      *, tq, past, cb, stream0, n_streams):
    b = pl.program_id(0)
    cum = cum_ref[...]
    row = lax.broadcasted_iota(jnp.int32, (tq, LANES), 0)
    col = lax.broadcasted_iota(jnp.int32, (tq, LANES), 1)
    zero = jnp.zeros((tq, LANES), F32)
    rows = cb * H_SB
    latest = past - cb

    def chunk_copy(stream, start, slot, which):
        src = (kc_hbm, vc_hbm)[which].at[stream0 + stream, pl.ds(start * H_SB, rows), :]
        return pltpu.make_async_copy(src, buf_ref.at[slot, which], sem_ref.at[slot, which])

    def cached(slot, which, hh):
        return buf_ref[slot, which, pl.ds(hh, cb, stride=H_SB), :].astype(BF16)

    slot = lax.rem(b, 2)

    @pl.when(b == 0)
    def _():
        for which in range(2):
            chunk_copy(0, latest, 0, which).start()

    for which in range(2):
        chunk_copy(b, latest, slot, which).wait()

    @pl.when(b + 1 < n_streams)
    def _():
        for which in range(2):
            chunk_copy(b + 1, latest, 1 - slot, which).start()

    steps = []
    for hh in range(H_SB):
        cs = slice(hh * LANES, (hh + 1) * LANES)
        chunks = [
            (cached(slot, 0, hh), cached(slot, 1, hh), None, None),
            (kn_ref[:, cs], vn_ref[:, cs], [col < row], None),
        ]
        steps.append(_sb_step_phases(q_ref[:, cs], chunks, zero, cum))
    for hh, (pv, r) in enumerate(_lockstep(steps)):
        acc_ref[hh] = pv
        r_ref[hh] = r

    def cond(j):
        return jnp.logical_and(j >= 0, jnp.min(r_ref[...]) < SB_EXIT_MASS)

    def body(j):
        for which in range(2):
            chunk_copy(b, j * cb, 2, which).start()
        for which in range(2):
            chunk_copy(b, j * cb, 2, which).wait()
        steps = []
        for hh in range(H_SB):
            cs = slice(hh * LANES, (hh + 1) * LANES)
            chunks = [(cached(2, 0, hh), cached(2, 1, hh), None, None)]
            steps.append(_sb_step_phases(q_ref[:, cs], chunks, r_ref[hh], cum))
        for hh, (pv, r) in enumerate(_lockstep(steps)):
            acc_ref[hh] += pv
            r_ref[hh] = r
        return j - 1

    lax.while_loop(cond, body, past // cb - 2)

    for hh in range(H_SB):
        o_ref[:, hh * LANES:(hh + 1) * LANES] = acc_ref[hh].astype(o_ref.dtype)


def _sb_attention_sample(proj, k_cache, v_cache, layer, batch, tq, *, w_sb):
    n = proj.shape[0]
    hd = w_sb // H_SB
    past = k_cache.shape[1] // H_SB
    cb = SB_BLOCK if past % SB_BLOCK == 0 else LANES
    assert hd == LANES and past % cb == 0 and tq <= LANES
    kv_new = proj[:, w_sb:3 * w_sb].reshape(batch, tq, 2 * w_sb)
    kv_new = jnp.pad(kv_new, ((0, 0), (0, LANES - tq), (0, 0)))
    est = 6 * cb * w_sb * 4 + 8 * LANES * w_sb * 2 + 40 * H_SB * max(tq, 8) * LANES * 4
    return pl.pallas_call(
        functools.partial(_sb_sample_kernel, tq=tq, past=past, cb=cb, stream0=layer * batch, n_streams=batch),
        grid=(batch,),
        in_specs=[
            pl.BlockSpec((tq, w_sb), lambda b: (b, 0)),
            pl.BlockSpec((None, LANES, w_sb), lambda b: (b, 0, 0)),
            pl.BlockSpec((None, LANES, w_sb), lambda b: (b, 0, 1)),
            pl.BlockSpec(memory_space=pl.ANY),
            pl.BlockSpec(memory_space=pl.ANY),
            pl.BlockSpec((2 * LANES, 2 * LANES), lambda b: (0, 0)),
        ],
        out_specs=pl.BlockSpec((tq, w_sb), lambda b: (b, 0)),
        out_shape=jax.ShapeDtypeStruct((n, w_sb), BF16),
        scratch_shapes=[pltpu.VMEM((H_SB, tq, LANES), F32), pltpu.VMEM((H_SB, tq, LANES), F32),
                        pltpu.VMEM((3, 2, cb * H_SB, hd), F32), pltpu.SemaphoreType.DMA((3, 2))],
        compiler_params=_cparams(("arbitrary",), est),
        name="sb_attention_sample",
    )(proj, kv_new, kv_new, k_cache, v_cache, _cumsum_matrix())


def _retention_kernel(dec_ref, q_ref, k_ref, va_ref, vb_ref, ga_ref, gb_ref, cos_ref, sin_ref, gnw_ref,
                      s0_ref, o_ref, sout_ref, s_ref, d_ref, qd_ref, kd_ref, *, blk, k_scale, dk, dv):
    c = pl.program_id(1)
    per_half = H_RET // 2

    @pl.when(c == 0)
    def _():
        s_ref[...] = s0_ref[...]
        n = lax.broadcasted_iota(jnp.int32, (blk, blk), 0)
        m = lax.broadcasted_iota(jnp.int32, (blk, blk), 1)
        causal = n >= m
        diff = jnp.where(causal, n - m, 0).astype(F32)
        rows = lax.broadcasted_iota(jnp.int32, (blk, LANES), 0).astype(F32)
        for hh in range(H_RET):
            lg = dec_ref[0, hh]
            d_ref[hh] = jnp.where(causal, jnp.exp(diff * lg) * k_scale, 0.0)
            qd_ref[hh] = jnp.exp((rows + 1.0) * lg)
            kd_ref[hh] = jnp.exp((blk - 1.0 - rows) * lg) * k_scale

    cos = cos_ref[...]
    sin = sin_ref[...]
    half = dk // 2
    for hh in range(H_RET):
        ks = slice(hh * dk, (hh + 1) * dk)
        vs = slice(hh * dv, (hh + 1) * dv)
        qf = q_ref[:, ks].astype(F32)
        kf = k_ref[:, ks].astype(F32)
        qr = qf * cos + pltpu.roll(qf, half, 1) * sin
        kr = kf * cos + pltpu.roll(kf, half, 1) * sin
        hs = slice((hh % per_half) * dv, (hh % per_half + 1) * dv)
        v = (va_ref if hh < per_half else vb_ref)[:, hs]

        scores = lax.dot_general(qr.astype(BF16), kr.astype(BF16), (((1,), (1,)), ((), ())),
                                 preferred_element_type=F32) * d_ref[hh]
        inner = jnp.dot(scores.astype(BF16), v, preferred_element_type=F32)
        s = s_ref[hh]
        cross = jnp.dot((qr * qd_ref[hh]).astype(BF16), s.astype(BF16), preferred_element_type=F32)
        o = inner + cross
        kv = lax.dot_general((kr * kd_ref[hh]).astype(BF16), v, (((0,), (0,)), ((), ())),
                             preferred_element_type=F32)
        s_new = dec_ref[1, hh] * s + kv
        s_ref[hh] = s_new
        sout_ref[hh] = s_new

        mu = jnp.mean(o, axis=-1, keepdims=True)
        dev = o - mu
        var = jnp.mean(dev * dev, axis=-1, keepdims=True)
        y = dev * lax.rsqrt(var + GN_EPS) * gnw_ref[:, vs]
        gf = (ga_ref if hh < per_half else gb_ref)[:, hs].astype(F32)
        o_ref[:, vs] = (gf * jax.nn.sigmoid(gf) * y).astype(o_ref.dtype)


def _retention(proj, gn_w, state0, batch, seq, pos0, *, w_sb, wk_ret, wv_ret):
    n = proj.shape[0]
    dk = wk_ret // H_RET
    dv = wv_ret // H_RET
    assert dk == LANES and dv % LANES == 0
    blk = min(seq, RET_CHUNK_MAX)
    assert seq % blk == 0
    nc = seq // blk
    hw = wv_ret // 2
    assert (3 * w_sb) % wk_ret == 0 and (3 * w_sb + 2 * wk_ret) % hw == 0 and H_RET % 2 == 0
    q0 = 3 * w_sb // wk_ret
    k0 = q0 + 1
    v0 = (3 * w_sb + 2 * wk_ret) // hw
    g0 = v0 + 2

    half = dk // 2
    inv = ROPE_BASE ** (-jnp.arange(half, dtype=F32) / half)
    ang = (pos0 + jnp.arange(seq, dtype=F32))[:, None] * inv[None, :]
    cos = jnp.concatenate([jnp.cos(ang), jnp.cos(ang)], axis=1)
    sin = jnp.concatenate([-jnp.sin(ang), jnp.sin(ang)], axis=1)
    log_gamma = jnp.log1p(-jnp.exp2(-5.0 - jnp.arange(H_RET, dtype=F32)))
    dec = jnp.stack([log_gamma, jnp.exp(blk * log_gamma)])

    est = 2 * blk * (2 * wk_ret + 2 * wv_ret) * 2 + 4 * blk * dk * 4 + 5 * H_RET * dk * dv * 4 \
        + 2 * blk * wv_ret * 2 + H_RET * (blk * blk + 2 * blk * LANES) * 4 + 12 * H_RET * blk * dv * 4
    row = lambda b, c: b * nc + c
    return pl.pallas_call(
        functools.partial(_retention_kernel, blk=blk, k_scale=dk ** -0.5, dk=dk, dv=dv),
        grid=(batch, nc),
        in_specs=[
            pl.BlockSpec(memory_space=pltpu.SMEM),
            pl.BlockSpec((blk, wk_ret), lambda b, c: (row(b, c), q0)),
            pl.BlockSpec((blk, wk_ret), lambda b, c: (row(b, c), k0)),
            pl.BlockSpec((blk, hw), lambda b, c: (row(b, c), v0)),
            pl.BlockSpec((blk, hw), lambda b, c: (row(b, c), v0 + 1)),
            pl.BlockSpec((blk, hw), lambda b, c: (row(b, c), g0)),
            pl.BlockSpec((blk, hw), lambda b, c: (row(b, c), g0 + 1)),
            pl.BlockSpec((blk, dk), lambda b, c: (c, 0)),
            pl.BlockSpec((blk, dk), lambda b, c: (c, 0)),
            pl.BlockSpec((1, wv_ret), lambda b, c: (0, 0)),
            pl.BlockSpec((None, H_RET, dk, dv), lambda b, c: (b, 0, 0, 0)),
        ],
        out_specs=[
            pl.BlockSpec((blk, wv_ret), lambda b, c: (row(b, c), 0)),
            pl.BlockSpec((None, H_RET, dk, dv), lambda b, c: (b, 0, 0, 0)),
        ],
        out_shape=[jax.ShapeDtypeStruct((n, wv_ret), BF16),
                   jax.ShapeDtypeStruct((batch, H_RET, dk, dv), F32)],
        scratch_shapes=[pltpu.VMEM((H_RET, dk, dv), F32), pltpu.VMEM((H_RET, blk, blk), F32),
                        pltpu.VMEM((H_RET, blk, LANES), F32), pltpu.VMEM((H_RET, blk, LANES), F32)],
        compiler_params=_cparams(("parallel", "arbitrary"), est),
        name="retention",
    )(dec, proj, proj, proj, proj, proj, proj, cos, sin, gn_w.reshape(1, wv_ret), state0)


def _merge_kernel(osb_ref, or_ref, asb0_ref, asb1_ref, ar0_ref, ar1_ref, h_ref,
                  wsb_ref, wr_ref, wo_ref, out_ref):
    b_sb = jnp.dot(osb_ref[...], wsb_ref[...], preferred_element_type=F32)
    b_r = jnp.dot(or_ref[...], wr_ref[...], preferred_element_type=F32)
    a_sb = jnp.concatenate([asb0_ref[...], asb1_ref[...]], axis=1).astype(F32)
    a_r = jnp.concatenate([ar0_ref[...], ar1_ref[...]], axis=1).astype(F32)
    merged = jax.nn.sigmoid(a_sb) * b_sb + jax.nn.sigmoid(a_r) * b_r
    out_ref[...] = h_ref[...] + jnp.dot(merged.astype(BF16), wo_ref[...], preferred_element_type=F32)


def _merge(o_sb, o_r, proj, h, w_sb_o, w_ret_o, w_out, *, w_sb, wk_ret, wv_ret):
    n, d = h.shape
    tm = _row_tile(n, 512)
    half = d // 2
    a0 = (3 * w_sb + 2 * wk_ret + 2 * wv_ret) // half
    assert (3 * w_sb + 2 * wk_ret + 2 * wv_ret) % half == 0
    gate = lambda t: pl.BlockSpec((tm, half), lambda i: (i, a0 + t))
    full = lambda a: pl.BlockSpec(a.shape, lambda i: (0, 0))
    est = 2 * tm * (w_sb + wv_ret + 2 * d) * 2 + 4 * tm * d * 4 + 2 * (w_sb + wv_ret + d) * d * 2 + 6 * tm * d * 4
    return pl.pallas_call(
        _merge_kernel,
        grid=(n // tm,),
        in_specs=[
            pl.BlockSpec((tm, w_sb), lambda i: (i, 0)),
            pl.BlockSpec((tm, wv_ret), lambda i: (i, 0)),
            gate(0), gate(1), gate(2), gate(3),
            pl.BlockSpec((tm, d), lambda i: (i, 0)),
            full(w_sb_o), full(w_ret_o), full(w_out),
        ],
        out_specs=pl.BlockSpec((tm, d), lambda i: (i, 0)),
        out_shape=jax.ShapeDtypeStruct((n, d), F32),
        compiler_params=_cparams(("parallel",), est),
        name="merge",
    )(o_sb, o_r, proj, proj, proj, proj, h, w_sb_o, w_ret_o, w_out)


def _finish(h, acc, fw_ref, out_ref):
    y = h + acc
    if fw_ref is not None:
        y = _rms_scale(y, fw_ref[...])
    out_ref[...] = y


def _ffn_kernel(h_ref, nw_ref, wg_ref, wu_ref, wd_ref, *rest, final, tf):
    fw_ref = rest[0] if final else None
    out_ref = rest[-1]
    h = h_ref[...]
    hn = _rms_scale(h, nw_ref[...]).astype(BF16)
    acc = None
    for f in range(wg_ref.shape[1] // tf):
        cols = slice(f * tf, (f + 1) * tf)
        g = jnp.dot(hn, wg_ref[:, cols], preferred_element_type=F32)
        u = jnp.dot(hn, wu_ref[:, cols], preferred_element_type=F32)
        act = (g * jax.nn.sigmoid(g) * u).astype(BF16)
        t = jnp.dot(act, wd_ref[cols, :], preferred_element_type=F32)
        acc = t if acc is None else acc + t
    _finish(h, acc, fw_ref, out_ref)


def _ffn(h, norm_w, wg, wu, wd, final_w):
    n, d = h.shape
    ff = wg.shape[1]
    tm = _row_tile(n, 512)
    tf = 2 * LANES
    assert ff % tf == 0
    final = final_w is not None
    resident = lambda a: pl.BlockSpec(a.shape, lambda i: (0, 0), pipeline_mode=pl.Buffered(1))
    in_specs = [
        pl.BlockSpec((tm, d), lambda i: (i, 0)),
        pl.BlockSpec((1, d), lambda i: (0, 0)),
        resident(wg), resident(wu), resident(wd),
    ]
    args = [h, norm_w.reshape(1, d), wg, wu, wd]
    if final:
        in_specs.append(pl.BlockSpec((1, d), lambda i: (0, 0)))
        args.append(final_w.reshape(1, d))
    est = 4 * tm * d * 4 + tm * d * 2 + 3 * d * ff * 2 + 4 * tm * tf * 4 + 2 * tm * d * 4
    return pl.pallas_call(
        functools.partial(_ffn_kernel, final=final, tf=tf),
        grid=(n // tm,),
        in_specs=in_specs,
        out_specs=pl.BlockSpec((tm, d), lambda i: (i, 0)),
        out_shape=jax.ShapeDtypeStruct((n, d), F32),
        compiler_params=_cparams(("parallel",), est),
        name="ffn_dense",
    )(*args)


MOE_TILE = 512


def _interleave_store(dst_ref, lead, x, n_rows):
    groups = x.shape[1] // LANES
    for g in range(groups):
        dst_ref[lead + (pl.ds(g, n_rows, stride=groups), slice(None))] = x[:, g * LANES:(g + 1) * LANES]


def _interleave_load(src_ref, lead, n_rows, groups):
    parts = [src_ref[lead + (pl.ds(g, n_rows, stride=groups), slice(None))] for g in range(groups)]
    return jnp.concatenate(parts, axis=1)


def _moe_route_kernel(h_ref, nw_ref, rw_ref, tri_ref, xs_hbm, pos_ref, w_ref, cnt_ref,
                      x3_ref, run_ref, posv_ref, poss_ref, zero_ref, sem_ref, psem_ref, zsem_ref,
                      *, n_exp, cap, pad_to):
    i = pl.program_id(0)
    nt = pl.num_programs(0)
    tm, d = h_ref.shape
    groups = d // LANES
    slot = lax.rem(i, 2)

    def slot_copies(s):
        return pltpu.make_async_copy(x3_ref.at[s], xs_hbm.at[pl.ds(0, tm * groups), :], sem_ref.at[s])

    def row_copy(s, r, p):
        return pltpu.make_async_copy(x3_ref.at[s, pl.ds(r * groups, groups), :],
                                     xs_hbm.at[pl.ds(p * groups, groups), :], sem_ref.at[s])

    @pl.when(i == 0)
    def _():
        run_ref[...] = jnp.zeros_like(run_ref)
        zero_ref[...] = jnp.zeros_like(zero_ref)

    @pl.when(i >= 2)
    def _():
        slot_copies(slot).wait()
        slot_copies(slot).wait()

    hn = _rms_scale(h_ref[...], nw_ref[...])
    _interleave_store(x3_ref, (slot,), hn, tm)

    er = rw_ref.shape[0] // 2
    nt_dims = (((1,), (1,)), ((), ()))
    hn_hi = hn.astype(BF16)
    hn_lo = (hn - hn_hi.astype(F32)).astype(BF16)
    by_hi = lax.dot_general(rw_ref[...], hn_hi, nt_dims, preferred_element_type=F32)
    logits = by_hi[:er] + by_hi[er:] + lax.dot_general(rw_ref[:er], hn_lo, nt_dims, preferred_element_type=F32)
    eid = lax.broadcasted_iota(jnp.int32, logits.shape, 0).astype(F32)
    logits = jnp.where(eid < n_exp, logits, -jnp.inf)
    m1 = jnp.max(logits, axis=0, keepdims=True)
    i1 = jnp.min(jnp.where(logits == m1, eid, float(er)), axis=0, keepdims=True)
    rest_l = jnp.where(eid == i1, -jnp.inf, logits)
    m2 = jnp.max(rest_l, axis=0, keepdims=True)
    i2 = jnp.min(jnp.where(rest_l == m2, eid, float(er)), axis=0, keepdims=True)
    ex = jnp.exp(m2 - m1)
    srow = lax.broadcasted_iota(jnp.int32, (LANES, tm), 0)
    wslab = jnp.where(srow == 0, 1.0 / (1.0 + ex), jnp.where(srow == 1, ex / (1.0 + ex), 0.0))
    wt = jnp.transpose(wslab)
    w_ref[...] = jnp.concatenate([jnp.broadcast_to(wt[:, 0:1], (tm, LANES)),
                                  jnp.broadcast_to(wt[:, 1:2], (tm, LANES))], axis=1)

    onehot = jnp.where(jnp.logical_or(eid == i1, eid == i2), 1.0, 0.0)
    before = run_ref[...][:, 0:1]
    ranks = jnp.dot(onehot.astype(BF16), tri_ref[...], preferred_element_type=F32) + before
    pos1 = i1 * float(cap) + jnp.sum(jnp.where(eid == i1, ranks, 0.0), axis=0, keepdims=True)
    pos2 = i2 * float(cap) + jnp.sum(jnp.where(eid == i2, ranks, 0.0), axis=0, keepdims=True)
    run = run_ref[...] + jnp.sum(onehot, axis=1, keepdims=True)
    run_ref[...] = run

    posv = jnp.concatenate([pos1, pos2], axis=0).astype(jnp.int32)
    pos_ref[...] = posv
    posv_ref[...] = posv
    to_smem = pltpu.make_async_copy(posv_ref, poss_ref, psem_ref.at[0])
    to_smem.start()
    to_smem.wait()

    def issue(r, carry):
        row_copy(slot, r, poss_ref[0, r]).start()
        row_copy(slot, r, poss_ref[1, r]).start()
        return carry

    lax.fori_loop(0, tm, issue, 0, unroll=8)

    @pl.when(i == nt - 1)
    def _():
        slot_copies(slot).wait()
        slot_copies(slot).wait()

        @pl.when(nt > 1)
        def _():
            slot_copies(1 - slot).wait()
            slot_copies(1 - slot).wait()

        cnt_ref[...] = run
        erow = lax.broadcasted_iota(jnp.int32, run.shape, 0)
        fills = []
        for x in range(n_exp):
            c = (jnp.sum(jnp.where(erow == x, run, 0.0)) * (1.0 / LANES)).astype(jnp.int32)
            gap = lax.div(c + (pad_to - 1), pad_to) * pad_to - c
            start = x * cap + c
            for bit in reversed(range(pad_to.bit_length() - 1)):
                size = 1 << bit
                blk = pltpu.make_async_copy(zero_ref.at[pl.ds(0, size * groups), :],
                                            xs_hbm.at[pl.ds(start * groups, size * groups), :], zsem_ref.at[0])
                has = jnp.bitwise_and(lax.shift_right_logical(gap, bit), 1)
                fills.append((has, blk))
                start = start + has * size
        for has, blk in fills:
            pl.when(has == 1)(blk.start)
        for has, blk in fills:
            pl.when(has == 1)(blk.wait)


def _moe_expert_kernel(te_ref, tb_ref, ok_ref, xs_ref, wg_ref, wu_ref, wd_ref, ys_ref, *, rows, groups):
    t = pl.program_id(0)

    @pl.when(ok_ref[t] == 1)
    def _():
        x = _interleave_load(xs_ref, (), rows, groups).astype(BF16)
        g = jnp.dot(x, wg_ref[...], preferred_element_type=F32)
        u = jnp.dot(x, wu_ref[...], preferred_element_type=F32)
        act = (g * jax.nn.sigmoid(g) * u).astype(BF16)
        _interleave_store(ys_ref, (), jnp.dot(act, wd_ref[...], preferred_element_type=F32), rows)


def _moe_combine_kernel(pos_ref, nxt_ref, h_ref, w_ref, *rest, final):
    fw_ref = rest[0] if final else None
    ys_hbm, out_ref, ga_ref, gb_ref, sem_ref = rest[1:] if final else rest
    i = pl.program_id(0)
    nt = pl.num_programs(0)
    tm, d = h_ref.shape
    groups = d // LANES
    bufs = (ga_ref, gb_ref)

    def row_copy(p_ref, s, k, r):
        return pltpu.make_async_copy(ys_hbm.at[pl.ds(p_ref[k, r] * groups, groups), :],
                                     bufs[s].at[k, pl.ds(r * groups, groups), :], sem_ref.at[s, k])

    def wait_tile(s):
        for k in range(TOP_K):
            pltpu.make_async_copy(ys_hbm.at[pl.ds(0, tm * groups), :], bufs[s].at[k], sem_ref.at[s, k]).wait()

    @pl.when(i == 0)
    def _():
        def issue(r, carry):
            for k in range(TOP_K):
                row_copy(pos_ref, 0, k, r).start()
            return carry

        lax.fori_loop(0, tm, issue, 0, unroll=8)

    def step(s):
        wait_tile(s)
        for r in range(tm):
            for k in range(TOP_K):
                row_copy(nxt_ref, 1 - s, k, r).start()
        w = w_ref[...]
        w1 = jnp.concatenate([w[:, :LANES]] * groups, axis=1)
        w2 = jnp.concatenate([w[:, LANES:]] * groups, axis=1)
        y = (w1 * _interleave_load(bufs[s], (0,), tm, groups)
             + w2 * _interleave_load(bufs[s], (1,), tm, groups))
        _finish(h_ref[...], y, fw_ref, out_ref)

    even = lax.rem(i, 2) == 0
    pl.when(even)(lambda: step(0))
    pl.when(jnp.logical_not(even))(lambda: step(1))

    @pl.when(i == nt - 1)
    def _():
        pl.when(even)(lambda: wait_tile(1))
        pl.when(jnp.logical_not(even))(lambda: wait_tile(0))


def _moe(h, norm_w, router_w, wg, wu, wd, final_w):
    n, d = h.shape
    n_exp, _, fp = wg.shape
    groups = d // LANES
    assert d % LANES == 0 and groups == 8, "a row must fill one f32 (8,128) tile to move as one contiguous copy"
    tm = _row_tile(n, 1024)
    nt = n // tm
    tmx = min(MOE_TILE, n)
    assert tmx & (tmx - 1) == 0, "the zero fill writes power-of-two blocks below one tile"
    cap = -(-n // tmx) * tmx
    blocks_per_expert = cap // tmx
    dummy_block = n_exp * blocks_per_expert
    sorted_rows = (n_exp * cap + tmx) * groups
    final = final_w is not None
    er = -(-n_exp // 16) * 16
    rw = jnp.pad(router_w.astype(F32).T, ((0, er - n_exp), (0, 0)))
    rw_hi = rw.astype(BF16)
    rw2 = jnp.concatenate([rw_hi, (rw - rw_hi.astype(F32)).astype(BF16)], axis=0)
    tri = (jnp.arange(tm)[:, None] < jnp.arange(tm)[None, :]).astype(BF16)

    est = 4 * tm * d * 4 + 4 * er * d * 2 + 2 * tm * tm * 2 + 2 * tm * d * 4 + 8 * tm * LANES * 4 + 6 * tm * d * 4
    xs, pos, wts, cnt = pl.pallas_call(
        functools.partial(_moe_route_kernel, n_exp=n_exp, cap=cap, pad_to=tmx),
        grid=(nt,),
        in_specs=[
            pl.BlockSpec((tm, d), lambda i: (i, 0)),
            pl.BlockSpec((1, d), lambda i: (0, 0)),
            pl.BlockSpec((2 * er, d), lambda i: (0, 0)),
            pl.BlockSpec((tm, tm), lambda i: (0, 0)),
        ],
        out_specs=[
            pl.BlockSpec(memory_space=pl.ANY),
            pl.BlockSpec((None, TOP_K, tm), lambda i: (i, 0, 0)),
            pl.BlockSpec((tm, TOP_K * LANES), lambda i: (i, 0)),
            pl.BlockSpec((er, LANES), lambda i: (0, 0)),
        ],
        out_shape=[
            jax.ShapeDtypeStruct((sorted_rows, LANES), F32),
            jax.ShapeDtypeStruct((nt, TOP_K, tm), jnp.int32),
            jax.ShapeDtypeStruct((n, TOP_K * LANES), F32),
            jax.ShapeDtypeStruct((er, LANES), F32),
        ],
        scratch_shapes=[
            pltpu.VMEM((2, tm * groups, LANES), F32), pltpu.VMEM((er, LANES), F32),
            pltpu.VMEM((TOP_K, tm), jnp.int32), pltpu.SMEM((TOP_K, tm), jnp.int32),
            pltpu.VMEM((max(tmx // 2, 1) * groups, LANES), F32),
            pltpu.SemaphoreType.DMA((2,)), pltpu.SemaphoreType.DMA((1,)), pltpu.SemaphoreType.DMA((1,)),
        ],
        compiler_params=_cparams(("arbitrary",), est),
        name="moe_route",
    )(h, norm_w.reshape(1, d), rw2, tri)

    counts = cnt[:n_exp, 0].astype(jnp.int32)
    tiles = (counts + (tmx - 1)) // tmx
    first = jnp.cumsum(tiles) - tiles
    n_steps = 2 * n // tmx + n_exp
    step = jnp.arange(n_steps, dtype=jnp.int32)
    te = jnp.clip(jnp.sum((step[:, None] >= (first + tiles)[None, :]).astype(jnp.int32), axis=1), 0, n_exp - 1)
    ok = (step < jnp.sum(tiles)).astype(jnp.int32)
    tb = jnp.where(ok == 1, te * blocks_per_expert + step - first[te], dummy_block).astype(jnp.int32)

    est = 4 * tmx * d * 4 + 6 * d * fp * 2 + 4 * tmx * fp * 4 + 2 * tmx * d * 4
    sorted_spec = pl.BlockSpec((tmx * groups, LANES), lambda t, te, tb, ok: (tb[t], 0))
    ys = pl.pallas_call(
        functools.partial(_moe_expert_kernel, rows=tmx, groups=groups),
        grid_spec=pltpu.PrefetchScalarGridSpec(
            num_scalar_prefetch=3,
            grid=(n_steps,),
            in_specs=[
                sorted_spec,
                pl.BlockSpec((None, d, fp), lambda t, te, tb, ok: (te[t], 0, 0)),
                pl.BlockSpec((None, d, fp), lambda t, te, tb, ok: (te[t], 0, 0)),
                pl.BlockSpec((None, fp, d), lambda t, te, tb, ok: (te[t], 0, 0)),
            ],
            out_specs=sorted_spec,
        ),
        out_shape=jax.ShapeDtypeStruct((sorted_rows, LANES), F32),
        compiler_params=_cparams(("arbitrary",), est),
        name="moe_experts",
    )(te, tb, ok, xs, wg, wu, wd)

    in_specs = [
        pl.BlockSpec((None, TOP_K, tm), lambda i: (i, 0, 0), memory_space=pltpu.SMEM),
        pl.BlockSpec((None, TOP_K, tm), lambda i: (jnp.minimum(i + 1, nt - 1), 0, 0), memory_space=pltpu.SMEM),
        pl.BlockSpec((tm, d), lambda i: (i, 0)),
        pl.BlockSpec((tm, TOP_K * LANES), lambda i: (i, 0)),
    ]
    args = [pos, pos, h, wts]
    if final:
        in_specs.append(pl.BlockSpec((1, d), lambda i: (0, 0)))
        args.append(final_w.reshape(1, d))
    in_specs.append(pl.BlockSpec(memory_space=pl.ANY))
    args.append(ys)
    est = 4 * tm * d * 4 + 2 * tm * TOP_K * LANES * 4 + 2 * TOP_K * tm * d * 4 + 4 * tm * d * 4
    return pl.pallas_call(
        functools.partial(_moe_combine_kernel, final=final),
        grid=(nt,),
        in_specs=in_specs,
        out_specs=pl.BlockSpec((tm, d), lambda i: (i, 0)),
        out_shape=jax.ShapeDtypeStruct((n, d), F32),
        scratch_shapes=[pltpu.VMEM((TOP_K, tm * groups, LANES), F32), pltpu.VMEM((TOP_K, tm * groups, LANES), F32),
                        pltpu.SemaphoreType.DMA((2, TOP_K))],
        compiler_params=_cparams(("arbitrary",), est),
        name="moe_combine",
    )(*args)


def _run_group(x, pos0, caches, states, weights):
    (norm_mix_w, w_in, w_sb_o, w_ret_o, gn_ret_w, w_out, norm_ffn_w, ffn_w_gate, ffn_w_up,
     ffn_w_down, router_w, moe_w_gate, moe_w_up, moe_w_down, norm_final_w) = weights
    batch, seq, d = x.shape
    depth = len(w_in)
    w_sb = w_sb_o[0].shape[0]
    wv_ret = w_ret_o[0].shape[0]
    d_in = w_in[0].shape[1]
    wk_ret = (d_in - 3 * w_sb - 2 * wv_ret - 2 * d) // 2
    dims = dict(w_sb=w_sb, wk_ret=wk_ret, wv_ret=wv_ret)
    dk, dv = wk_ret // H_RET, wv_ret // H_RET
    hd = w_sb // H_SB

    h = x.reshape(batch * seq, d)
    kv = None
    finals = []
    if caches is not None:
        kc = caches[0].reshape(depth * batch, -1, hd)
        vc = caches[1].reshape(depth * batch, -1, hd)
    for l in range(depth):
        proj, k_all, v_all = _in_projection(h, norm_mix_w[l], w_in[l], l, depth, kv, w_sb=w_sb)
        kv = (k_all, v_all)
        if caches is None:
            o_sb = _sb_attention_prompt(proj, batch, seq, w_sb=w_sb)
            s0 = jnp.zeros((batch, H_RET, dk, dv), F32)
        else:
            o_sb = _sb_attention_sample(proj, kc, vc, l, batch, seq, w_sb=w_sb)
            s0 = states[l].astype(F32)
        o_r, s_fin = _retention(proj, gn_ret_w[l], s0, batch, seq, pos0, **dims)
        finals.append(s_fin)
        h = _merge(o_sb, o_r, proj, h, w_sb_o[l], w_ret_o[l], w_out[l], **dims)
        fw = norm_final_w if l == depth - 1 else None
        if l % 2 == 0:
            h = _ffn(h, norm_ffn_w[l], ffn_w_gate[l // 2], ffn_w_up[l // 2], ffn_w_down[l // 2], fw)
        else:
            h = _moe(h, norm_ffn_w[l], router_w[l // 2], moe_w_gate[l // 2], moe_w_up[l // 2],
                     moe_w_down[l // 2], fw)
    y = h.reshape(batch, seq, d)
    new_k = kv[0].reshape(depth, batch, seq, H_SB, hd)
    new_v = kv[1].reshape(depth, batch, seq, H_SB, hd)
    return y, new_k, new_v, jnp.stack(finals)


def kernel(x_prompt, x_sample, cache_sb_k, cache_sb_v, state_ret, norm_mix_w, w_in, w_sb_o, w_ret_o,
           gn_ret_w, w_out, norm_ffn_w, ffn_w_gate, ffn_w_up, ffn_w_down, router_w, moe_w_gate,
           moe_w_up, moe_w_down, norm_final_w):
    per_layer = lambda w: [w[l].astype(BF16) for l in range(w.shape[0])]
    weights = (
        norm_mix_w, per_layer(w_in), per_layer(w_sb_o), per_layer(w_ret_o), gn_ret_w,
        per_layer(w_out), norm_ffn_w, per_layer(ffn_w_gate), per_layer(ffn_w_up),
        per_layer(ffn_w_down), router_w,
        per_layer(moe_w_gate), per_layer(moe_w_up), per_layer(moe_w_down), norm_final_w,
    )
    past = cache_sb_k.shape[2]
    y_p, k_p, v_p, r_p = _run_group(x_prompt, 0.0, None, None, weights)
    y_s, k_s, v_s, r_s = _run_group(x_sample, float(past), (cache_sb_k, cache_sb_v), state_ret, weights)
    return (y_p, y_s, k_p, v_p, r_p, k_s, v_s, r_s)
```

```python
import functools

import jax
import jax.numpy as jnp
from jax import lax
from jax.experimental import pallas as pl
from jax.experimental.pallas import tpu as pltpu

F32 = jnp.float32
BF16 = jnp.bfloat16

H_SB = 4
H_RET = 4
TOP_K = 2
ROPE_BASE = 10000.0
EPS = 1e-6
GN_EPS = 1e-5
RET_CHUNK_MAX = 256

LANES = 128
V7X_VMEM_BYTES = 64 * 1024 * 1024
VMEM_CAP_BYTES = V7X_VMEM_BYTES - 8 * 1024 * 1024

SB_EXIT_MASS = 104.0
SB_BLOCK = 256


def _cparams(sem, est_bytes):
    limit = int(min(VMEM_CAP_BYTES, max(32 * 1024 * 1024, 2 * est_bytes)))
    return pltpu.CompilerParams(dimension_semantics=sem, vmem_limit_bytes=limit)


def _row_tile(n, want):
    t = min(n, want)
    assert n % t == 0, (n, t)
    return t


def _rms_scale(x, w):
    ms = jnp.mean(x * x, axis=-1, keepdims=True)
    return x * lax.rsqrt(ms + EPS) * w


def _inproj_kernel(x_ref, nw_ref, w_ref, *rest, q_scale, n_alias, tm, tn):
    proj_ref, k_ref, v_ref = rest[n_alias:]
    hn = _rms_scale(x_ref[...], nw_ref[...]).astype(BF16)

    def store_heads(dst_ref, acc):
        for hh in range(H_SB):
            dst_ref[pl.ds(hh, tm, stride=H_SB), :] = acc[:, hh * LANES:(hh + 1) * LANES]

    for j in range(w_ref.shape[1] // tn):
        cols = slice(j * tn, (j + 1) * tn)
        acc = jnp.dot(hn, w_ref[:, cols], preferred_element_type=F32)
        if j == 1:
            store_heads(k_ref, acc)
        if j == 2:
            store_heads(v_ref, acc)
        proj_ref[:, cols] = (acc * q_scale if j == 0 else acc).astype(BF16)


def _in_projection(x, norm_w, w_bf, layer, depth, kv_prev, *, w_sb):
    n, d = x.shape
    d_in = w_bf.shape[1]
    tn = w_sb
    assert d_in % tn == 0
    tm = _row_tile(n, 512)
    hd = w_sb // H_SB
    assert hd == LANES
    nm = n // tm
    kv_shape = jax.ShapeDtypeStruct((depth * n * H_SB, hd), F32)
    kv_spec = pl.BlockSpec((tm * H_SB, hd), lambda i: (layer * nm + i, 0))
    in_specs = [
        pl.BlockSpec((tm, d), lambda i: (i, 0)),
        pl.BlockSpec((1, d), lambda i: (0, 0)),
        pl.BlockSpec((d, d_in), lambda i: (0, 0), pipeline_mode=pl.Buffered(1)),
    ]
    args = [x, norm_w.reshape(1, d), w_bf]
    aliases = {}
    n_alias = 0
    if kv_prev is not None:
        in_specs += [pl.BlockSpec(memory_space=pl.ANY), pl.BlockSpec(memory_space=pl.ANY)]
        args += [kv_prev[0], kv_prev[1]]
        aliases = {3: 1, 4: 2}
        n_alias = 2
    est = 2 * tm * d * 4 + tm * d * 2 + d * d_in * 2 + 2 * tm * d_in * 2 + 4 * tm * w_sb * 4 + 4 * tm * tn * 4
    proj, k_all, v_all = pl.pallas_call(
        functools.partial(_inproj_kernel, q_scale=hd ** -0.5, n_alias=n_alias, tm=tm, tn=tn),
        grid=(nm,),
        in_specs=in_specs,
        out_specs=[pl.BlockSpec((tm, d_in), lambda i: (i, 0)), kv_spec, kv_spec],
        out_shape=[jax.ShapeDtypeStruct((n, d_in), BF16), kv_shape, kv_shape],
        input_output_aliases=aliases,
        compiler_params=_cparams(("parallel",), est),
        name=f"in_projection_l{layer}",
    )(*args)
    return proj, k_all, v_all


def _cumsum_matrix():
    j = jnp.arange(LANES)[:, None]
    s = jnp.arange(LANES)[None, :]
    tri = (j >= s).astype(BF16)
    m = jnp.concatenate([tri, jnp.ones((LANES, LANES), BF16)], axis=1)
    return jnp.concatenate([m, m], axis=0)


def _sb_step_phases(q, chunks, r, cum):
    zs = [lax.dot_general(q, k, (((1,), (1,)), ((), ())), preferred_element_type=F32)
          for k, _, _, _ in chunks]
    yield
    groups = []
    for (k, _, masks, live), z in zip(chunks, zs):
        for g in range(k.shape[0] // LANES):
            zg = z[:, g * LANES:(g + 1) * LANES]
            sp = jnp.maximum(zg, 0.0) + jnp.log(1.0 + jnp.exp(-jnp.abs(zg)))
            m = None if masks is None else masks[g]
            if m is not None:
                sp = jnp.where(m, sp, 0.0)
            if live is not None:
                sp = jnp.where(live, sp, 0.0)
            hi = sp.astype(BF16)
            lo = (sp - hi.astype(F32)).astype(BF16)
            groups.append([zg, jnp.concatenate([hi, lo], axis=1), m, live])
    yield
    for grp in groups:
        grp[1] = jnp.dot(grp[1], cum, preferred_element_type=F32)
    yield
    weights = [None] * len(groups)
    for idx in reversed(range(len(groups))):
        zg, ct, m, live = groups[idx]
        a = jnp.exp(zg - ct[:, :LANES] - r)
        if m is not None:
            a = jnp.where(m, a, 0.0)
        if live is not None:
            a = jnp.where(live, a, 0.0)
        weights[idx] = a.astype(BF16)
        r = r + ct[:, LANES:]
    yield
    pv = None
    idx = 0
    for _, v, _, _ in chunks:
        n_g = v.shape[0] // LANES
        a = weights[idx] if n_g == 1 else jnp.concatenate(weights[idx:idx + n_g], axis=1)
        t = jnp.dot(a, v, preferred_element_type=F32)
        pv = t if pv is None else pv + t
        idx += n_g
    return pv, r


def _lockstep(gens):
    results = [None] * len(gens)
    running = list(range(len(gens)))
    while running:
        still = []
        for t in running:
            try:
                next(gens[t])
                still.append(t)
            except StopIteration as stop:
                results[t] = stop.value
        running = still
    return results


def _sb_step(q, chunks, r, cum):
    return _lockstep([_sb_step_phases(q, chunks, r, cum)])[0]


def _sb_prompt_kernel(q_ref, k_ref, v_ref, cum_ref, o_ref, r_ref, acc_ref, *, qb, heads):
    i = pl.program_id(2)
    cum = cum_ref[...]
    row = lax.broadcasted_iota(jnp.int32, (qb, LANES), 0)
    col = lax.broadcasted_iota(jnp.int32, (qb, LANES), 1)
    causal = [(col + g * LANES) < row for g in range(qb // LANES)]
    has_prev = i > 0
    prev0 = pl.multiple_of(jnp.maximum(i - 1, 0) * qb, qb)
    diag0 = pl.multiple_of(i * qb, qb)
    zero = jnp.zeros((qb, LANES), F32)

    steps = []
    for hh in range(heads):
        cs = slice(hh * LANES, (hh + 1) * LANES)
        chunks = [
            (k_ref[pl.ds(prev0, qb), cs], v_ref[pl.ds(prev0, qb), cs], None, has_prev),
            (k_ref[pl.ds(diag0, qb), cs], v_ref[pl.ds(diag0, qb), cs], causal, None),
        ]
        steps.append(_sb_step_phases(q_ref[:, cs], chunks, zero, cum))
    for hh, (pv, r) in enumerate(_lockstep(steps)):
        acc_ref[hh] = pv
        r_ref[hh] = r

    for hh in range(heads):
        cs = slice(hh * LANES, (hh + 1) * LANES)

        def cond(j, hh=hh):
            return jnp.logical_and(j >= 0, jnp.min(r_ref[hh]) < SB_EXIT_MASS)

        def body(j, hh=hh, cs=cs):
            start = pl.multiple_of(j * qb, qb)
            chunks = [(k_ref[pl.ds(start, qb), cs], v_ref[pl.ds(start, qb), cs], None, None)]
            pv, r = _sb_step(q_ref[:, cs], chunks, r_ref[hh], cum)
            acc_ref[hh] += pv
            r_ref[hh] = r
            return j - 1

        lax.while_loop(cond, body, i - 2)

    for hh in range(heads):
        o_ref[:, hh * LANES:(hh + 1) * LANES] = acc_ref[hh].astype(o_ref.dtype)


def _sb_attention_prompt(proj, batch, seq, *, w_sb):
    n = proj.shape[0]
    hd = w_sb // H_SB
    assert hd == LANES
    qb = SB_BLOCK
    assert seq % qb == 0
    nq = seq // qb
    heads = H_SB
    wb = heads * hd
    kcol = w_sb // wb
    vcol = 2 * w_sb // wb
    est = 4 * qb * wb * 2 + 2 * seq * wb * 2 + 4 * heads * qb * LANES * 4 + 40 * qb * LANES * 4 * heads
    resident = lambda c0: pl.BlockSpec((seq, wb), lambda b, h, i: (b, c0 + h), pipeline_mode=pl.Buffered(1))
    return pl.pallas_call(
        functools.partial(_sb_prompt_kernel, qb=qb, heads=heads),
        grid=(batch, H_SB // heads, nq),
        in_specs=[
            pl.BlockSpec((qb, wb), lambda b, h, i: (b * nq + i, h)),
            resident(kcol),
            resident(vcol),
            pl.BlockSpec((2 * LANES, 2 * LANES), lambda b, h, i: (0, 0)),
        ],
        out_specs=pl.BlockSpec((qb, wb), lambda b, h, i: (b * nq + i, h)),
        out_shape=jax.ShapeDtypeStruct((n, w_sb), BF16),
        scratch_shapes=[pltpu.VMEM((heads, qb, LANES), F32), pltpu.VMEM((heads, qb, LANES), F32)],
        compiler_params=_cparams(("parallel", "parallel", "arbitrary"), est),
        name="sb_attention_prompt",
    )(proj, proj, proj, _cumsum_matrix())


def _sb_sample_kernel(q_ref, kn_ref, vn_ref, kc_hbm, vc_hbm, cum_ref, o_ref, r_ref, acc_ref, buf_ref, sem_ref,
                      *, tq, past, cb, stream0, n_streams):
    b = pl.program_id(0)
    cum = cum_ref[...]
    row = lax.broadcasted_iota(jnp.int32, (tq, LANES), 0)
    col = lax.broadcasted_iota(jnp.int32, (tq, LANES), 1)
    zero = jnp.zeros((tq, LANES), F32)
    rows = cb * H_SB
    latest = past - cb

    def chunk_copy(stream, start, slot, which):
        src = (kc_hbm, vc_hbm)[which].at[stream0 + stream, pl.ds(start * H_SB, rows), :]
        return pltpu.make_async_copy(src, buf_ref.at[slot, which], sem_ref.at[slot, which])

    def cached(slot, which, hh):
        return buf_ref[slot, which, pl.ds(hh, cb, stride=H_SB), :].astype(BF16)

    slot = lax.rem(b, 2)

    @pl.when(b == 0)
    def _():
        for which in range(2):
            chunk_copy(0, latest, 0, which).start()

    for which in range(2):
        chunk_copy(b, latest, slot, which).wait()

    @pl.when(b + 1 < n_streams)
    def _():
        for which in range(2):
            chunk_copy(b + 1, latest, 1 - slot, which).start()

    steps = []
    for hh in range(H_SB):
        cs = slice(hh * LANES, (hh + 1) * LANES)
        chunks = [
            (cached(slot, 0, hh), cached(slot, 1, hh), None, None),
            (kn_ref[:, cs], vn_ref[:, cs], [col < row], None),
        ]
        steps.append(_sb_step_phases(q_ref[:, cs], chunks, zero, cum))
    for hh, (pv, r) in enumerate(_lockstep(steps)):
        acc_ref[hh] = pv
        r_ref[hh] = r

    def cond(j):
        return jnp.logical_and(j >= 0, jnp.min(r_ref[...]) < SB_EXIT_MASS)

    def body(j):
        for which in range(2):
            chunk_copy(b, j * cb, 2, which).start()
        for which in range(2):
            chunk_copy(b, j * cb, 2, which).wait()
        steps = []
        for hh in range(H_SB):
            cs = slice(hh * LANES, (hh + 1) * LANES)
            chunks = [(cached(2, 0, hh), cached(2, 1, hh), None, None)]
            steps.append(_sb_step_phases(q_ref[:, cs], chunks, r_ref[hh], cum))
        for hh, (pv, r) in enumerate(_lockstep(steps)):
            acc_ref[hh] += pv
            r_ref[hh] = r
        return j - 1

    lax.while_loop(cond, body, past // cb - 2)

    for hh in range(H_SB):
        o_ref[:, hh * LANES:(hh + 1) * LANES] = acc_ref[hh].astype(o_ref.dtype)


def _sb_attention_sample(proj, k_cache, v_cache, layer, batch, tq, *, w_sb):
    n = proj.shape[0]
    hd = w_sb // H_SB
    past = k_cache.shape[1] // H_SB
    cb = SB_BLOCK if past % SB_BLOCK == 0 else LANES
    assert hd == LANES and past % cb == 0 and tq <= LANES
    kv_new = proj[:, w_sb:3 * w_sb].reshape(batch, tq, 2 * w_sb)
    kv_new = jnp.pad(kv_new, ((0, 0), (0, LANES - tq), (0, 0)))
    est = 6 * cb * w_sb * 4 + 8 * LANES * w_sb * 2 + 40 * H_SB * max(tq, 8) * LANES * 4
    return pl.pallas_call(
        functools.partial(_sb_sample_kernel, tq=tq, past=past, cb=cb, stream0=layer * batch, n_streams=batch),
        grid=(batch,),
        in_specs=[
            pl.BlockSpec((tq, w_sb), lambda b: (b, 0)),
            pl.BlockSpec((None, LANES, w_sb), lambda b: (b, 0, 0)),
            pl.BlockSpec((None, LANES, w_sb), lambda b: (b, 0, 1)),
            pl.BlockSpec(memory_space=pl.ANY),
            pl.BlockSpec(memory_space=pl.ANY),
            pl.BlockSpec((2 * LANES, 2 * LANES), lambda b: (0, 0)),
        ],
        out_specs=pl.BlockSpec((tq, w_sb), lambda b: (b, 0)),
        out_shape=jax.ShapeDtypeStruct((n, w_sb), BF16),
        scratch_shapes=[pltpu.VMEM((H_SB, tq, LANES), F32), pltpu.VMEM((H_SB, tq, LANES), F32),
                        pltpu.VMEM((3, 2, cb * H_SB, hd), F32), pltpu.SemaphoreType.DMA((3, 2))],
        compiler_params=_cparams(("arbitrary",), est),
        name="sb_attention_sample",
    )(proj, kv_new, kv_new, k_cache, v_cache, _cumsum_matrix())


def _retention_kernel(dec_ref, q_ref, k_ref, va_ref, vb_ref, ga_ref, gb_ref, cos_ref, sin_ref, gnw_ref,
                      s0_ref, o_ref, sout_ref, s_ref, d_ref, qd_ref, kd_ref, *, blk, k_scale, dk, dv):
    c = pl.program_id(1)
    per_half = H_RET // 2

    @pl.when(c == 0)
    def _():
        s_ref[...] = s0_ref[...]
        n = lax.broadcasted_iota(jnp.int32, (blk, blk), 0)
        m = lax.broadcasted_iota(jnp.int32, (blk, blk), 1)
        causal = n >= m
        diff = jnp.where(causal, n - m, 0).astype(F32)
        rows = lax.broadcasted_iota(jnp.int32, (blk, LANES), 0).astype(F32)
        for hh in range(H_RET):
            lg = dec_ref[0, hh]
            d_ref[hh] = jnp.where(causal, jnp.exp(diff * lg) * k_scale, 0.0)
            qd_ref[hh] = jnp.exp((rows + 1.0) * lg)
            kd_ref[hh] = jnp.exp((blk - 1.0 - rows) * lg) * k_scale

    cos = cos_ref[...]
    sin = sin_ref[...]
    half = dk // 2
    for hh in range(H_RET):
        ks = slice(hh * dk, (hh + 1) * dk)
        vs = slice(hh * dv, (hh + 1) * dv)
        qf = q_ref[:, ks].astype(F32)
        kf = k_ref[:, ks].astype(F32)
        qr = qf * cos + pltpu.roll(qf, half, 1) * sin
        kr = kf * cos + pltpu.roll(kf, half, 1) * sin
        hs = slice((hh % per_half) * dv, (hh % per_half + 1) * dv)
        v = (va_ref if hh < per_half else vb_ref)[:, hs]

        scores = lax.dot_general(qr.astype(BF16), kr.astype(BF16), (((1,), (1,)), ((), ())),
                                 preferred_element_type=F32) * d_ref[hh]
        inner = jnp.dot(scores.astype(BF16), v, preferred_element_type=F32)
        s = s_ref[hh]
        cross = jnp.dot((qr * qd_ref[hh]).astype(BF16), s.astype(BF16), preferred_element_type=F32)
        o = inner + cross
        kv = lax.dot_general((kr * kd_ref[hh]).astype(BF16), v, (((0,), (0,)), ((), ())),
                             preferred_element_type=F32)
        s_new = dec_ref[1, hh] * s + kv
        s_ref[hh] = s_new
        sout_ref[hh] = s_new

        mu = jnp.mean(o, axis=-1, keepdims=True)
        dev = o - mu
        var = jnp.mean(dev * dev, axis=-1, keepdims=True)
        y = dev * lax.rsqrt(var + GN_EPS) * gnw_ref[:, vs]
        gf = (ga_ref if hh < per_half else gb_ref)[:, hs].astype(F32)
        o_ref[:, vs] = (gf * jax.nn.sigmoid(gf) * y).astype(o_ref.dtype)


def _retention(proj, gn_w, state0, batch, seq, pos0, *, w_sb, wk_ret, wv_ret):
    n = proj.shape[0]
    dk = wk_ret // H_RET
    dv = wv_ret // H_RET
    assert dk == LANES and dv % LANES == 0
    blk = min(seq, RET_CHUNK_MAX)
    assert seq % blk == 0
    nc = seq // blk
    hw = wv_ret // 2
    assert (3 * w_sb) % wk_ret == 0 and (3 * w_sb + 2 * wk_ret) % hw == 0 and H_RET % 2 == 0
    q0 = 3 * w_sb // wk_ret
    k0 = q0 + 1
    v0 = (3 * w_sb + 2 * wk_ret) // hw
    g0 = v0 + 2

    half = dk // 2
    inv = ROPE_BASE ** (-jnp.arange(half, dtype=F32) / half)
    ang = (pos0 + jnp.arange(seq, dtype=F32))[:, None] * inv[None, :]
    cos = jnp.concatenate([jnp.cos(ang), jnp.cos(ang)], axis=1)
    sin = jnp.concatenate([-jnp.sin(ang), jnp.sin(ang)], axis=1)
    log_gamma = jnp.log1p(-jnp.exp2(-5.0 - jnp.arange(H_RET, dtype=F32)))
    dec = jnp.stack([log_gamma, jnp.exp(blk * log_gamma)])

    est = 2 * blk * (2 * wk_ret + 2 * wv_ret) * 2 + 4 * blk * dk * 4 + 5 * H_RET * dk * dv * 4 \
        + 2 * blk * wv_ret * 2 + H_RET * (blk * blk + 2 * blk * LANES) * 4 + 12 * H_RET * blk * dv * 4
    row = lambda b, c: b * nc + c
    return pl.pallas_call(
        functools.partial(_retention_kernel, blk=blk, k_scale=dk ** -0.5, dk=dk, dv=dv),
        grid=(batch, nc),
        in_specs=[
            pl.BlockSpec(memory_space=pltpu.SMEM),
            pl.BlockSpec((blk, wk_ret), lambda b, c: (row(b, c), q0)),
            pl.BlockSpec((blk, wk_ret), lambda b, c: (row(b, c), k0)),
            pl.BlockSpec((blk, hw), lambda b, c: (row(b, c), v0)),
            pl.BlockSpec((blk, hw), lambda b, c: (row(b, c), v0 + 1)),
            pl.BlockSpec((blk, hw), lambda b, c: (row(b, c), g0)),
            pl.BlockSpec((blk, hw), lambda b, c: (row(b, c), g0 + 1)),
            pl.BlockSpec((blk, dk), lambda b, c: (c, 0)),
            pl.BlockSpec((blk, dk), lambda b, c: (c, 0)),
            pl.BlockSpec((1, wv_ret), lambda b, c: (0, 0)),
            pl.BlockSpec((None, H_RET, dk, dv), lambda b, c: (b, 0, 0, 0)),
        ],
        out_specs=[
            pl.BlockSpec((blk, wv_ret), lambda b, c: (row(b, c), 0)),
            pl.BlockSpec((None, H_RET, dk, dv), lambda b, c: (b, 0, 0, 0)),
        ],
        out_shape=[jax.ShapeDtypeStruct((n, wv_ret), BF16),
                   jax.ShapeDtypeStruct((batch, H_RET, dk, dv), F32)],
        scratch_shapes=[pltpu.VMEM((H_RET, dk, dv), F32), pltpu.VMEM((H_RET, blk, blk), F32),
                        pltpu.VMEM((H_RET, blk, LANES), F32), pltpu.VMEM((H_RET, blk, LANES), F32)],
        compiler_params=_cparams(("parallel", "arbitrary"), est),
        name="retention",
    )(dec, proj, proj, proj, proj, proj, proj, cos, sin, gn_w.reshape(1, wv_ret), state0)


def _merge_kernel(osb_ref, or_ref, asb0_ref, asb1_ref, ar0_ref, ar1_ref, h_ref,
                  wsb_ref, wr_ref, wo_ref, out_ref):
    b_sb = jnp.dot(osb_ref[...], wsb_ref[...], preferred_element_type=F32)
    b_r = jnp.dot(or_ref[...], wr_ref[...], preferred_element_type=F32)
    a_sb = jnp.concatenate([asb0_ref[...], asb1_ref[...]], axis=1).astype(F32)
    a_r = jnp.concatenate([ar0_ref[...], ar1_ref[...]], axis=1).astype(F32)
    merged = jax.nn.sigmoid(a_sb) * b_sb + jax.nn.sigmoid(a_r) * b_r
    out_ref[...] = h_ref[...] + jnp.dot(merged.astype(BF16), wo_ref[...], preferred_element_type=F32)


def _merge(o_sb, o_r, proj, h, w_sb_o, w_ret_o, w_out, *, w_sb, wk_ret, wv_ret):
    n, d = h.shape
    tm = _row_tile(n, 512)
    half = d // 2
    a0 = (3 * w_sb + 2 * wk_ret + 2 * wv_ret) // half
    assert (3 * w_sb + 2 * wk_ret + 2 * wv_ret) % half == 0
    gate = lambda t: pl.BlockSpec((tm, half), lambda i: (i, a0 + t))
    full = lambda a: pl.BlockSpec(a.shape, lambda i: (0, 0))
    est = 2 * tm * (w_sb + wv_ret + 2 * d) * 2 + 4 * tm * d * 4 + 2 * (w_sb + wv_ret + d) * d * 2 + 6 * tm * d * 4
    return pl.pallas_call(
        _merge_kernel,
        grid=(n // tm,),
        in_specs=[
            pl.BlockSpec((tm, w_sb), lambda i: (i, 0)),
            pl.BlockSpec((tm, wv_ret), lambda i: (i, 0)),
            gate(0), gate(1), gate(2), gate(3),
            pl.BlockSpec((tm, d), lambda i: (i, 0)),
            full(w_sb_o), full(w_ret_o), full(w_out),
        ],
        out_specs=pl.BlockSpec((tm, d), lambda i: (i, 0)),
        out_shape=jax.ShapeDtypeStruct((n, d), F32),
        compiler_params=_cparams(("parallel",), est),
        name="merge",
    )(o_sb, o_r, proj, proj, proj, proj, h, w_sb_o, w_ret_o, w_out)


def _finish(h, acc, fw_ref, out_ref):
    y = h + acc
    if fw_ref is not None:
        y = _rms_scale(y, fw_ref[...])
    out_ref[...] = y


def _ffn_kernel(h_ref, nw_ref, wg_ref, wu_ref, wd_ref, *rest, final, tf):
    fw_ref = rest[0] if final else None
    out_ref = rest[-1]
    h = h_ref[...]
    hn = _rms_scale(h, nw_ref[...]).astype(BF16)
    acc = None
    for f in range(wg_ref.shape[1] // tf):
        cols = slice(f * tf, (f + 1) * tf)
        g = jnp.dot(hn, wg_ref[:, cols], preferred_element_type=F32)
        u = jnp.dot(hn, wu_ref[:, cols], preferred_element_type=F32)
        act = (g * jax.nn.sigmoid(g) * u).astype(BF16)
        t = jnp.dot(act, wd_ref[cols, :], preferred_element_type=F32)
        acc = t if acc is None else acc + t
    _finish(h, acc, fw_ref, out_ref)


def _ffn(h, norm_w, wg, wu, wd, final_w):
    n, d = h.shape
    ff = wg.shape[1]
    tm = _row_tile(n, 512)
    tf = 2 * LANES
    assert ff % tf == 0
    final = final_w is not None
    resident = lambda a: pl.BlockSpec(a.shape, lambda i: (0, 0), pipeline_mode=pl.Buffered(1))
    in_specs = [
        pl.BlockSpec((tm, d), lambda i: (i, 0)),
        pl.BlockSpec((1, d), lambda i: (0, 0)),
        resident(wg), resident(wu), resident(wd),
    ]
    args = [h, norm_w.reshape(1, d), wg, wu, wd]
    if final:
        in_specs.append(pl.BlockSpec((1, d), lambda i: (0, 0)))
        args.append(final_w.reshape(1, d))
    est = 4 * tm * d * 4 + tm * d * 2 + 3 * d * ff * 2 + 4 * tm * tf * 4 + 2 * tm * d * 4
    return pl.pallas_call(
        functools.partial(_ffn_kernel, final=final, tf=tf),
        grid=(n // tm,),
        in_specs=in_specs,
        out_specs=pl.BlockSpec((tm, d), lambda i: (i, 0)),
        out_shape=jax.ShapeDtypeStruct((n, d), F32),
        compiler_params=_cparams(("parallel",), est),
        name="ffn_dense",
    )(*args)


MOE_TILE = 512


def _interleave_store(dst_ref, lead, x, n_rows):
    groups = x.shape[1] // LANES
    for g in range(groups):
        dst_ref[lead + (pl.ds(g, n_rows, stride=groups), slice(None))] = x[:, g * LANES:(g + 1) * LANES]


def _interleave_load(src_ref, lead, n_rows, groups):
    parts = [src_ref[lead + (pl.ds(g, n_rows, stride=groups), slice(None))] for g in range(groups)]
    return jnp.concatenate(parts, axis=1)


def _moe_route_kernel(h_ref, nw_ref, rw_ref, tri_ref, xs_hbm, pos_ref, w_ref, cnt_ref,
                      x3_ref, run_ref, posv_ref, poss_ref, zero_ref, sem_ref, psem_ref, zsem_ref,
                      *, n_exp, cap, pad_to):
    i = pl.program_id(0)
    nt = pl.num_programs(0)
    tm, d = h_ref.shape
    groups = d // LANES
    slot = lax.rem(i, 2)

    def slot_copies(s):
        return pltpu.make_async_copy(x3_ref.at[s], xs_hbm.at[pl.ds(0, tm * groups), :], sem_ref.at[s])

    def row_copy(s, r, p):
        return pltpu.make_async_copy(x3_ref.at[s, pl.ds(r * groups, groups), :],
                                     xs_hbm.at[pl.ds(p * groups, groups), :], sem_ref.at[s])

    @pl.when(i == 0)
    def _():
        run_ref[...] = jnp.zeros_like(run_ref)
        zero_ref[...] = jnp.zeros_like(zero_ref)

    @pl.when(i >= 2)
    def _():
        slot_copies(slot).wait()
        slot_copies(slot).wait()

    hn = _rms_scale(h_ref[...], nw_ref[...])
    _interleave_store(x3_ref, (slot,), hn, tm)

    er = rw_ref.shape[0] // 2
    nt_dims = (((1,), (1,)), ((), ()))
    hn_hi = hn.astype(BF16)
    hn_lo = (hn - hn_hi.astype(F32)).astype(BF16)
    by_hi = lax.dot_general(rw_ref[...], hn_hi, nt_dims, preferred_element_type=F32)
    logits = by_hi[:er] + by_hi[er:] + lax.dot_general(rw_ref[:er], hn_lo, nt_dims, preferred_element_type=F32)
    eid = lax.broadcasted_iota(jnp.int32, logits.shape, 0).astype(F32)
    logits = jnp.where(eid < n_exp, logits, -jnp.inf)
    m1 = jnp.max(logits, axis=0, keepdims=True)
    i1 = jnp.min(jnp.where(logits == m1, eid, float(er)), axis=0, keepdims=True)
    rest_l = jnp.where(eid == i1, -jnp.inf, logits)
    m2 = jnp.max(rest_l, axis=0, keepdims=True)
    i2 = jnp.min(jnp.where(rest_l == m2, eid, float(er)), axis=0, keepdims=True)
    ex = jnp.exp(m2 - m1)
    srow = lax.broadcasted_iota(jnp.int32, (LANES, tm), 0)
    wslab = jnp.where(srow == 0, 1.0 / (1.0 + ex), jnp.where(srow == 1, ex / (1.0 + ex), 0.0))
    wt = jnp.transpose(wslab)
    w_ref[...] = jnp.concatenate([jnp.broadcast_to(wt[:, 0:1], (tm, LANES)),
                                  jnp.broadcast_to(wt[:, 1:2], (tm, LANES))], axis=1)

    onehot = jnp.where(jnp.logical_or(eid == i1, eid == i2), 1.0, 0.0)
    before = run_ref[...][:, 0:1]
    ranks = jnp.dot(onehot.astype(BF16), tri_ref[...], preferred_element_type=F32) + before
    pos1 = i1 * float(cap) + jnp.sum(jnp.where(eid == i1, ranks, 0.0), axis=0, keepdims=True)
    pos2 = i2 * float(cap) + jnp.sum(jnp.where(eid == i2, ranks, 0.0), axis=0, keepdims=True)
    run = run_ref[...] + jnp.sum(onehot, axis=1, keepdims=True)
    run_ref[...] = run

    posv = jnp.concatenate([pos1, pos2], axis=0).astype(jnp.int32)
    pos_ref[...] = posv
    posv_ref[...] = posv
    to_smem = pltpu.make_async_copy(posv_ref, poss_ref, psem_ref.at[0])
    to_smem.start()
    to_smem.wait()

    def issue(r, carry):
        row_copy(slot, r, poss_ref[0, r]).start(priority=0)
        row_copy(slot, r, poss_ref[1, r]).start(priority=1)
        return carry

    lax.fori_loop(0, tm, issue, 0, unroll=8)

    @pl.when(i == nt - 1)
    def _():
        slot_copies(slot).wait()
        slot_copies(slot).wait()

        @pl.when(nt > 1)
        def _():
            slot_copies(1 - slot).wait()
            slot_copies(1 - slot).wait()

        cnt_ref[...] = run
        erow = lax.broadcasted_iota(jnp.int32, run.shape, 0)
        fills = []
        for x in range(n_exp):
            c = (jnp.sum(jnp.where(erow == x, run, 0.0)) * (1.0 / LANES)).astype(jnp.int32)
            gap = lax.div(c + (pad_to - 1), pad_to) * pad_to - c
            start = x * cap + c
            for bit in reversed(range(pad_to.bit_length() - 1)):
                size = 1 << bit
                blk = pltpu.make_async_copy(zero_ref.at[pl.ds(0, size * groups), :],
                                            xs_hbm.at[pl.ds(start * groups, size * groups), :], zsem_ref.at[0])
                has = jnp.bitwise_and(lax.shift_right_logical(gap, bit), 1)
                fills.append((has, blk))
                start = start + has * size
        for has, blk in fills:
            pl.when(has == 1)(blk.start)
        for has, blk in fills:
            pl.when(has == 1)(blk.wait)


def _moe_expert_kernel(te_ref, tb_ref, ok_ref, xs_ref, wg_ref, wu_ref, wd_ref, ys_ref, *, rows, groups):
    t = pl.program_id(0)

    @pl.when(ok_ref[t] == 1)
    def _():
        x = _interleave_load(xs_ref, (), rows, groups).astype(BF16)
        g = jnp.dot(x, wg_ref[...], preferred_element_type=F32)
        u = jnp.dot(x, wu_ref[...], preferred_element_type=F32)
        act = (g * jax.nn.sigmoid(g) * u).astype(BF16)
        _interleave_store(ys_ref, (), jnp.dot(act, wd_ref[...], preferred_element_type=F32), rows)


def _moe_combine_kernel(pos_ref, nxt_ref, h_ref, w_ref, *rest, final):
    fw_ref = rest[0] if final else None
    ys_hbm, out_ref, ga_ref, gb_ref, sem_ref = rest[1:] if final else rest
    i = pl.program_id(0)
    nt = pl.num_programs(0)
    tm, d = h_ref.shape
    groups = d // LANES
    bufs = (ga_ref, gb_ref)

    def row_copy(p_ref, s, k, r):
        return pltpu.make_async_copy(ys_hbm.at[pl.ds(p_ref[k, r] * groups, groups), :],
                                     bufs[s].at[k, pl.ds(r * groups, groups), :], sem_ref.at[s, k])

    def wait_tile(s):
        for k in range(TOP_K):
            pltpu.make_async_copy(ys_hbm.at[pl.ds(0, tm * groups), :], bufs[s].at[k], sem_ref.at[s, k]).wait()

    @pl.when(i == 0)
    def _():
        def issue(r, carry):
            for k in range(TOP_K):
                row_copy(pos_ref, 0, k, r).start(priority=k % 2)
            return carry

        lax.fori_loop(0, tm, issue, 0, unroll=8)

    def step(s):
        wait_tile(s)
        for r in range(tm):
            for k in range(TOP_K):
                row_copy(nxt_ref, 1 - s, k, r).start(priority=k % 2)
        w = w_ref[...]
        w1 = jnp.concatenate([w[:, :LANES]] * groups, axis=1)
        w2 = jnp.concatenate([w[:, LANES:]] * groups, axis=1)
        y = (w1 * _interleave_load(bufs[s], (0,), tm, groups)
             + w2 * _interleave_load(bufs[s], (1,), tm, groups))
        _finish(h_ref[...], y, fw_ref, out_ref)

    even = lax.rem(i, 2) == 0
    pl.when(even)(lambda: step(0))
    pl.when(jnp.logical_not(even))(lambda: step(1))

    @pl.when(i == nt - 1)
    def _():
        pl.when(even)(lambda: wait_tile(1))
        pl.when(jnp.logical_not(even))(lambda: wait_tile(0))


def _moe(h, norm_w, router_w, wg, wu, wd, final_w):
    n, d = h.shape
    n_exp, _, fp = wg.shape
    groups = d // LANES
    assert d % LANES == 0 and groups == 8, "a row must fill one f32 (8,128) tile to move as one contiguous copy"
    tm = _row_tile(n, 1024)
    nt = n // tm
    tmx = min(MOE_TILE, n)
    assert tmx & (tmx - 1) == 0, "the zero fill writes power-of-two blocks below one tile"
    cap = -(-n // tmx) * tmx
    blocks_per_expert = cap // tmx
    dummy_block = n_exp * blocks_per_expert
    sorted_rows = (n_exp * cap + tmx) * groups
    final = final_w is not None
    er = -(-n_exp // 16) * 16
    rw = jnp.pad(router_w.astype(F32).T, ((0, er - n_exp), (0, 0)))
    rw_hi = rw.astype(BF16)
    rw2 = jnp.concatenate([rw_hi, (rw - rw_hi.astype(F32)).astype(BF16)], axis=0)
    tri = (jnp.arange(tm)[:, None] < jnp.arange(tm)[None, :]).astype(BF16)

    est = 4 * tm * d * 4 + 4 * er * d * 2 + 2 * tm * tm * 2 + 2 * tm * d * 4 + 8 * tm * LANES * 4 + 6 * tm * d * 4
    xs, pos, wts, cnt = pl.pallas_call(
        functools.partial(_moe_route_kernel, n_exp=n_exp, cap=cap, pad_to=tmx),
        grid=(nt,),
        in_specs=[
            pl.BlockSpec((tm, d), lambda i: (i, 0)),
            pl.BlockSpec((1, d), lambda i: (0, 0)),
            pl.BlockSpec((2 * er, d), lambda i: (0, 0)),
            pl.BlockSpec((tm, tm), lambda i: (0, 0)),
        ],
        out_specs=[
            pl.BlockSpec(memory_space=pl.ANY),
            pl.BlockSpec((None, TOP_K, tm), lambda i: (i, 0, 0)),
            pl.BlockSpec((tm, TOP_K * LANES), lambda i: (i, 0)),
            pl.BlockSpec((er, LANES), lambda i: (0, 0)),
        ],
        out_shape=[
            jax.ShapeDtypeStruct((sorted_rows, LANES), F32),
            jax.ShapeDtypeStruct((nt, TOP_K, tm), jnp.int32),
            jax.ShapeDtypeStruct((n, TOP_K * LANES), F32),
            jax.ShapeDtypeStruct((er, LANES), F32),
        ],
        scratch_shapes=[
            pltpu.VMEM((2, tm * groups, LANES), F32), pltpu.VMEM((er, LANES), F32),
            pltpu.VMEM((TOP_K, tm), jnp.int32), pltpu.SMEM((TOP_K, tm), jnp.int32),
            pltpu.VMEM((max(tmx // 2, 1) * groups, LANES), F32),
            pltpu.SemaphoreType.DMA((2,)), pltpu.SemaphoreType.DMA((1,)), pltpu.SemaphoreType.DMA((1,)),
        ],
        compiler_params=_cparams(("arbitrary",), est),
        name="moe_route",
    )(h, norm_w.reshape(1, d), rw2, tri)

    counts = cnt[:n_exp, 0].astype(jnp.int32)
    tiles = (counts + (tmx - 1)) // tmx
    first = jnp.cumsum(tiles) - tiles
    n_steps = 2 * n // tmx + n_exp
    step = jnp.arange(n_steps, dtype=jnp.int32)
    te = jnp.clip(jnp.sum((step[:, None] >= (first + tiles)[None, :]).astype(jnp.int32), axis=1), 0, n_exp - 1)
    ok = (step < jnp.sum(tiles)).astype(jnp.int32)
    tb = jnp.where(ok == 1, te * blocks_per_expert + step - first[te], dummy_block).astype(jnp.int32)

    est = 4 * tmx * d * 4 + 6 * d * fp * 2 + 4 * tmx * fp * 4 + 2 * tmx * d * 4
    sorted_spec = pl.BlockSpec((tmx * groups, LANES), lambda t, te, tb, ok: (tb[t], 0))
    ys = pl.pallas_call(
        functools.partial(_moe_expert_kernel, rows=tmx, groups=groups),
        grid_spec=pltpu.PrefetchScalarGridSpec(
            num_scalar_prefetch=3,
            grid=(n_steps,),
            in_specs=[
                sorted_spec,
                pl.BlockSpec((None, d, fp), lambda t, te, tb, ok: (te[t], 0, 0)),
                pl.BlockSpec((None, d, fp), lambda t, te, tb, ok: (te[t], 0, 0)),
                pl.BlockSpec((None, fp, d), lambda t, te, tb, ok: (te[t], 0, 0)),
            ],
            out_specs=sorted_spec,
        ),
        out_shape=jax.ShapeDtypeStruct((sorted_rows, LANES), F32),
        compiler_params=_cparams(("arbitrary",), est),
        name="moe_experts",
    )(te, tb, ok, xs, wg, wu, wd)

    in_specs = [
        pl.BlockSpec((None, TOP_K, tm), lambda i: (i, 0, 0), memory_space=pltpu.SMEM),
        pl.BlockSpec((None, TOP_K, tm), lambda i: (jnp.minimum(i + 1, nt - 1), 0, 0), memory_space=pltpu.SMEM),
        pl.BlockSpec((tm, d), lambda i: (i, 0)),
        pl.BlockSpec((tm, TOP_K * LANES), lambda i: (i, 0)),
    ]
    args = [pos, pos, h, wts]
    if final:
        in_specs.append(pl.BlockSpec((1, d), lambda i: (0, 0)))
        args.append(final_w.reshape(1, d))
    in_specs.append(pl.BlockSpec(memory_space=pl.ANY))
    args.append(ys)
    est = 4 * tm * d * 4 + 2 * tm * TOP_K * LANES * 4 + 2 * TOP_K * tm * d * 4 + 4 * tm * d * 4
    return pl.pallas_call(
        functools.partial(_moe_combine_kernel, final=final),
        grid=(nt,),
        in_specs=in_specs,
        out_specs=pl.BlockSpec((tm, d), lambda i: (i, 0)),
        out_shape=jax.ShapeDtypeStruct((n, d), F32),
        scratch_shapes=[pltpu.VMEM((TOP_K, tm * groups, LANES), F32), pltpu.VMEM((TOP_K, tm * groups, LANES), F32),
                        pltpu.SemaphoreType.DMA((2, TOP_K))],
        compiler_params=_cparams(("arbitrary",), est),
        name="moe_combine",
    )(*args)


def _run_group(x, pos0, caches, states, weights):
    (norm_mix_w, w_in, w_sb_o, w_ret_o, gn_ret_w, w_out, norm_ffn_w, ffn_w_gate, ffn_w_up,
     ffn_w_down, router_w, moe_w_gate, moe_w_up, moe_w_down, norm_final_w) = weights
    batch, seq, d = x.shape
    depth = len(w_in)
    w_sb = w_sb_o[0].shape[0]
    wv_ret = w_ret_o[0].shape[0]
    d_in = w_in[0].shape[1]
    wk_ret = (d_in - 3 * w_sb - 2 * wv_ret - 2 * d) // 2
    dims = dict(w_sb=w_sb, wk_ret=wk_ret, wv_ret=wv_ret)
    dk, dv = wk_ret // H_RET, wv_ret // H_RET
    hd = w_sb // H_SB

    h = x.reshape(batch * seq, d)
    kv = None
    finals = []
    if caches is not None:
        kc = caches[0].reshape(depth * batch, -1, hd)
        vc = caches[1].reshape(depth * batch, -1, hd)
    for l in range(depth):
        proj, k_all, v_all = _in_projection(h, norm_mix_w[l], w_in[l], l, depth, kv, w_sb=w_sb)
        kv = (k_all, v_all)
        if caches is None:
            o_sb = _sb_attention_prompt(proj, batch, seq, w_sb=w_sb)
            s0 = jnp.zeros((batch, H_RET, dk, dv), F32)
        else:
            o_sb = _sb_attention_sample(proj, kc, vc, l, batch, seq, w_sb=w_sb)
            s0 = states[l].astype(F32)
        o_r, s_fin = _retention(proj, gn_ret_w[l], s0, batch, seq, pos0, **dims)
        finals.append(s_fin)
        h = _merge(o_sb, o_r, proj, h, w_sb_o[l], w_ret_o[l], w_out[l], **dims)
        fw = norm_final_w if l == depth - 1 else None
        if l % 2 == 0:
            h = _ffn(h, norm_ffn_w[l], ffn_w_gate[l // 2], ffn_w_up[l // 2], ffn_w_down[l // 2], fw)
        else:
            h = _moe(h, norm_ffn_w[l], router_w[l // 2], moe_w_gate[l // 2], moe_w_up[l // 2],
                     moe_w_down[l // 2], fw)
    y = h.reshape(batch, seq, d)
    new_k = kv[0].reshape(depth, batch, seq, H_SB, hd)
    new_v = kv[1].reshape(depth, batch, seq, H_SB, hd)
    return y, new_k, new_v, jnp.stack(finals)


def kernel(x_prompt, x_sample, cache_sb_k, cache_sb_v, state_ret, norm_mix_w, w_in, w_sb_o, w_ret_o,
           gn_ret_w, w_out, norm_ffn_w, ffn_w_gate, ffn_w_up, ffn_w_down, router_w, moe_w_gate,
           moe_w_up, moe_w_down, norm_final_w):
    per_layer = lambda w: [w[l].astype(BF16) for l in range(w.shape[0])]
    weights = (
        norm_mix_w, per_layer(w_in), per_layer(w_sb_o), per_layer(w_ret_o), gn_ret_w,
        per_layer(w_out), norm_ffn_w, per_layer(ffn_w_gate), per_layer(ffn_w_up),
        per_layer(ffn_w_down), router_w,
        per_layer(moe_w_gate), per_layer(moe_w_up), per_layer(moe_w_down), norm_final_w,
    )
    past = cache_sb_k.shape[2]
    y_p, k_p, v_p, r_p = _run_group(x_prompt, 0.0, None, None, weights)
    y_s, k_s, v_s, r_s = _run_group(x_sample, float(past), (cache_sb_k, cache_sb_v), state_ret, weights)
    return (y_p, y_s, k_p, v_p, r_p, k_s, v_s, r_s)
```
